```python
import jax, jax.numpy as jnp
from jax import lax
import numpy as np

D_MODEL = 1024
BATCH = 4
SEQ = 4096
DEPTH = 2
DEC_BATCH = 8
DEC_SEQ = 16
PAST_LEN = 2048

CHUNK = 64
N_MIXERS = 2
N_A = (DEPTH + 1) // 2
N_B = DEPTH // 2
A_HEADS = 16
A_KV_HEADS = 4
A_HEAD_DIM = D_MODEL // A_HEADS
A_GROUP = A_HEADS // A_KV_HEADS
IDX_HEADS = 8
IDX_DIM = 64
TOPK_MAX = 256
Q_BLOCK = 128
A_SPLITS = [A_HEADS * A_HEAD_DIM, A_KV_HEADS * A_HEAD_DIM, A_KV_HEADS * A_HEAD_DIM,
            IDX_HEADS * IDX_DIM, IDX_DIM, IDX_HEADS]
A_IN = sum(A_SPLITS)
GLA_HEADS = 4
GLA_DK = D_MODEL // 2 // GLA_HEADS
GLA_DV = D_MODEL // GLA_HEADS
GLA_RANK = 16
GLA_TAU = 16.0
GLA_CHUNK = 64
B_SPLITS = [GLA_HEADS * GLA_DK, GLA_HEADS * GLA_DK, GLA_HEADS * GLA_DV, D_MODEL, GLA_RANK]
B_IN = sum(B_SPLITS)
D_FF = 4 * D_MODEL
EPS = 1e-6

kernel_name = 'dsa_gla_streaming_encoder_step'


def rmsnorm(x, g):
    xf = x.astype(jnp.float32)
    y = xf * lax.rsqrt(jnp.mean(xf * xf, axis=-1, keepdims=True) + EPS)
    return (y * g.astype(jnp.float32)).astype(x.dtype)


def split_cols(h, sizes):
    offs = np.cumsum(sizes)[:-1].tolist()
    return jnp.split(h, offs, axis=-1)


def sq_relu_mlp(x, w_up, w_down):
    return jnp.square(jax.nn.relu(x @ w_up)) @ w_down


def dsa_project(x, w_in):
    B, T, _ = x.shape
    q, k, v, qi, ki, wt = split_cols(x @ w_in, A_SPLITS)
    return (q.reshape(B, T, A_KV_HEADS, A_GROUP, A_HEAD_DIM),
            k.reshape(B, T, A_KV_HEADS, A_HEAD_DIM),
            v.reshape(B, T, A_KV_HEADS, A_HEAD_DIM),
            qi.reshape(B, T, IDX_HEADS, IDX_DIM), ki, wt)


def dsa_attend(q, qi, wt, q_pos, k, v, ki, topk):
    L = k.shape[1]
    s = jnp.einsum('bqhd,bld->bqhl', qi, ki).astype(jnp.float32) * (IDX_DIM ** -0.5)
    w = wt.astype(jnp.float32) * (IDX_HEADS ** -0.5)
    score = jnp.einsum('bqhl,bqh->bql', jax.nn.relu(s), w)
    limit = (q_pos // CHUNK + 1) * CHUNK
    adm = jnp.arange(L)[None, :] < limit[:, None]
    score = jnp.where(adm[None], score, -jnp.inf)
    _, idx = lax.top_k(score, topk)
    valid = idx < limit[None, :, None]
    k_sel = jax.vmap(lambda kk, ii: kk[ii])(k, idx)
    v_sel = jax.vmap(lambda vv, ii: vv[ii])(v, idx)
    logits = jnp.einsum('bqhgd,bqjhd->bqhgj', q, k_sel).astype(jnp.float32) * (A_HEAD_DIM ** -0.5)
    logits = jnp.where(valid[:, :, None, None, :], logits, -jnp.inf)
    p = jax.nn.softmax(logits, axis=-1).astype(v.dtype)
    o = jnp.einsum('bqhgj,bqjhd->bqhgd', p, v_sel)
    return o.reshape(o.shape[0], o.shape[1], A_HEADS * A_HEAD_DIM)


def dsa_prompt(x, w_in, w_o):
    B, S, _ = x.shape
    q, k, v, qi, ki, wt = dsa_project(x, w_in)
    topk = min(TOPK_MAX, S // 4)
    nb = S // Q_BLOCK

    def blocks(a):
        return jnp.moveaxis(a.reshape((B, nb, Q_BLOCK) + a.shape[2:]), 1, 0)

    pos = jnp.arange(S, dtype=jnp.int32).reshape(nb, Q_BLOCK)
    o = lax.map(lambda a: dsa_attend(a[0], a[1], a[2], a[3], k, v, ki, topk),
                (blocks(q), blocks(qi), blocks(wt), pos))
    o = jnp.moveaxis(o, 0, 1).reshape(B, S, A_HEADS * A_HEAD_DIM)
    return o @ w_o, k, v, ki


def dsa_sample(x, c_k, c_v, c_ki, w_in, w_o):
    B, T, _ = x.shape
    q, k, v, qi, ki, wt = dsa_project(x, w_in)
    past = c_k.shape[1]
    k_all = jnp.concatenate([c_k.astype(k.dtype), k], axis=1)
    v_all = jnp.concatenate([c_v.astype(v.dtype), v], axis=1)
    ki_all = jnp.concatenate([c_ki.astype(ki.dtype), ki], axis=1)
    topk = min(TOPK_MAX, (past + T) // 4)
    pos = past + jnp.arange(T, dtype=jnp.int32)
    o = dsa_attend(q, qi, wt, pos, k_all, v_all, ki_all, topk)
    return o @ w_o, k, v, ki


def gla_project(x, w_in, w_a2, b_a):
    B, T, _ = x.shape
    q, k, v, g, a = split_cols(x @ w_in, B_SPLITS)
    la = jax.nn.log_sigmoid((a @ w_a2 + b_a).astype(jnp.float32)) / GLA_TAU
    shp = (B, T, GLA_HEADS, GLA_DK)
    return (q.reshape(shp).astype(jnp.float32) * (GLA_DK ** -0.5),
            k.reshape(shp).astype(jnp.float32),
            v.reshape(B, T, GLA_HEADS, GLA_DV).astype(jnp.float32),
            la.reshape(shp), g)


def gla_chunk(S0, q, k, v, la):
    C = q.shape[1]
    b = jnp.cumsum(la, axis=1)
    causal = jnp.tril(jnp.ones((C, C), dtype=bool))
    diff = b[:, :, None] - b[:, None, :]
    decay = jnp.exp(jnp.where(causal[None, :, :, None, None], diff, -jnp.inf))
    A = jnp.einsum('bthk,bshk,btshk->bhts', q, k, decay)
    o = (jnp.einsum('bthk,bhkv->bthv', q * jnp.exp(b), S0)
         + jnp.einsum('bhts,bshv->bthv', A, v))
    bC = b[:, -1]
    S1 = (jnp.exp(bC)[..., None] * S0
          + jnp.einsum('bshk,bshv->bhkv', k * jnp.exp(bC[:, None] - b), v))
    return S1, o


def gla_output(o, g, norm_g, w_o):
    B, T = o.shape[0], o.shape[1]
    o = rmsnorm(o, norm_g).reshape(B, T, GLA_HEADS * GLA_DV)
    o = o * jax.nn.silu(g.astype(jnp.float32))
    return o.astype(g.dtype) @ w_o


def gla_prompt(x, w_in, w_a2, b_a, norm_g, w_o):
    B, S, _ = x.shape
    q, k, v, la, g = gla_project(x, w_in, w_a2, b_a)
    n = S // GLA_CHUNK

    def chunks(a):
        return jnp.moveaxis(a.reshape((B, n, GLA_CHUNK) + a.shape[2:]), 1, 0)

    S0 = jnp.zeros((B, GLA_HEADS, GLA_DK, GLA_DV), jnp.float32)
    S_fin, o = lax.scan(lambda s, xs: gla_chunk(s, xs[0], xs[1], xs[2], xs[3]), S0,
                        (chunks(q), chunks(k), chunks(v), chunks(la)))
    o = jnp.moveaxis(o, 0, 1).reshape(B, S, GLA_HEADS, GLA_DV)
    return gla_output(o, g, norm_g, w_o), S_fin.astype(x.dtype)


def gla_sample(x, state, w_in, w_a2, b_a, norm_g, w_o):
    q, k, v, la, g = gla_project(x, w_in, w_a2, b_a)
    S1, o = gla_chunk(state.astype(jnp.float32), q, k, v, la)
    return gla_output(o, g, norm_g, w_o), S1.astype(x.dtype)


def setup_inputs(seed: int = 0) -> dict:
    key = jax.random.key(seed)
    ks = jax.random.split(key, 20)
    f32 = jnp.float32
    nrm = lambda k, shp, s: jax.random.normal(k, shp, f32) * s
    return {
        'x_prompt': nrm(ks[0], (BATCH, SEQ, D_MODEL), 1.0),
        'x_sample': nrm(ks[1], (DEC_BATCH, DEC_SEQ, D_MODEL), 1.0),
        'cache_k': nrm(ks[2], (N_A, DEC_BATCH, PAST_LEN, A_KV_HEADS, A_HEAD_DIM), 1.0),
        'cache_v': nrm(ks[3], (N_A, DEC_BATCH, PAST_LEN, A_KV_HEADS, A_HEAD_DIM), 1.0),
        'cache_kidx': nrm(ks[4], (N_A, DEC_BATCH, PAST_LEN, IDX_DIM), 1.0),
        'state_gla': nrm(ks[5], (N_B, DEC_BATCH, GLA_HEADS, GLA_DK, GLA_DV), 1.0),
        'norm_mix': 1.0 + nrm(ks[6], (DEPTH, D_MODEL), 0.01),
        'norm_ffn': 1.0 + nrm(ks[7], (DEPTH, D_MODEL), 0.01),
        'norm_final': 1.0 + nrm(ks[8], (D_MODEL,), 0.01),
        'a_w_in': nrm(ks[9], (N_A, D_MODEL, A_IN), D_MODEL ** -0.5),
        'a_w_o': nrm(ks[10], (N_A, A_HEADS * A_HEAD_DIM, D_MODEL), (A_HEADS * A_HEAD_DIM) ** -0.5),
        'b_w_in': nrm(ks[11], (N_B, D_MODEL, B_IN), D_MODEL ** -0.5),
        'b_w_a2': nrm(ks[12], (N_B, GLA_RANK, GLA_HEADS * GLA_DK), GLA_RANK ** -0.5),
        'b_b_a': nrm(ks[13], (N_B, GLA_HEADS * GLA_DK), 0.1),
        'b_norm': 1.0 + nrm(ks[14], (N_B, GLA_DV), 0.01),
        'b_w_o': nrm(ks[15], (N_B, GLA_HEADS * GLA_DV, D_MODEL), (GLA_HEADS * GLA_DV) ** -0.5),
        'ffn_w_up': nrm(ks[16], (DEPTH, D_MODEL, D_FF), D_MODEL ** -0.5),
        'ffn_w_down': nrm(ks[17], (DEPTH, D_FF, D_MODEL), D_FF ** -0.5),
    }


def reference(x_prompt, x_sample, cache_k, cache_v, cache_kidx, state_gla,
              norm_mix, norm_ffn, norm_final, a_w_in, a_w_o,
              b_w_in, b_w_a2, b_b_a, b_norm, b_w_o, ffn_w_up, ffn_w_down):
    hp, hs = x_prompt, x_sample
    kp_l, vp_l, ip_l, sp_l = [], [], [], []
    ks_l, vs_l, is_l, ss_l = [], [], [], []
    for i in range(DEPTH):
        j = i // N_MIXERS
        up = rmsnorm(hp, norm_mix[i])
        us = rmsnorm(hs, norm_mix[i])
        if i % N_MIXERS == 0:
            mp, k1, v1, i1 = dsa_prompt(up, a_w_in[j], a_w_o[j])
            ms, k2, v2, i2 = dsa_sample(us, cache_k[j], cache_v[j], cache_kidx[j], a_w_in[j], a_w_o[j])
            kp_l.append(k1); vp_l.append(v1); ip_l.append(i1)
            ks_l.append(k2); vs_l.append(v2); is_l.append(i2)
        else:
            mp, s1 = gla_prompt(up, b_w_in[j], b_w_a2[j], b_b_a[j], b_norm[j], b_w_o[j])
            ms, s2 = gla_sample(us, state_gla[j], b_w_in[j], b_w_a2[j], b_b_a[j], b_norm[j], b_w_o[j])
            sp_l.append(s1); ss_l.append(s2)
        hp = hp + mp
        hs = hs + ms
        hp = hp + sq_relu_mlp(rmsnorm(hp, norm_ffn[i]), ffn_w_up[i], ffn_w_down[i])
        hs = hs + sq_relu_mlp(rmsnorm(hs, norm_ffn[i]), ffn_w_up[i], ffn_w_down[i])
    y_prompt = rmsnorm(hp, norm_final)
    y_sample = rmsnorm(hs, norm_final)
    k_prompt = jnp.stack(kp_l)
    v_prompt = jnp.stack(vp_l)
    kidx_prompt = jnp.stack(ip_l)
    gla_prompt_state = jnp.stack(sp_l)
    k_sample = jnp.stack(ks_l)
    v_sample = jnp.stack(vs_l)
    kidx_sample = jnp.stack(is_l)
    gla_sample_state = jnp.stack(ss_l)
    return (y_prompt, y_sample, k_prompt, v_prompt, kidx_prompt, gla_prompt_state,
            k_sample, v_sample, kidx_sample, gla_sample_state)
```

```python
import functools

import jax
import jax.numpy as jnp
from jax import lax
from jax.experimental import pallas as pl
from jax.experimental.pallas import tpu as pltpu

F32 = jnp.float32
BF16 = jnp.bfloat16
I32 = jnp.int32

EPS = 1e-6
STREAM_CHUNK = 64
TOPK_MAX = 256
A_HEADS = 16
A_KV_HEADS = 4
A_GROUP = A_HEADS // A_KV_HEADS
A_HEAD_DIM = 64
IDX_HEADS = 8
IDX_DIM = 64
GLA_HEADS = 4
GLA_DK = 128
GLA_DV = 256
GLA_RANK = 16
GLA_TAU = 16.0

LANES = 128
SUBLANES = 8
INT_MIN = -(2 ** 31)
NEG_BIG = -1e30

DSA_Q_TILE = LANES
DSA_KEY_CHUNK = 256
GLA_DIAG = 8
VMEM_LIMIT = 56 * 1024 * 1024


def _params(*sem):
    return pltpu.CompilerParams(dimension_semantics=sem, vmem_limit_bytes=VMEM_LIMIT)


def _rms(x, g):
    ms = jnp.mean(x * x, axis=-1, keepdims=True)
    return x * lax.rsqrt(ms + EPS) * g


def _dot(a, b):
    return jnp.dot(a, b, preferred_element_type=F32)


def _dot_nt(a, b):
    return lax.dot_general(a, b, (((1,), (1,)), ((), ())), preferred_element_type=F32)


_NQ = A_HEADS * A_HEAD_DIM
_NKV = A_KV_HEADS * A_HEAD_DIM
_NQI = IDX_HEADS * IDX_DIM
_T_ROWS = _NQ + _NQI + _NKV + 16


def _dsa_proj_kernel(x_ref, g_ref, wt_ref, wn_ref,
                     qT_ref, qiT_ref, wT_ref, vT_ref, k_ref, v_ref, ki_ref, kg_ref, kib_ref, *, tq):
    xn = _rms(x_ref[0], g_ref[...]).astype(BF16)
    tm = xn.shape[0]
    yT = _dot_nt(wt_ref[...], xn)
    y = _dot(xn, wn_ref[...])
    for j in range(tm // tq):
        sl = slice(j * tq, (j + 1) * tq)
        qT_ref[0, j] = yT[0:_NQ, sl].astype(BF16)
        qiT_ref[0, j] = yT[_NQ:_NQ + _NQI, sl].astype(BF16)
        wT_ref[0, j] = yT[_NQ + _NQI + _NKV:_NQ + _NQI + _NKV + IDX_HEADS, sl]
    vT_ref[0] = yT[_NQ + _NQI:_NQ + _NQI + _NKV, :].astype(BF16)
    k = y[:, 0:_NKV]
    ki = y[:, 2 * _NKV:2 * _NKV + IDX_DIM]
    k_ref[0] = k
    v_ref[0] = y[:, _NKV:2 * _NKV]
    ki_ref[0] = ki
    for h in range(A_KV_HEADS):
        kg_ref[0, h] = k[:, h * A_HEAD_DIM:(h + 1) * A_HEAD_DIM].astype(BF16)
    kib_ref[0] = ki.astype(BF16)


def _dsa_project(x, gain, w_in, *, tm, tq):
    B, S, D = x.shape
    o = 0
    parts = []
    for n in (_NQ, _NKV, _NKV, _NQI, IDX_DIM, IDX_HEADS):
        parts.append(w_in[:, o:o + n])
        o += n
    w_q, w_k, w_v, w_qi, w_ki, w_wt = parts
    wt_all = jnp.concatenate(
        [w_q * (A_HEAD_DIM ** -0.5), w_qi, w_v, w_wt, jnp.zeros((D, 16 - IDX_HEADS), F32)], axis=1).T.astype(BF16)
    wn_all = jnp.concatenate([w_k, w_v, w_ki], axis=1).astype(BF16)
    nq = S // tq
    nt = S // tm
    jq = tm // tq
    out_shape = (
        jax.ShapeDtypeStruct((B, nq, _NQ, tq), BF16),
        jax.ShapeDtypeStruct((B, nq, _NQI, tq), BF16),
        jax.ShapeDtypeStruct((B, nq, IDX_HEADS, tq), F32),
        jax.ShapeDtypeStruct((B, _NKV, S), BF16),
        jax.ShapeDtypeStruct((B, S, _NKV), F32),
        jax.ShapeDtypeStruct((B, S, _NKV), F32),
        jax.ShapeDtypeStruct((B, S, IDX_DIM), F32),
        jax.ShapeDtypeStruct((B, A_KV_HEADS, S, A_HEAD_DIM), BF16),
        jax.ShapeDtypeStruct((B, S, IDX_DIM), BF16),
    )
    out_specs = (
        pl.BlockSpec((1, jq, _NQ, tq), lambda b, i: (b, i, 0, 0)),
        pl.BlockSpec((1, jq, _NQI, tq), lambda b, i: (b, i, 0, 0)),
        pl.BlockSpec((1, jq, IDX_HEADS, tq), lambda b, i: (b, i, 0, 0)),
        pl.BlockSpec((1, _NKV, tm), lambda b, i: (b, 0, i)),
        pl.BlockSpec((1, tm, _NKV), lambda b, i: (b, i, 0)),
        pl.BlockSpec((1, tm, _NKV), lambda b, i: (b, i, 0)),
        pl.BlockSpec((1, tm, IDX_DIM), lambda b, i: (b, i, 0)),
        pl.BlockSpec((1, A_KV_HEADS, tm, A_HEAD_DIM), lambda b, i: (b, 0, i, 0)),
        pl.BlockSpec((1, tm, IDX_DIM), lambda b, i: (b, i, 0)),
    )
    in_specs = [
        pl.BlockSpec((1, tm, D), lambda b, i: (b, i, 0)),
        pl.BlockSpec((1, D), lambda b, i: (0, 0)),
        pl.BlockSpec((_T_ROWS, D), lambda b, i: (0, 0)),
        pl.BlockSpec((D, 2 * _NKV + IDX_DIM), lambda b, i: (0, 0)),
    ]
    return pl.pallas_call(
        functools.partial(_dsa_proj_kernel, tq=tq),
        grid=(B, nt), in_specs=in_specs, out_specs=out_specs, out_shape=out_shape,
        compiler_params=_params("arbitrary", "arbitrary"),
    )(x, gain.reshape(1, D), wt_all, wn_all)


def _dsa_attn_kernel(lim_ref, x_ref, qT_ref, qiT_ref, wT_ref, kg_ref, vT_ref, kib_ref, wo_ref,
                     out_ref, key_scr, tie_scr, oT_scr, res_scr,
                     *, topk, causal, n_chunks_total, rows_out):
    tq = qT_ref.shape[-1]
    lc = DSA_KEY_CHUNK
    if causal:
        n_chunks = jnp.minimum(((pl.program_id(1) + 1) * tq + lc - 1) // lc, n_chunks_total)
    else:
        n_chunks = n_chunks_total
    lim = lim_ref[0]

    def chunk_start(c):
        return pl.multiple_of(c * lc, lc)

    def key_index(c):
        return c * lc + lax.broadcasted_iota(I32, (lc, tq), 0)

    qi_all = jnp.concatenate(
        [qiT_ref[0, 0, h * IDX_DIM:(h + 1) * IDX_DIM, :] for h in range(IDX_HEADS)], axis=1)
    w_rows = wT_ref[0, 0] * ((IDX_DIM ** -0.5) * (IDX_HEADS ** -0.5))

    def score_body(c, carry):
        off = chunk_start(c)
        s = _dot(kib_ref[0, pl.ds(off, lc), :], qi_all)
        score = jnp.maximum(s[:, 0:tq], 0.0) * w_rows[0:1, :]
        for h in range(1, IDX_HEADS):
            score = score + jnp.maximum(s[:, h * tq:(h + 1) * tq], 0.0) * w_rows[h:h + 1, :]
        bits = pltpu.bitcast(score, I32)
        key = jnp.where(bits < 0, -(bits & 0x7FFFFFFF), bits)
        key = jnp.where(key_index(c) < lim, key, INT_MIN)
        key_scr[pl.ds(off, lc), :] = key
        return carry

    lax.fori_loop(0, n_chunks, score_body, 0)

    def count(pred):
        def body(c, acc):
            kk = key_scr[pl.ds(chunk_start(c), lc), :]
            m = jnp.where(pred(kk, c), 1, 0).astype(I32)
            return acc + jnp.sum(m.reshape(lc // SUBLANES, SUBLANES, tq), axis=0)
        acc = lax.fori_loop(0, n_chunks, body, jnp.zeros((SUBLANES, tq), I32))
        return jnp.sum(acc, axis=0, keepdims=True)

    def bit_body(i, carry):
        thr, cge = carry
        cand = thr + jnp.left_shift(jnp.int32(1), 31 - i)
        c = count(lambda kk, _: kk >= cand)
        ok = c >= topk
        return jnp.where(ok, cand, thr), jnp.where(ok, c, cge)

    thr0 = jnp.full((1, tq), INT_MIN, I32)
    cge0 = jnp.zeros((1, tq), I32) + n_chunks * lc
    thr, cge = lax.fori_loop(0, 32, bit_body, (thr0, cge0))

    n_idx_bits = max(1, (n_chunks_total * lc - 1).bit_length())
    tie_scr[...] = jnp.full((SUBLANES, tq), 2 ** 30, I32)
    surplus = jnp.logical_and(cge > topk, thr > INT_MIN)

    @pl.when(jnp.max(jnp.where(surplus, 1, 0)) > 0)
    def _():
        cgt = count(lambda kk, _: kk > thr)
        want = topk - cgt
        bound = jnp.zeros((1, tq), I32)
        for bit in range(n_idx_bits - 1, -1, -1):
            cand = bound + (1 << bit)
            c = count(lambda kk, cc: jnp.logical_and(kk == thr, key_index(cc) < cand))
            bound = jnp.where(c < want, cand, bound)
        tie_scr[...] = jnp.broadcast_to(bound, (SUBLANES, tq))

    tie_bound = tie_scr[0:1, :]

    def bias_body(c, carry):
        off = chunk_start(c)
        kk = key_scr[pl.ds(off, lc), :]
        sel = jnp.where(kk > thr, 1, jnp.where(kk == thr, jnp.where(key_index(c) <= tie_bound, 1, 0), 0))
        sel = jnp.where(kk > INT_MIN, sel, 0)
        key_scr[pl.ds(off, lc), :] = sel
        return carry

    lax.fori_loop(0, n_chunks, bias_body, 0)

    def bias_chunk(c):
        return jnp.where(key_scr[pl.ds(chunk_start(c), lc), :] > 0, 0.0, -jnp.inf).astype(F32)

    gw = A_GROUP * tq
    for g in range(A_KV_HEADS):
        qg = jnp.concatenate(
            [qT_ref[0, 0, (g * A_GROUP + h) * A_HEAD_DIM:(g * A_GROUP + h + 1) * A_HEAD_DIM, :]
             for h in range(A_GROUP)], axis=1)

        def att_body(c, carry, g=g, qg=qg):
            m, l, acc = carry
            off = chunk_start(c)
            s = _dot(kg_ref[0, g, pl.ds(off, lc), :], qg)
            bias = bias_chunk(c)
            s = s + jnp.concatenate([bias] * A_GROUP, axis=1)
            mn = jnp.maximum(m, jnp.max(s, axis=0, keepdims=True))
            alpha = jnp.exp(m - mn)
            p = jnp.exp(s - mn)
            l = alpha * l + jnp.sum(p, axis=0, keepdims=True)
            vt = vT_ref[0, g * A_HEAD_DIM:(g + 1) * A_HEAD_DIM, pl.ds(off, lc)]
            acc = alpha * acc + _dot(vt, p.astype(BF16))
            return mn, l, acc

        m0 = jnp.full((1, gw), NEG_BIG, F32)
        l0 = jnp.zeros((1, gw), F32)
        a0 = jnp.zeros((A_HEAD_DIM, gw), F32)
        _, l, acc = lax.fori_loop(0, n_chunks, att_body, (m0, l0, a0))
        o = acc / l
        for h in range(A_GROUP):
            r0 = (g * A_GROUP + h) * A_HEAD_DIM
            oT_scr[r0:r0 + A_HEAD_DIM, :] = o[:, h * tq:(h + 1) * tq]

    o = jnp.transpose(oT_scr[...]).astype(BF16)
    res = x_ref[0] + _dot(o, wo_ref[...])
    if rows_out == tq:
        out_ref[0] = res
    else:
        res_scr[...] = res
        r0 = pl.multiple_of(pl.program_id(0) * rows_out, rows_out)
        out_ref[0] = res_scr[pl.ds(r0, rows_out), :]


def _dsa_attend(x, qT, qiT, wT, lim, kg, vT, kib, wo, *, topk, causal, shared_queries, rows_out):
    Bk, _, L, _ = kg.shape
    D = x.shape[-1]
    tq = qT.shape[-1]
    nq = qT.shape[1]
    assert L % DSA_KEY_CHUNK == 0
    if shared_queries:
        grid = (Bk, 1)
        qmap = lambda b, j: (0, 0, 0, 0)
        xmap = lambda b, j: (0, 0, 0)
        omap = lambda b, j: (0, b, 0)
        lmap = lambda b, j: (0, 0, 0)
    else:
        grid = (Bk, nq)
        qmap = lambda b, j: (b, j, 0, 0)
        xmap = lambda b, j: (b, j, 0)
        omap = xmap
        lmap = lambda b, j: (j, 0, 0)
    in_specs = [
        pl.BlockSpec((1, 1, tq), lmap),
        pl.BlockSpec((1, tq, D), xmap),
        pl.BlockSpec((1, 1, _NQ, tq), qmap),
        pl.BlockSpec((1, 1, _NQI, tq), qmap),
        pl.BlockSpec((1, 1, IDX_HEADS, tq), qmap),
        pl.BlockSpec((1, A_KV_HEADS, L, A_HEAD_DIM), lambda b, j: (b, 0, 0, 0)),
        pl.BlockSpec((1, _NKV, L), lambda b, j: (b, 0, 0)),
        pl.BlockSpec((1, L, IDX_DIM), lambda b, j: (b, 0, 0)),
        pl.BlockSpec((_NQ, D), lambda b, j: (0, 0)),
    ]
    kern = functools.partial(_dsa_attn_kernel, topk=topk, causal=causal,
                             n_chunks_total=L // DSA_KEY_CHUNK, rows_out=rows_out)
    return pl.pallas_call(
        kern, grid=grid, in_specs=in_specs,
        out_specs=pl.BlockSpec((1, rows_out, D), omap),
        out_shape=jax.ShapeDtypeStruct(x.shape, F32),
        scratch_shapes=[pltpu.VMEM((L, tq), I32), pltpu.VMEM((SUBLANES, tq), I32),
                        pltpu.VMEM((_NQ, tq), F32), pltpu.VMEM((tq, D), F32)],
        compiler_params=_params("arbitrary", "arbitrary"),
    )(lim, x, qT, qiT, wT, kg, vT, kib, wo)


MLP_FF_CHUNK = 1024


def _mlp_kernel(x_ref, g_ref, wu_ref, wd_ref, gf_ref, out_ref, *, final_norm):
    x = x_ref[...]
    xn = _rms(x, g_ref[...]).astype(BF16)
    acc = x
    for f in range(0, wu_ref.shape[1], MLP_FF_CHUNK):
        h = jnp.maximum(_dot(xn, wu_ref[:, f:f + MLP_FF_CHUNK]), 0.0)
        acc = acc + _dot((h * h).astype(BF16), wd_ref[f:f + MLP_FF_CHUNK, :])
    if final_norm:
        acc = _rms(acc, gf_ref[...])
    out_ref[...] = acc


def _mlp(x, gain, w_up, w_down, gain_final, *, tm, final_norm):
    M, D = x.shape
    FF = w_up.shape[1]
    const = lambda i: (0, 0)
    return pl.pallas_call(
        functools.partial(_mlp_kernel, final_norm=final_norm),
        grid=(M // tm,),
        in_specs=[
            pl.BlockSpec((tm, D), lambda i: (i, 0)),
            pl.BlockSpec((1, D), const),
            pl.BlockSpec((D, FF), const, pipeline_mode=pl.Buffered(1)),
            pl.BlockSpec((FF, D), const, pipeline_mode=pl.Buffered(1)),
            pl.BlockSpec((1, D), const),
        ],
        out_specs=pl.BlockSpec((tm, D), lambda i: (i, 0)),
        out_shape=jax.ShapeDtypeStruct((M, D), F32),
        compiler_params=_params("arbitrary"),
    )(x, gain.reshape(1, D), w_up.astype(BF16), w_down.astype(BF16), gain_final.reshape(1, D))


_GQK = GLA_HEADS * GLA_DK
_GV = GLA_HEADS * GLA_DV


def _gla_proj_kernel(x_ref, g_ref, w_ref, wa_ref, ba_ref, q_ref, k_ref, v_ref, gate_ref, la_ref):
    xn = _rms(x_ref[...], g_ref[...]).astype(BF16)
    y = _dot(xn, w_ref[...])
    D = gate_ref.shape[-1]
    q_ref[...] = y[:, 0:_GQK]
    k_ref[...] = y[:, _GQK:2 * _GQK]
    v_ref[...] = y[:, 2 * _GQK:2 * _GQK + _GV].astype(BF16)
    o = 2 * _GQK + _GV
    gate_ref[...] = y[:, o:o + D]
    a = y[:, o + D:o + D + GLA_RANK].astype(BF16)
    z = _dot(a, wa_ref[...]) + ba_ref[...]
    la_ref[...] = (jnp.minimum(z, 0.0) - jnp.log1p(jnp.exp(-jnp.abs(z)))) * (1.0 / GLA_TAU)


def _gla_project(x, gain, w_in, w_a2, b_a, *, tm):
    M, D = x.shape
    N = w_in.shape[1]
    const = lambda i: (0, 0)
    row = lambda i: (i, 0)
    return pl.pallas_call(
        _gla_proj_kernel, grid=(M // tm,),
        in_specs=[
            pl.BlockSpec((tm, D), row),
            pl.BlockSpec((1, D), const),
            pl.BlockSpec((D, N), const),
            pl.BlockSpec((GLA_RANK, _GQK), const),
            pl.BlockSpec((1, _GQK), const),
        ],
        out_specs=(
            pl.BlockSpec((tm, _GQK), row), pl.BlockSpec((tm, _GQK), row), pl.BlockSpec((tm, _GV), row),
            pl.BlockSpec((tm, D), row), pl.BlockSpec((tm, _GQK), row)),
        out_shape=(
            jax.ShapeDtypeStruct((M, _GQK), F32), jax.ShapeDtypeStruct((M, _GQK), F32),
            jax.ShapeDtypeStruct((M, _GV), BF16), jax.ShapeDtypeStruct((M, D), F32),
            jax.ShapeDtypeStruct((M, _GQK), F32)),
        compiler_params=_params("arbitrary"),
    )(x, gain.reshape(1, D), w_in.astype(BF16), w_a2.astype(BF16), b_a.reshape(1, _GQK))


def _split3(a):
    a0 = a.astype(BF16)
    r = a - a0.astype(F32)
    a1 = r.astype(BF16)
    a2 = (r - a1.astype(F32)).astype(BF16)
    return a0, a1, a2


def _gla_kernel(x_ref, q_ref, k_ref, v_ref, gate_ref, la_ref, s0_ref, ng_ref, wo_ref,
                out_ref, sout_ref, s_scr, a_scr, o_scr):
    c = pl.program_id(1)
    C = q_ref.shape[1]

    @pl.when(c == 0)
    def _():
        s_scr[...] = s0_ref[0]

    la = la_ref[0]
    row = lax.broadcasted_iota(I32, (C, C), 0)
    col = lax.broadcasted_iota(I32, (C, C), 1)
    tri = jnp.where(row >= col, 1.0, 0.0).astype(BF16)
    l0, l1, l2 = _split3(la)
    b = _dot(tri, l0) + _dot(tri, l1) + _dot(tri, l2)
    q = q_ref[0] * (GLA_DK ** -0.5)
    k = k_ref[0]
    trow = lax.broadcasted_iota(I32, (C, 1), 0)

    def block_ref(values, size, pick):
        v3 = values.reshape(C // size, size, values.shape[-1])
        return jnp.broadcast_to(v3[:, pick:pick + 1, :], v3.shape).reshape(values.shape)

    def accumulate(qh, kh, mask, first):
        for h in range(GLA_HEADS):
            sl = slice(h * GLA_DK, (h + 1) * GLA_DK)
            blk = jnp.where(mask, _dot_nt(qh[:, sl], kh[:, sl]), 0.0)
            if first:
                a_scr[h] = blk
            else:
                a_scr[h] = a_scr[h] + blk

    d = min(GLA_DIAG, C)
    before = block_ref(b - la, d, 0)
    qh = (q * jnp.exp(b - before)).astype(BF16)
    kh = (k * jnp.exp(before - b)).astype(BF16)
    sh = d.bit_length() - 1
    accumulate(qh, kh, jnp.logical_and((row >> sh) == (col >> sh), row >= col), True)
    half = d
    while half < C:
        upper = (trow & (2 * half - 1)) >= half
        split = block_ref(b, 2 * half, half - 1)
        qh = (q * jnp.exp(jnp.where(upper, b - split, -jnp.inf))).astype(BF16)
        kh = (k * jnp.exp(jnp.where(upper, -jnp.inf, split - b))).astype(BF16)
        sh = (2 * half).bit_length() - 1
        accumulate(qh, kh, (row >> sh) == (col >> sh), False)
        half *= 2

    b_end = b[C - 1:C, :]
    q_in = (q * jnp.exp(b)).astype(BF16)
    k_out = k * jnp.exp(b_end - b)
    ng = ng_ref[...]
    for h in range(GLA_HEADS):
        sl = slice(h * GLA_DK, (h + 1) * GLA_DK)
        vh = v_ref[0, :, h * GLA_DV:(h + 1) * GLA_DV]
        s_old = s_scr[h]
        o = _dot(q_in[:, sl], s_old.astype(BF16)) + _dot(a_scr[h].astype(BF16), vh)
        dec = jnp.transpose(jnp.broadcast_to(jnp.exp(b_end[:, sl]), (GLA_DK, GLA_DK)))
        dec = jnp.concatenate([dec] * (GLA_DV // GLA_DK), axis=1)
        s_scr[h] = dec * s_old + _dot(jnp.transpose(k_out[:, sl]).astype(BF16), vh)
        o = _rms(o, ng)
        gt = gate_ref[0, :, h * GLA_DV:(h + 1) * GLA_DV]
        o_scr[:, h * GLA_DV:(h + 1) * GLA_DV] = (o * (gt / (1.0 + jnp.exp(-gt)))).astype(BF16)

    out_ref[0] = x_ref[0] + _dot(o_scr[...], wo_ref[...])

    @pl.when(c == pl.num_programs(1) - 1)
    def _():
        sout_ref[0] = s_scr[...]


def _gla(x, q, k, v, gate, la, s0, norm_g, w_o, *, chunk):
    B, T, D = x.shape
    n = T // chunk
    tok = lambda b, c: (b, c, 0)
    const = lambda b, c: (0, 0)
    st = lambda b, c: (b, 0, 0, 0)
    return pl.pallas_call(
        _gla_kernel, grid=(B, n),
        in_specs=[
            pl.BlockSpec((1, chunk, D), tok),
            pl.BlockSpec((1, chunk, _GQK), tok),
            pl.BlockSpec((1, chunk, _GQK), tok),
            pl.BlockSpec((1, chunk, _GV), tok),
            pl.BlockSpec((1, chunk, D), tok),
            pl.BlockSpec((1, chunk, _GQK), tok),
            pl.BlockSpec((1, GLA_HEADS, GLA_DK, GLA_DV), st),
            pl.BlockSpec((1, GLA_DV), const),
            pl.BlockSpec((_GV, D), const),
        ],
        out_specs=(pl.BlockSpec((1, chunk, D), tok), pl.BlockSpec((1, GLA_HEADS, GLA_DK, GLA_DV), st)),
        out_shape=(jax.ShapeDtypeStruct((B, T, D), F32),
                   jax.ShapeDtypeStruct((B, GLA_HEADS, GLA_DK, GLA_DV), F32)),
        scratch_shapes=[pltpu.VMEM((GLA_HEADS, GLA_DK, GLA_DV), F32),
                        pltpu.VMEM((GLA_HEADS, chunk, chunk), F32),
                        pltpu.VMEM((chunk, _GV), BF16)],
        compiler_params=_params("arbitrary", "arbitrary"),
    )(x, q, k, v, gate, la, s0, norm_g.reshape(1, GLA_DV), w_o.astype(BF16))


def _pick_tile(n, candidates):
    for t in candidates:
        if n % t == 0:
            return t
    return n


def _round_up(n, m):
    return (n + m - 1) // m * m


def kernel(x_prompt, x_sample, cache_k, cache_v, cache_kidx, state_gla, norm_mix, norm_ffn, norm_final,
           a_w_in, a_w_o, b_w_in, b_w_a2, b_b_a, b_norm, b_w_o, ffn_w_up, ffn_w_down):
    B, S, D = x_prompt.shape
    Bs, Ts, _ = x_sample.shape
    past = cache_k.shape[2]
    Ms = Bs * Ts
    tq = DSA_Q_TILE
    assert S % tq == 0 and Ms == tq and Ts % SUBLANES == 0

    wo_a = a_w_o[0].astype(BF16)
    tm = _pick_tile(S, (512, 256, 128))
    qT, qiT, wT, vT, k_p, v_p, ki_p, kg, kib = _dsa_project(x_prompt, norm_mix[0], a_w_in[0], tm=tm, tq=tq)
    pos = jnp.arange(S, dtype=I32)
    lim_p = ((pos // STREAM_CHUNK + 1) * STREAM_CHUNK).reshape(S // tq, 1, tq)
    L_p = _round_up(S, DSA_KEY_CHUNK)
    if L_p != S:
        pad = L_p - S
        kg = jnp.pad(kg, ((0, 0), (0, 0), (0, pad), (0, 0)))
        vT = jnp.pad(vT, ((0, 0), (0, 0), (0, pad)))
        kib = jnp.pad(kib, ((0, 0), (0, pad), (0, 0)))
    hp = _dsa_attend(x_prompt, qT, qiT, wT, lim_p, kg, vT, kib, wo_a,
                     topk=min(TOPK_MAX, S // 4), causal=True, shared_queries=False, rows_out=tq)
    xs = x_sample.reshape(1, Ms, D)
    qT_s, qiT_s, wT_s, _, k_s, v_s, ki_s, _, _ = _dsa_project(xs, norm_mix[0], a_w_in[0], tm=Ms, tq=tq)
    k_s = k_s.reshape(Bs, Ts, _NKV)
    v_s = v_s.reshape(Bs, Ts, _NKV)
    ki_s = ki_s.reshape(Bs, Ts, IDX_DIM)
    L_real = past + Ts
    L_s = _round_up(L_real, DSA_KEY_CHUNK)
    padn = L_s - L_real
    k_all = jnp.concatenate([cache_k[0].reshape(Bs, past, _NKV), k_s, jnp.zeros((Bs, padn, _NKV), F32)], axis=1)
    v_all = jnp.concatenate([cache_v[0].reshape(Bs, past, _NKV), v_s, jnp.zeros((Bs, padn, _NKV), F32)], axis=1)
    ki_all = jnp.concatenate([cache_kidx[0], ki_s, jnp.zeros((Bs, padn, IDX_DIM), F32)], axis=1)
    kg_s = k_all.reshape(Bs, L_s, A_KV_HEADS, A_HEAD_DIM).transpose(0, 2, 1, 3).astype(BF16)
    vT_s = v_all.transpose(0, 2, 1).astype(BF16)
    kib_s = ki_all.astype(BF16)
    pos_s = past + jnp.arange(Ts, dtype=I32)
    lim_s = jnp.minimum((pos_s // STREAM_CHUNK + 1) * STREAM_CHUNK, L_real)
    lim_s = jnp.tile(lim_s, Bs).reshape(1, 1, Ms)
    hs = _dsa_attend(xs, qT_s, qiT_s, wT_s, lim_s, kg_s, vT_s, kib_s, wo_a,
                     topk=min(TOPK_MAX, L_real // 4), causal=False, shared_queries=True, rows_out=Ts)

    tmm = _pick_tile(B * S, (512, 256, 128))
    hp = _mlp(hp.reshape(B * S, D), norm_ffn[0], ffn_w_up[0], ffn_w_down[0], norm_final, tm=tmm, final_norm=False)
    hs = _mlp(hs.reshape(Ms, D), norm_ffn[0], ffn_w_up[0], ffn_w_down[0], norm_final, tm=Ms, final_norm=False)

    q, k, v, gate, la = _gla_project(hp, norm_mix[1], b_w_in[0], b_w_a2[0], b_b_a[0], tm=tmm)
    chunk = _pick_tile(S, (256, 128, 64))
    r3 = lambda a: a.reshape(B, S, a.shape[-1])
    s0 = jnp.zeros((B, GLA_HEADS, GLA_DK, GLA_DV), F32)
    hp, s_p = _gla(hp.reshape(B, S, D), r3(q), r3(k), r3(v), r3(gate), r3(la), s0, b_norm[0], b_w_o[0], chunk=chunk)
    q, k, v, gate, la = _gla_project(hs, norm_mix[1], b_w_in[0], b_w_a2[0], b_b_a[0], tm=Ms)
    r3 = lambda a: a.reshape(Bs, Ts, a.shape[-1])
    hs, s_s = _gla(hs.reshape(Bs, Ts, D), r3(q), r3(k), r3(v), r3(gate), r3(la), state_gla[0], b_norm[0], b_w_o[0],
                   chunk=Ts)

    y_p = _mlp(hp.reshape(B * S, D), norm_ffn[1], ffn_w_up[1], ffn_w_down[1], norm_final, tm=tmm, final_norm=True)
    y_s = _mlp(hs.reshape(Ms, D), norm_ffn[1], ffn_w_up[1], ffn_w_down[1], norm_final, tm=Ms, final_norm=True)

    return (y_p.reshape(B, S, D), y_s.reshape(Bs, Ts, D),
            k_p.reshape(1, B, S, A_KV_HEADS, A_HEAD_DIM), v_p.reshape(1, B, S, A_KV_HEADS, A_HEAD_DIM),
            ki_p.reshape(1, B, S, IDX_DIM), s_p[None],
            k_s.reshape(1, Bs, Ts, A_KV_HEADS, A_HEAD_DIM), v_s.reshape(1, Bs, Ts, A_KV_HEADS, A_HEAD_DIM),
            ki_s.reshape(1, Bs, Ts, IDX_DIM), s_s[None])
```

```python
import functools

import jax
import jax.numpy as jnp
from jax import lax
from jax.experimental import pallas as pl
from jax.experimental.pallas import tpu as pltpu

F32 = jnp.float32
BF16 = jnp.bfloat16
I32 = jnp.int32

EPS = 1e-6
STREAM_CHUNK = 64
TOPK_MAX = 256
A_HEADS = 16
A_KV_HEADS = 4
A_GROUP = A_HEADS // A_KV_HEADS
A_HEAD_DIM = 64
IDX_HEADS = 8
IDX_DIM = 64
GLA_HEADS = 4
GLA_DK = 128
GLA_DV = 256
GLA_RANK = 16
GLA_TAU = 16.0

LANES = 128
SUBLANES = 8
INT_MIN = -(2 ** 31)
NEG_BIG = -1e30

DSA_Q_TILE = LANES
DSA_KEY_CHUNK = 256
GLA_DIAG = 8
VMEM_LIMIT = 56 * 1024 * 1024


def _params(*sem):
    return pltpu.CompilerParams(dimension_semantics=sem, vmem_limit_bytes=VMEM_LIMIT)


def _rms(x, g):
    ms = jnp.mean(x * x, axis=-1, keepdims=True)
    return x * lax.rsqrt(ms + EPS) * g


def _dot(a, b):
    return jnp.dot(a, b, preferred_element_type=F32)


def _dot_nt(a, b):
    return lax.dot_general(a, b, (((1,), (1,)), ((), ())), preferred_element_type=F32)


_NQ = A_HEADS * A_HEAD_DIM
_NKV = A_KV_HEADS * A_HEAD_DIM
_NQI = IDX_HEADS * IDX_DIM
_T_ROWS = _NQ + _NQI + _NKV + 16
BF16_SUBLANES = 16
V_AUG_ROWS = A_HEAD_DIM + BF16_SUBLANES
LOG2E = 1.4426950408889634
MASKED = -1e30
M_INIT = -1e29


def _dsa_proj_kernel(x_ref, g_ref, wt_ref, wn_ref,
                     qT_ref, qiT_ref, wT_ref, vT_ref, k_ref, v_ref, ki_ref, kg_ref, kib_ref, *, tq):
    xn = _rms(x_ref[0], g_ref[...]).astype(BF16)
    tm = xn.shape[0]
    yT = _dot_nt(wt_ref[...], xn)
    y = _dot(xn, wn_ref[...])
    for j in range(tm // tq):
        sl = slice(j * tq, (j + 1) * tq)
        qT_ref[0, j] = yT[0:_NQ, sl].astype(BF16)
        qiT_ref[0, j] = yT[_NQ:_NQ + _NQI, sl].astype(BF16)
        wT_ref[0, j] = yT[_NQ + _NQI + _NKV:_NQ + _NQI + _NKV + IDX_HEADS, sl]
    for h in range(A_KV_HEADS):
        r0 = _NQ + _NQI + h * A_HEAD_DIM
        vT_ref[0, h, 0:A_HEAD_DIM, :] = yT[r0:r0 + A_HEAD_DIM, :].astype(BF16)
        vT_ref[0, h, A_HEAD_DIM:V_AUG_ROWS, :] = jnp.ones((V_AUG_ROWS - A_HEAD_DIM, tm), BF16)
    k = y[:, 0:_NKV]
    ki = y[:, 2 * _NKV:2 * _NKV + IDX_DIM]
    k_ref[0] = k
    v_ref[0] = y[:, _NKV:2 * _NKV]
    ki_ref[0] = ki
    for h in range(A_KV_HEADS):
        kg_ref[0, h] = k[:, h * A_HEAD_DIM:(h + 1) * A_HEAD_DIM].astype(BF16)
    kib_ref[0] = ki.astype(BF16)


def _dsa_project(x, gain, w_in, *, tm, tq):
    B, S, D = x.shape
    o = 0
    parts = []
    for n in (_NQ, _NKV, _NKV, _NQI, IDX_DIM, IDX_HEADS):
        parts.append(w_in[:, o:o + n])
        o += n
    w_q, w_k, w_v, w_qi, w_ki, w_wt = parts
    wt_all = jnp.concatenate(
        [w_q * (A_HEAD_DIM ** -0.5 * LOG2E), w_qi, w_v, w_wt, jnp.zeros((D, 16 - IDX_HEADS), F32)],
        axis=1).T.astype(BF16)
    wn_all = jnp.concatenate([w_k, w_v, w_ki], axis=1).astype(BF16)
    nq = S // tq
    nt = S // tm
    jq = tm // tq
    out_shape = (
        jax.ShapeDtypeStruct((B, nq, _NQ, tq), BF16),
        jax.ShapeDtypeStruct((B, nq, _NQI, tq), BF16),
        jax.ShapeDtypeStruct((B, nq, IDX_HEADS, tq), F32),
        jax.ShapeDtypeStruct((B, A_KV_HEADS, V_AUG_ROWS, S), BF16),
        jax.ShapeDtypeStruct((B, S, _NKV), F32),
        jax.ShapeDtypeStruct((B, S, _NKV), F32),
        jax.ShapeDtypeStruct((B, S, IDX_DIM), F32),
        jax.ShapeDtypeStruct((B, A_KV_HEADS, S, A_HEAD_DIM), BF16),
        jax.ShapeDtypeStruct((B, S, IDX_DIM), BF16),
    )
    out_specs = (
        pl.BlockSpec((1, jq, _NQ, tq), lambda b, i: (b, i, 0, 0)),
        pl.BlockSpec((1, jq, _NQI, tq), lambda b, i: (b, i, 0, 0)),
        pl.BlockSpec((1, jq, IDX_HEADS, tq), lambda b, i: (b, i, 0, 0)),
        pl.BlockSpec((1, A_KV_HEADS, V_AUG_ROWS, tm), lambda b, i: (b, 0, 0, i)),
        pl.BlockSpec((1, tm, _NKV), lambda b, i: (b, i, 0)),
        pl.BlockSpec((1, tm, _NKV), lambda b, i: (b, i, 0)),
        pl.BlockSpec((1, tm, IDX_DIM), lambda b, i: (b, i, 0)),
        pl.BlockSpec((1, A_KV_HEADS, tm, A_HEAD_DIM), lambda b, i: (b, 0, i, 0)),
        pl.BlockSpec((1, tm, IDX_DIM), lambda b, i: (b, i, 0)),
    )
    in_specs = [
        pl.BlockSpec((1, tm, D), lambda b, i: (b, i, 0)),
        pl.BlockSpec((1, D), lambda b, i: (0, 0)),
        pl.BlockSpec((_T_ROWS, D), lambda b, i: (0, 0)),
        pl.BlockSpec((D, 2 * _NKV + IDX_DIM), lambda b, i: (0, 0)),
    ]
    return pl.pallas_call(
        functools.partial(_dsa_proj_kernel, tq=tq), name="dsa_project",
        grid=(B, nt), in_specs=in_specs, out_specs=out_specs, out_shape=out_shape,
        compiler_params=_params("arbitrary", "arbitrary"),
    )(x, gain.reshape(1, D), wt_all, wn_all)


def _dsa_attn_kernel(lim_ref, x_ref, qT_ref, qiT_ref, wT_ref, kg_ref, vT_ref, kib_ref, wo_ref,
                     out_ref, key_scr, tie_scr, oT_scr, res_scr, acc_scr, bias_scr, sa_scr, sb_scr,
                     *, topk, causal, n_chunks_total, rows_out):
    tq = qT_ref.shape[-1]
    lc = DSA_KEY_CHUNK
    if causal:
        n_chunks = jnp.minimum(2 * (((pl.program_id(1) + 1) * tq + 2 * lc - 1) // (2 * lc)), n_chunks_total)
    else:
        n_chunks = n_chunks_total
    lim = lim_ref[0]

    def chunk_start(c):
        return pl.multiple_of(c * lc, lc)

    def key_index(c):
        return c * lc + lax.broadcasted_iota(I32, (lc, tq), 0)

    qi_all = jnp.concatenate(
        [qiT_ref[0, 0, h * IDX_DIM:(h + 1) * IDX_DIM, :] for h in range(IDX_HEADS)], axis=1)
    w_rows = wT_ref[0, 0] * ((IDX_DIM ** -0.5) * (IDX_HEADS ** -0.5))

    def score_body(c, carry):
        off = chunk_start(c)
        s = _dot(kib_ref[0, pl.ds(off, lc), :], qi_all)
        score = jnp.maximum(s[:, 0:tq], 0.0) * w_rows[0:1, :]
        for h in range(1, IDX_HEADS):
            score = score + jnp.maximum(s[:, h * tq:(h + 1) * tq], 0.0) * w_rows[h:h + 1, :]
        bits = pltpu.bitcast(score, I32)
        key = jnp.where(bits < 0, -(bits & 0x7FFFFFFF), bits)
        key = jnp.where(key_index(c) < lim, key, INT_MIN)
        key_scr[pl.ds(off, lc), :] = key
        return carry

    lax.fori_loop(0, n_chunks, score_body, 0)

    def count(pred):
        def body(c, acc):
            kk = key_scr[pl.ds(chunk_start(c), lc), :]
            m = jnp.where(pred(kk, c), 1, 0).astype(I32)
            return acc + jnp.sum(m.reshape(lc // SUBLANES, SUBLANES, tq), axis=0)
        acc = lax.fori_loop(0, n_chunks, body, jnp.zeros((SUBLANES, tq), I32))
        return jnp.sum(acc, axis=0, keepdims=True)

    def bit_body(i, carry):
        thr, cge = carry
        cand = thr + jnp.left_shift(jnp.int32(1), 31 - i)
        c = count(lambda kk, _: kk >= cand)
        ok = c >= topk
        return jnp.where(ok, cand, thr), jnp.where(ok, c, cge)

    thr0 = jnp.full((1, tq), INT_MIN, I32)
    cge0 = jnp.zeros((1, tq), I32) + n_chunks * lc
    thr, cge = lax.fori_loop(0, 32, bit_body, (thr0, cge0))

    n_idx_bits = max(1, (n_chunks_total * lc - 1).bit_length())
    tie_scr[...] = jnp.full((SUBLANES, tq), 2 ** 30, I32)
    surplus = jnp.logical_and(cge > topk, thr > INT_MIN)

    @pl.when(jnp.max(jnp.where(surplus, 1, 0)) > 0)
    def _():
        cgt = count(lambda kk, _: kk > thr)
        want = topk - cgt
        bound = jnp.zeros((1, tq), I32)
        for bit in range(n_idx_bits - 1, -1, -1):
            cand = bound + (1 << bit)
            c = count(lambda kk, cc: jnp.logical_and(kk == thr, key_index(cc) < cand))
            bound = jnp.where(c < want, cand, bound)
        tie_scr[...] = jnp.broadcast_to(bound, (SUBLANES, tq))

    tie_bound = tie_scr[0:1, :]

    def bias_body(c, carry):
        off = chunk_start(c)
        kk = key_scr[pl.ds(off, lc), :]
        tie = jnp.where(key_index(c) <= tie_bound, 0.0, MASKED)
        b = jnp.where(kk > thr, 0.0, jnp.where(kk == thr, tie, MASKED))
        b = jnp.where(kk > INT_MIN, b, MASKED)
        bias_scr[pl.ds(off, lc), :] = b.astype(BF16)
        return carry

    lax.fori_loop(0, n_chunks, bias_body, 0)

    gw = A_GROUP * tq
    eye = jnp.where(lax.broadcasted_iota(I32, (tq, gw), 0) == (lax.broadcasted_iota(I32, (tq, gw), 1) & (tq - 1)),
                    1.0, 0.0).astype(BF16)
    q_aug = [jnp.concatenate(
        [eye, jnp.concatenate(
            [qT_ref[0, 0, (g * A_GROUP + h) * A_HEAD_DIM:(g * A_GROUP + h + 1) * A_HEAD_DIM, :]
             for h in range(A_GROUP)], axis=1)], axis=0) for g in range(A_KV_HEADS)]
    acc_scr[...] = jnp.zeros(acc_scr.shape, F32)

    def logits(c, g):
        off = chunk_start(c)
        k_aug = jnp.concatenate([bias_scr[pl.ds(off, lc), :], kg_ref[0, g, pl.ds(off, lc), :]], axis=1)
        return _dot(k_aug, q_aug[g])

    def softmax_pv(s, c, g, m):
        mn = jnp.maximum(m, jnp.max(s, axis=0, keepdims=True))
        alpha = jnp.exp2(m - mn)
        p = jnp.exp2(s - mn).astype(BF16)
        vt = vT_ref[0, g, :, pl.ds(chunk_start(c), lc)]
        acc_scr[g] = alpha * acc_scr[g] + _dot(vt, p)
        return mn

    for g in range(A_KV_HEADS):
        sa_scr[g] = logits(0, g)

    def att_body(i, ms):
        c0 = 2 * i
        ms = list(ms)
        for g in range(A_KV_HEADS):
            sb_scr[g] = logits(c0 + 1, g)
            ms[g] = softmax_pv(sa_scr[g], c0, g, ms[g])
        c2 = jnp.minimum(c0 + 2, n_chunks - 1)
        for g in range(A_KV_HEADS):
            sa_scr[g] = logits(c2, g)
            ms[g] = softmax_pv(sb_scr[g], c0 + 1, g, ms[g])
        return tuple(ms)

    m0 = tuple(jnp.full((1, gw), M_INIT, F32) for _ in range(A_KV_HEADS))
    lax.fori_loop(0, n_chunks // 2, att_body, m0)
    for g in range(A_KV_HEADS):
        acc = acc_scr[g]
        o = acc[0:A_HEAD_DIM, :] / acc[A_HEAD_DIM:A_HEAD_DIM + 1, :]
        for h in range(A_GROUP):
            r0 = (g * A_GROUP + h) * A_HEAD_DIM
            oT_scr[r0:r0 + A_HEAD_DIM, :] = o[:, h * tq:(h + 1) * tq]

    o = jnp.transpose(oT_scr[...]).astype(BF16)
    res = x_ref[0] + _dot(o, wo_ref[...])
    if rows_out == tq:
        out_ref[0] = res
    else:
        res_scr[...] = res
        r0 = pl.multiple_of(pl.program_id(0) * rows_out, rows_out)
        out_ref[0] = res_scr[pl.ds(r0, rows_out), :]


def _dsa_attend(x, qT, qiT, wT, lim, kg, vT, kib, wo, *, topk, causal, shared_queries, rows_out):
    Bk, _, L, _ = kg.shape
    D = x.shape[-1]
    tq = qT.shape[-1]
    nq = qT.shape[1]
    assert L % (2 * DSA_KEY_CHUNK) == 0
    if shared_queries:
        grid = (Bk, 1)
        qmap = lambda b, j: (0, 0, 0, 0)
        xmap = lambda b, j: (0, 0, 0)
        omap = lambda b, j: (0, b, 0)
        lmap = lambda b, j: (0, 0, 0)
    else:
        grid = (Bk, nq)
        qmap = lambda b, j: (b, j, 0, 0)
        xmap = lambda b, j: (b, j, 0)
        omap = xmap
        lmap = lambda b, j: (j, 0, 0)
    in_specs = [
        pl.BlockSpec((1, 1, tq), lmap),
        pl.BlockSpec((1, tq, D), xmap),
        pl.BlockSpec((1, 1, _NQ, tq), qmap),
        pl.BlockSpec((1, 1, _NQI, tq), qmap),
        pl.BlockSpec((1, 1, IDX_HEADS, tq), qmap),
        pl.BlockSpec((1, A_KV_HEADS, L, A_HEAD_DIM), lambda b, j: (b, 0, 0, 0)),
        pl.BlockSpec((1, A_KV_HEADS, V_AUG_ROWS, L), lambda b, j: (b, 0, 0, 0)),
        pl.BlockSpec((1, L, IDX_DIM), lambda b, j: (b, 0, 0)),
        pl.BlockSpec((_NQ, D), lambda b, j: (0, 0)),
    ]
    kern = functools.partial(_dsa_attn_kernel, topk=topk, causal=causal,
                             n_chunks_total=L // DSA_KEY_CHUNK, rows_out=rows_out)
    return pl.pallas_call(
        kern, grid=grid, in_specs=in_specs, name="dsa_attend",
        out_specs=pl.BlockSpec((1, rows_out, D), omap),
        out_shape=jax.ShapeDtypeStruct(x.shape, F32),
        scratch_shapes=[pltpu.VMEM((L, tq), I32), pltpu.VMEM((SUBLANES, tq), I32),
                        pltpu.VMEM((_NQ, tq), F32), pltpu.VMEM((tq, D), F32),
                        pltpu.VMEM((A_KV_HEADS, V_AUG_ROWS, A_GROUP * tq), F32),
                        pltpu.VMEM((L, tq), BF16),
                        pltpu.VMEM((A_KV_HEADS, DSA_KEY_CHUNK, A_GROUP * tq), F32),
                        pltpu.VMEM((A_KV_HEADS, DSA_KEY_CHUNK, A_GROUP * tq), F32)],
        compiler_params=_params("arbitrary", "arbitrary"),
    )(lim, x, qT, qiT, wT, kg, vT, kib, wo)


MLP_FF_CHUNK = 1024


def _mlp_kernel(x_ref, g_ref, wu_ref, wd_ref, gf_ref, out_ref, *, final_norm):
    x = x_ref[...]
    xn = _rms(x, g_ref[...]).astype(BF16)
    acc = x
    for f in range(0, wu_ref.shape[1], MLP_FF_CHUNK):
        h = jnp.maximum(_dot(xn, wu_ref[:, f:f + MLP_FF_CHUNK]), 0.0)
        acc = acc + _dot((h * h).astype(BF16), wd_ref[f:f + MLP_FF_CHUNK, :])
    if final_norm:
        acc = _rms(acc, gf_ref[...])
    out_ref[...] = acc


def _mlp(x, gain, w_up, w_down, gain_final, *, tm, final_norm):
    M, D = x.shape
    FF = w_up.shape[1]
    const = lambda i: (0, 0)
    return pl.pallas_call(
        functools.partial(_mlp_kernel, final_norm=final_norm), name="mlp",
        grid=(M // tm,),
        in_specs=[
            pl.BlockSpec((tm, D), lambda i: (i, 0)),
            pl.BlockSpec((1, D), const),
            pl.BlockSpec((D, FF), const, pipeline_mode=pl.Buffered(1)),
            pl.BlockSpec((FF, D), const, pipeline_mode=pl.Buffered(1)),
            pl.BlockSpec((1, D), const),
        ],
        out_specs=pl.BlockSpec((tm, D), lambda i: (i, 0)),
        out_shape=jax.ShapeDtypeStruct((M, D), F32),
        compiler_params=_params("arbitrary"),
    )(x, gain.reshape(1, D), w_up.astype(BF16), w_down.astype(BF16), gain_final.reshape(1, D))


_GQK = GLA_HEADS * GLA_DK
_GV = GLA_HEADS * GLA_DV


def _gla_proj_kernel(x_ref, g_ref, w_ref, wa_ref, ba_ref, q_ref, k_ref, v_ref, gate_ref, la_ref):
    xn = _rms(x_ref[...], g_ref[...]).astype(BF16)
    y = _dot(xn, w_ref[...])
    D = gate_ref.shape[-1]
    q_ref[...] = y[:, 0:_GQK]
    k_ref[...] = y[:, _GQK:2 * _GQK]
    v_ref[...] = y[:, 2 * _GQK:2 * _GQK + _GV].astype(BF16)
    o = 2 * _GQK + _GV
    gate_ref[...] = y[:, o:o + D]
    a = y[:, o + D:o + D + GLA_RANK].astype(BF16)
    z = _dot(a, wa_ref[...]) + ba_ref[...]
    la_ref[...] = (jnp.minimum(z, 0.0) - jnp.log1p(jnp.exp(-jnp.abs(z)))) * (1.0 / GLA_TAU)


def _gla_project(x, gain, w_in, w_a2, b_a, *, tm):
    M, D = x.shape
    N = w_in.shape[1]
    const = lambda i: (0, 0)
    row = lambda i: (i, 0)
    return pl.pallas_call(
        _gla_proj_kernel, grid=(M // tm,), name="gla_project",
        in_specs=[
            pl.BlockSpec((tm, D), row),
            pl.BlockSpec((1, D), const),
            pl.BlockSpec((D, N), const),
            pl.BlockSpec((GLA_RANK, _GQK), const),
            pl.BlockSpec((1, _GQK), const),
        ],
        out_specs=(
            pl.BlockSpec((tm, _GQK), row), pl.BlockSpec((tm, _GQK), row), pl.BlockSpec((tm, _GV), row),
            pl.BlockSpec((tm, D), row), pl.BlockSpec((tm, _GQK), row)),
        out_shape=(
            jax.ShapeDtypeStruct((M, _GQK), F32), jax.ShapeDtypeStruct((M, _GQK), F32),
            jax.ShapeDtypeStruct((M, _GV), BF16), jax.ShapeDtypeStruct((M, D), F32),
            jax.ShapeDtypeStruct((M, _GQK), F32)),
        compiler_params=_params("arbitrary"),
    )(x, gain.reshape(1, D), w_in.astype(BF16), w_a2.astype(BF16), b_a.reshape(1, _GQK))


def _split3(a):
    a0 = a.astype(BF16)
    r = a - a0.astype(F32)
    a1 = r.astype(BF16)
    a2 = (r - a1.astype(F32)).astype(BF16)
    return a0, a1, a2


def _gla_kernel(x_ref, q_ref, k_ref, v_ref, gate_ref, la_ref, s0_ref, ng_ref, wo_ref,
                out_ref, sout_ref, s_scr, a_scr, o_scr):
    c = pl.program_id(1)
    C = q_ref.shape[1]

    @pl.when(c == 0)
    def _():
        s_scr[...] = s0_ref[0]

    la = la_ref[0]
    row = lax.broadcasted_iota(I32, (C, C), 0)
    col = lax.broadcasted_iota(I32, (C, C), 1)
    tri = jnp.where(row >= col, 1.0, 0.0).astype(BF16)
    l0, l1, l2 = _split3(la)
    b = _dot(tri, l0) + _dot(tri, l1) + _dot(tri, l2)
    q = q_ref[0] * (GLA_DK ** -0.5)
    k = k_ref[0]
    trow = lax.broadcasted_iota(I32, (C, 1), 0)

    def block_ref(values, size, pick):
        v3 = values.reshape(C // size, size, values.shape[-1])
        return jnp.broadcast_to(v3[:, pick:pick + 1, :], v3.shape).reshape(values.shape)

    def accumulate(qh, kh, mask, first):
        for h in range(GLA_HEADS):
            sl = slice(h * GLA_DK, (h + 1) * GLA_DK)
            blk = jnp.where(mask, _dot_nt(qh[:, sl], kh[:, sl]), 0.0)
            if first:
                a_scr[h] = blk
            else:
                a_scr[h] = a_scr[h] + blk

    d = min(GLA_DIAG, C)
    before = block_ref(b - la, d, 0)
    qh = (q * jnp.exp(b - before)).astype(BF16)
    kh = (k * jnp.exp(before - b)).astype(BF16)
    sh = d.bit_length() - 1
    accumulate(qh, kh, jnp.logical_and((row >> sh) == (col >> sh), row >= col), True)
    half = d
    while half < C:
        upper = (trow & (2 * half - 1)) >= half
        split = block_ref(b, 2 * half, half - 1)
        qh = (q * jnp.exp(jnp.where(upper, b - split, -jnp.inf))).astype(BF16)
        kh = (k * jnp.exp(jnp.where(upper, -jnp.inf, split - b))).astype(BF16)
        sh = (2 * half).bit_length() - 1
        accumulate(qh, kh, (row >> sh) == (col >> sh), False)
        half *= 2

    b_end = b[C - 1:C, :]
    q_in = (q * jnp.exp(b)).astype(BF16)
    k_out = k * jnp.exp(b_end - b)
    ng = ng_ref[...]
    for h in range(GLA_HEADS):
        sl = slice(h * GLA_DK, (h + 1) * GLA_DK)
        vh = v_ref[0, :, h * GLA_DV:(h + 1) * GLA_DV]
        s_old = s_scr[h]
        o = _dot(q_in[:, sl], s_old.astype(BF16)) + _dot(a_scr[h].astype(BF16), vh)
        dec = jnp.transpose(jnp.broadcast_to(jnp.exp(b_end[:, sl]), (GLA_DK, GLA_DK)))
        dec = jnp.concatenate([dec] * (GLA_DV // GLA_DK), axis=1)
        s_scr[h] = dec * s_old + _dot(jnp.transpose(k_out[:, sl]).astype(BF16), vh)
        o = _rms(o, ng)
        gt = gate_ref[0, :, h * GLA_DV:(h + 1) * GLA_DV]
        o_scr[:, h * GLA_DV:(h + 1) * GLA_DV] = (o * (gt / (1.0 + jnp.exp(-gt)))).astype(BF16)

    out_ref[0] = x_ref[0] + _dot(o_scr[...], wo_ref[...])

    @pl.when(c == pl.num_programs(1) - 1)
    def _():
        sout_ref[0] = s_scr[...]


def _gla(x, q, k, v, gate, la, s0, norm_g, w_o, *, chunk):
    B, T, D = x.shape
    n = T // chunk
    tok = lambda b, c: (b, c, 0)
    const = lambda b, c: (0, 0)
    st = lambda b, c: (b, 0, 0, 0)
    return pl.pallas_call(
        _gla_kernel, grid=(B, n), name="gla",
        in_specs=[
            pl.BlockSpec((1, chunk, D), tok),
            pl.BlockSpec((1, chunk, _GQK), tok),
            pl.BlockSpec((1, chunk, _GQK), tok),
            pl.BlockSpec((1, chunk, _GV), tok),
            pl.BlockSpec((1, chunk, D), tok),
            pl.BlockSpec((1, chunk, _GQK), tok),
            pl.BlockSpec((1, GLA_HEADS, GLA_DK, GLA_DV), st),
            pl.BlockSpec((1, GLA_DV), const),
            pl.BlockSpec((_GV, D), const),
        ],
        out_specs=(pl.BlockSpec((1, chunk, D), tok), pl.BlockSpec((1, GLA_HEADS, GLA_DK, GLA_DV), st)),
        out_shape=(jax.ShapeDtypeStruct((B, T, D), F32),
                   jax.ShapeDtypeStruct((B, GLA_HEADS, GLA_DK, GLA_DV), F32)),
        scratch_shapes=[pltpu.VMEM((GLA_HEADS, GLA_DK, GLA_DV), F32),
                        pltpu.VMEM((GLA_HEADS, chunk, chunk), F32),
                        pltpu.VMEM((chunk, _GV), BF16)],
        compiler_params=_params("arbitrary", "arbitrary"),
    )(x, q, k, v, gate, la, s0, norm_g.reshape(1, GLA_DV), w_o.astype(BF16))


def _pick_tile(n, candidates):
    for t in candidates:
        if n % t == 0:
            return t
    return n


def _round_up(n, m):
    return (n + m - 1) // m * m


def kernel(x_prompt, x_sample, cache_k, cache_v, cache_kidx, state_gla, norm_mix, norm_ffn, norm_final,
           a_w_in, a_w_o, b_w_in, b_w_a2, b_b_a, b_norm, b_w_o, ffn_w_up, ffn_w_down):
    B, S, D = x_prompt.shape
    Bs, Ts, _ = x_sample.shape
    past = cache_k.shape[2]
    Ms = Bs * Ts
    tq = DSA_Q_TILE
    assert S % tq == 0 and Ms == tq and Ts % SUBLANES == 0

    wo_a = a_w_o[0].astype(BF16)
    tm = _pick_tile(S, (512, 256, 128))
    qT, qiT, wT, vT, k_p, v_p, ki_p, kg, kib = _dsa_project(x_prompt, norm_mix[0], a_w_in[0], tm=tm, tq=tq)
    pos = jnp.arange(S, dtype=I32)
    lim_p = ((pos // STREAM_CHUNK + 1) * STREAM_CHUNK).reshape(S // tq, 1, tq)
    L_p = _round_up(S, 2 * DSA_KEY_CHUNK)
    if L_p != S:
        pad = L_p - S
        kg = jnp.pad(kg, ((0, 0), (0, 0), (0, pad), (0, 0)))
        vT = jnp.pad(vT, ((0, 0), (0, 0), (0, 0), (0, pad)))
        kib = jnp.pad(kib, ((0, 0), (0, pad), (0, 0)))
    hp = _dsa_attend(x_prompt, qT, qiT, wT, lim_p, kg, vT, kib, wo_a,
                     topk=min(TOPK_MAX, S // 4), causal=True, shared_queries=False, rows_out=tq)
    xs = x_sample.reshape(1, Ms, D)
    qT_s, qiT_s, wT_s, _, k_s, v_s, ki_s, _, _ = _dsa_project(xs, norm_mix[0], a_w_in[0], tm=Ms, tq=tq)
    k_s = k_s.reshape(Bs, Ts, _NKV)
    v_s = v_s.reshape(Bs, Ts, _NKV)
    ki_s = ki_s.reshape(Bs, Ts, IDX_DIM)
    L_real = past + Ts
    L_s = _round_up(L_real, 2 * DSA_KEY_CHUNK)
    padn = L_s - L_real
    k_all = jnp.concatenate([cache_k[0].reshape(Bs, past, _NKV), k_s, jnp.zeros((Bs, padn, _NKV), F32)], axis=1)
    v_all = jnp.concatenate([cache_v[0].reshape(Bs, past, _NKV), v_s, jnp.zeros((Bs, padn, _NKV), F32)], axis=1)
    ki_all = jnp.concatenate([cache_kidx[0], ki_s, jnp.zeros((Bs, padn, IDX_DIM), F32)], axis=1)
    kg_s = k_all.reshape(Bs, L_s, A_KV_HEADS, A_HEAD_DIM).transpose(0, 2, 1, 3).astype(BF16)
    vT_s = jnp.concatenate(
        [v_all.reshape(Bs, L_s, A_KV_HEADS, A_HEAD_DIM).transpose(0, 2, 3, 1),
         jnp.ones((Bs, A_KV_HEADS, V_AUG_ROWS - A_HEAD_DIM, L_s), F32)], axis=2).astype(BF16)
    kib_s = ki_all.astype(BF16)
    pos_s = past + jnp.arange(Ts, dtype=I32)
    lim_s = jnp.minimum((pos_s // STREAM_CHUNK + 1) * STREAM_CHUNK, L_real)
    lim_s = jnp.tile(lim_s, Bs).reshape(1, 1, Ms)
    hs = _dsa_attend(xs, qT_s, qiT_s, wT_s, lim_s, kg_s, vT_s, kib_s, wo_a,
                     topk=min(TOPK_MAX, L_real // 4), causal=False, shared_queries=True, rows_out=Ts)

    tmm = _pick_tile(B * S, (512, 256, 128))
    hp = _mlp(hp.reshape(B * S, D), norm_ffn[0], ffn_w_up[0], ffn_w_down[0], norm_final, tm=tmm, final_norm=False)
    hs = _mlp(hs.reshape(Ms, D), norm_ffn[0], ffn_w_up[0], ffn_w_down[0], norm_final, tm=Ms, final_norm=False)

    q, k, v, gate, la = _gla_project(hp, norm_mix[1], b_w_in[0], b_w_a2[0], b_b_a[0], tm=tmm)
    chunk = _pick_tile(S, (256, 128, 64))
    r3 = lambda a: a.reshape(B, S, a.shape[-1])
    s0 = jnp.zeros((B, GLA_HEADS, GLA_DK, GLA_DV), F32)
    hp, s_p = _gla(hp.reshape(B, S, D), r3(q), r3(k), r3(v), r3(gate), r3(la), s0, b_norm[0], b_w_o[0], chunk=chunk)
    q, k, v, gate, la = _gla_project(hs, norm_mix[1], b_w_in[0], b_w_a2[0], b_b_a[0], tm=Ms)
    r3 = lambda a: a.reshape(Bs, Ts, a.shape[-1])
    hs, s_s = _gla(hs.reshape(Bs, Ts, D), r3(q), r3(k), r3(v), r3(gate), r3(la), state_gla[0], b_norm[0], b_w_o[0],
                   chunk=Ts)

    y_p = _mlp(hp.reshape(B * S, D), norm_ffn[1], ffn_w_up[1], ffn_w_down[1], norm_final, tm=tmm, final_norm=True)
    y_s = _mlp(hs.reshape(Ms, D), norm_ffn[1], ffn_w_up[1], ffn_w_down[1], norm_final, tm=Ms, final_norm=True)

    return (y_p.reshape(B, S, D), y_s.reshape(Bs, Ts, D),
            k_p.reshape(1, B, S, A_KV_HEADS, A_HEAD_DIM), v_p.reshape(1, B, S, A_KV_HEADS, A_HEAD_DIM),
            ki_p.reshape(1, B, S, IDX_DIM), s_p[None],
            k_s.reshape(1, Bs, Ts, A_KV_HEADS, A_HEAD_DIM), v_s.reshape(1, Bs, Ts, A_KV_HEADS, A_HEAD_DIM),
            ki_s.reshape(1, Bs, Ts, IDX_DIM), s_s[None])
```

```python
import functools

import jax
import jax.numpy as jnp
from jax import lax
from jax.experimental import pallas as pl
from jax.experimental.pallas import tpu as pltpu

F32 = jnp.float32
BF16 = jnp.bfloat16
I32 = jnp.int32
I16 = jnp.int16

EPS = 1e-6
STREAM_CHUNK = 64
TOPK_MAX = 256
A_HEADS = 16
A_KV_HEADS = 4
A_GROUP = A_HEADS // A_KV_HEADS
A_HEAD_DIM = 64
IDX_HEADS = 8
IDX_DIM = 64
GLA_HEADS = 4
GLA_DK = 128
GLA_DV = 256
GLA_RANK = 16
GLA_TAU = 16.0

LANES = 128
SUBLANES = 8
INT_MIN = -(2 ** 31)
NEG_BIG = -1e30

DSA_Q_TILE = LANES
DSA_KEY_CHUNK = 256
GLA_DIAG = 8
VMEM_LIMIT = 56 * 1024 * 1024


def _params(*sem):
    return pltpu.CompilerParams(dimension_semantics=sem, vmem_limit_bytes=VMEM_LIMIT)


def _rms(x, g):
    ms = jnp.mean(x * x, axis=-1, keepdims=True)
    return x * lax.rsqrt(ms + EPS) * g


def _dot(a, b):
    return jnp.dot(a, b, preferred_element_type=F32)


def _dot_nt(a, b):
    return lax.dot_general(a, b, (((1,), (1,)), ((), ())), preferred_element_type=F32)


_NQ = A_HEADS * A_HEAD_DIM
_NKV = A_KV_HEADS * A_HEAD_DIM
_NQI = IDX_HEADS * IDX_DIM
_T_ROWS = _NQ + _NQI + _NKV + 16
BF16_SUBLANES = 16
V_AUG_ROWS = A_HEAD_DIM + BF16_SUBLANES
LOG2E = 1.4426950408889634
MASKED = -1e30
M_INIT = -1e29


def _dsa_proj_kernel(x_ref, g_ref, wt_ref, wn_ref,
                     qT_ref, qiT_ref, wT_ref, vT_ref, k_ref, v_ref, ki_ref, kg_ref, kib_ref, *, tq):
    xn = _rms(x_ref[0], g_ref[...]).astype(BF16)
    tm = xn.shape[0]
    yT = _dot_nt(wt_ref[...], xn)
    y = _dot(xn, wn_ref[...])
    for j in range(tm // tq):
        sl = slice(j * tq, (j + 1) * tq)
        qT_ref[0, j] = yT[0:_NQ, sl].astype(BF16)
        qiT_ref[0, j] = yT[_NQ:_NQ + _NQI, sl].astype(BF16)
        wT_ref[0, j] = yT[_NQ + _NQI + _NKV:_NQ + _NQI + _NKV + IDX_HEADS, sl]
    for h in range(A_KV_HEADS):
        r0 = _NQ + _NQI + h * A_HEAD_DIM
        vT_ref[0, h, 0:A_HEAD_DIM, :] = yT[r0:r0 + A_HEAD_DIM, :].astype(BF16)
        vT_ref[0, h, A_HEAD_DIM:V_AUG_ROWS, :] = jnp.ones((V_AUG_ROWS - A_HEAD_DIM, tm), BF16)
    k = y[:, 0:_NKV]
    ki = y[:, 2 * _NKV:2 * _NKV + IDX_DIM]
    k_ref[0] = k
    v_ref[0] = y[:, _NKV:2 * _NKV]
    ki_ref[0] = ki
    for h in range(A_KV_HEADS):
        kg_ref[0, h] = k[:, h * A_HEAD_DIM:(h + 1) * A_HEAD_DIM].astype(BF16)
    kib_ref[0] = ki.astype(BF16)


def _dsa_project(x, gain, w_in, *, tm, tq):
    B, S, D = x.shape
    o = 0
    parts = []
    for n in (_NQ, _NKV, _NKV, _NQI, IDX_DIM, IDX_HEADS):
        parts.append(w_in[:, o:o + n])
        o += n
    w_q, w_k, w_v, w_qi, w_ki, w_wt = parts
    wt_all = jnp.concatenate(
        [w_q * (A_HEAD_DIM ** -0.5 * LOG2E), w_qi, w_v, w_wt, jnp.zeros((D, 16 - IDX_HEADS), F32)],
        axis=1).T.astype(BF16)
    wn_all = jnp.concatenate([w_k, w_v, w_ki], axis=1).astype(BF16)
    nq = S // tq
    nt = S // tm
    jq = tm // tq
    out_shape = (
        jax.ShapeDtypeStruct((B, nq, _NQ, tq), BF16),
        jax.ShapeDtypeStruct((B, nq, _NQI, tq), BF16),
        jax.ShapeDtypeStruct((B, nq, IDX_HEADS, tq), F32),
        jax.ShapeDtypeStruct((B, A_KV_HEADS, V_AUG_ROWS, S), BF16),
        jax.ShapeDtypeStruct((B, S, _NKV), F32),
        jax.ShapeDtypeStruct((B, S, _NKV), F32),
        jax.ShapeDtypeStruct((B, S, IDX_DIM), F32),
        jax.ShapeDtypeStruct((B, A_KV_HEADS, S, A_HEAD_DIM), BF16),
        jax.ShapeDtypeStruct((B, S, IDX_DIM), BF16),
    )
    out_specs = (
        pl.BlockSpec((1, jq, _NQ, tq), lambda b, i: (b, i, 0, 0)),
        pl.BlockSpec((1, jq, _NQI, tq), lambda b, i: (b, i, 0, 0)),
        pl.BlockSpec((1, jq, IDX_HEADS, tq), lambda b, i: (b, i, 0, 0)),
        pl.BlockSpec((1, A_KV_HEADS, V_AUG_ROWS, tm), lambda b, i: (b, 0, 0, i)),
        pl.BlockSpec((1, tm, _NKV), lambda b, i: (b, i, 0)),
        pl.BlockSpec((1, tm, _NKV), lambda b, i: (b, i, 0)),
        pl.BlockSpec((1, tm, IDX_DIM), lambda b, i: (b, i, 0)),
        pl.BlockSpec((1, A_KV_HEADS, tm, A_HEAD_DIM), lambda b, i: (b, 0, i, 0)),
        pl.BlockSpec((1, tm, IDX_DIM), lambda b, i: (b, i, 0)),
    )
    in_specs = [
        pl.BlockSpec((1, tm, D), lambda b, i: (b, i, 0)),
        pl.BlockSpec((1, D), lambda b, i: (0, 0)),
        pl.BlockSpec((_T_ROWS, D), lambda b, i: (0, 0)),
        pl.BlockSpec((D, 2 * _NKV + IDX_DIM), lambda b, i: (0, 0)),
    ]
    return pl.pallas_call(
        functools.partial(_dsa_proj_kernel, tq=tq), name="dsa_project",
        grid=(B, nt), in_specs=in_specs, out_specs=out_specs, out_shape=out_shape,
        compiler_params=_params("arbitrary", "arbitrary"),
    )(x, gain.reshape(1, D), wt_all, wn_all)


def _dsa_attn_kernel(lim_ref, x_ref, qT_ref, qiT_ref, wT_ref, kg_ref, vT_ref, kib_ref, wo_ref,
                     out_ref, key_scr, tie_scr, oT_scr, res_scr, acc_scr, bias_scr, sa_scr, sb_scr, hi_scr, lo_scr,
                     ia_scr, ib_scr,
                     *, topk, causal, n_chunks_total, rows_out):
    tq = qT_ref.shape[-1]
    lc = DSA_KEY_CHUNK
    if causal:
        n_chunks = jnp.minimum(2 * (((pl.program_id(1) + 1) * tq + 2 * lc - 1) // (2 * lc)), n_chunks_total)
    else:
        n_chunks = n_chunks_total
    lim = lim_ref[0]

    def chunk_start(c):
        return pl.multiple_of(c * lc, lc)

    def key_index(c):
        return c * lc + lax.broadcasted_iota(I32, (lc, tq), 0)

    qi_all = jnp.concatenate(
        [qiT_ref[0, 0, h * IDX_DIM:(h + 1) * IDX_DIM, :] for h in range(IDX_HEADS)], axis=1)
    w_rows = wT_ref[0, 0] * ((IDX_DIM ** -0.5) * (IDX_HEADS ** -0.5))

    def idx_logits(c):
        return _dot(kib_ref[0, pl.ds(chunk_start(c), lc), :], qi_all)

    def store_keys(s_ref, c):
        off = chunk_start(c)
        score = jnp.maximum(s_ref[:, 0:tq], 0.0) * w_rows[0:1, :]
        for h in range(1, IDX_HEADS):
            score = score + jnp.maximum(s_ref[:, h * tq:(h + 1) * tq], 0.0) * w_rows[h:h + 1, :]
        bits = pltpu.bitcast(score, I32)
        key = jnp.where(bits < 0, -(bits & 0x7FFFFFFF), bits)
        key = jnp.where(key_index(c) < lim, key, INT_MIN)
        key_scr[pl.ds(off, lc), :] = key
        hi_scr[pl.ds(off, lc), :] = (key >> 16).astype(I16)

    ia_scr[...] = idx_logits(0)

    def score_body(i, carry):
        c0 = 2 * i
        ib_scr[...] = idx_logits(c0 + 1)
        store_keys(ia_scr, c0)
        ia_scr[...] = idx_logits(jnp.minimum(c0 + 2, n_chunks - 1))
        store_keys(ib_scr, c0 + 1)
        return carry

    lax.fori_loop(0, n_chunks // 2, score_body, 0)

    def count16(ref, pred):
        rows = BF16_SUBLANES
        ONE_BF16 = jnp.ones((rows, tq), jnp.bfloat16)
        ZERO_BF16 = jnp.zeros((rows, tq), jnp.bfloat16)

        def body(i, acc):
            blk = ref[pl.ds(pl.multiple_of(i * (2 * lc), 2 * lc), 2 * lc), :]
            flags = [jnp.where(pred(blk[r * rows:(r + 1) * rows, :]), ONE_BF16, ZERO_BF16)
                     for r in range(2 * lc // rows)]
            while len(flags) > 1:
                flags = [a + b for a, b in zip(flags[0::2], flags[1::2])]
            return acc + flags[0].astype(F32)

        acc = lax.fori_loop(0, n_chunks // 2, body, jnp.zeros((rows, tq), F32))
        return jnp.sum(acc, axis=0, keepdims=True).astype(I32)

    def wide16(v):
        return jnp.broadcast_to(v, (BF16_SUBLANES, tq)).astype(I16)

    def search16(ref, need, c_all):
        def bit_body(i, carry):
            t, cnt = carry
            cand = t + jnp.left_shift(jnp.int32(1), 15 - i)
            cw = wide16(cand)
            c = count16(ref, lambda blk: blk >= cw)
            ok = c >= need
            return jnp.where(ok, cand, t), jnp.where(ok, c, cnt)
        return lax.fori_loop(0, 16, bit_body, (jnp.full((1, tq), -32768, I32), c_all))

    def count(pred):
        def body(c, acc):
            kk = key_scr[pl.ds(chunk_start(c), lc), :]
            m = jnp.where(pred(kk, c), 1, 0).astype(I32)
            return acc + jnp.sum(m.reshape(lc // SUBLANES, SUBLANES, tq), axis=0)
        acc = lax.fori_loop(0, n_chunks, body, jnp.zeros((SUBLANES, tq), I32))
        return jnp.sum(acc, axis=0, keepdims=True)

    n_rows = jnp.zeros((1, tq), I32) + n_chunks * lc
    t_hi, c_hi = search16(hi_scr, topk, n_rows)
    t_hi_w = wide16(t_hi)
    c_above = count16(hi_scr, lambda blk: blk > t_hi_w)

    def low_body(c, carry):
        off = chunk_start(c)
        kk = key_scr[pl.ds(off, lc), :]
        low = (kk & 0xFFFF) - 32768
        lo_scr[pl.ds(off, lc), :] = jnp.where((kk >> 16) == t_hi, low, -32768).astype(I16)
        return carry

    lax.fori_loop(0, n_chunks, low_body, 0)
    t_lo, c_lo = search16(lo_scr, topk - c_above, c_hi - c_above)
    thr = jnp.left_shift(t_hi, 16) | (t_lo + 32768)
    cge = c_above + c_lo

    n_idx_bits = max(1, (n_chunks_total * lc - 1).bit_length())
    tie_scr[...] = jnp.full((SUBLANES, tq), 2 ** 30, I32)
    surplus = jnp.logical_and(cge > topk, thr > INT_MIN)

    @pl.when(jnp.max(jnp.where(surplus, 1, 0)) > 0)
    def _():
        cgt = count(lambda kk, _: kk > thr)
        want = topk - cgt
        bound = jnp.zeros((1, tq), I32)
        for bit in range(n_idx_bits - 1, -1, -1):
            cand = bound + (1 << bit)
            c = count(lambda kk, cc: jnp.logical_and(kk == thr, key_index(cc) < cand))
            bound = jnp.where(c < want, cand, bound)
        tie_scr[...] = jnp.broadcast_to(bound, (SUBLANES, tq))

    tie_bound = tie_scr[0:1, :]

    def bias_body(c, carry):
        off = chunk_start(c)
        kk = key_scr[pl.ds(off, lc), :]
        tie = jnp.where(key_index(c) <= tie_bound, 0.0, MASKED)
        b = jnp.where(kk > thr, 0.0, jnp.where(kk == thr, tie, MASKED))
        b = jnp.where(kk > INT_MIN, b, MASKED)
        bias_scr[pl.ds(off, lc), :] = b.astype(BF16)
        return carry

    lax.fori_loop(0, n_chunks, bias_body, 0)

    gw = A_GROUP * tq
    eye = jnp.where(lax.broadcasted_iota(I32, (tq, gw), 0) == (lax.broadcasted_iota(I32, (tq, gw), 1) & (tq - 1)),
                    1.0, 0.0).astype(BF16)
    q_aug = [jnp.concatenate(
        [eye, jnp.concatenate(
            [qT_ref[0, 0, (g * A_GROUP + h) * A_HEAD_DIM:(g * A_GROUP + h + 1) * A_HEAD_DIM, :]
             for h in range(A_GROUP)], axis=1)], axis=0) for g in range(A_KV_HEADS)]
    acc_scr[...] = jnp.zeros(acc_scr.shape, F32)

    def logits(c, g):
        off = chunk_start(c)
        k_aug = jnp.concatenate([bias_scr[pl.ds(off, lc), :], kg_ref[0, g, pl.ds(off, lc), :]], axis=1)
        return _dot(k_aug, q_aug[g])

    def softmax_pv(s, c, g, m):
        mn = jnp.maximum(m, jnp.max(s, axis=0, keepdims=True))
        alpha = jnp.exp2(m - mn)
        p = jnp.exp2(s - mn).astype(BF16)
        vt = vT_ref[0, g, :, pl.ds(chunk_start(c), lc)]
        acc_scr[g] = alpha * acc_scr[g] + _dot(vt, p)
        return mn

    for g in range(A_KV_HEADS):
        sa_scr[g] = logits(0, g)

    def att_body(i, ms):
        c0 = 2 * i
        ms = list(ms)
        for g in range(A_KV_HEADS):
            sb_scr[g] = logits(c0 + 1, g)
            ms[g] = softmax_pv(sa_scr[g], c0, g, ms[g])
        c2 = jnp.minimum(c0 + 2, n_chunks - 1)
        for g in range(A_KV_HEADS):
            sa_scr[g] = logits(c2, g)
            ms[g] = softmax_pv(sb_scr[g], c0 + 1, g, ms[g])
        return tuple(ms)

    m0 = tuple(jnp.full((1, gw), M_INIT, F32) for _ in range(A_KV_HEADS))
    lax.fori_loop(0, n_chunks // 2, att_body, m0)
    for g in range(A_KV_HEADS):
        acc = acc_scr[g]
        o = acc[0:A_HEAD_DIM, :] / acc[A_HEAD_DIM:A_HEAD_DIM + 1, :]
        for h in range(A_GROUP):
            r0 = (g * A_GROUP + h) * A_HEAD_DIM
            oT_scr[r0:r0 + A_HEAD_DIM, :] = o[:, h * tq:(h + 1) * tq]

    o = jnp.transpose(oT_scr[...]).astype(BF16)
    res = x_ref[0] + _dot(o, wo_ref[...])
    if rows_out == tq:
        out_ref[0] = res
    else:
        res_scr[...] = res
        r0 = pl.multiple_of(pl.program_id(0) * rows_out, rows_out)
        out_ref[0] = res_scr[pl.ds(r0, rows_out), :]


def _dsa_attend(x, qT, qiT, wT, lim, kg, vT, kib, wo, *, topk, causal, shared_queries, rows_out):
    Bk, _, L, _ = kg.shape
    D = x.shape[-1]
    tq = qT.shape[-1]
    nq = qT.shape[1]
    assert L % (2 * DSA_KEY_CHUNK) == 0
    if shared_queries:
        grid = (Bk, 1)
        qmap = lambda b, j: (0, 0, 0, 0)
        xmap = lambda b, j: (0, 0, 0)
        omap = lambda b, j: (0, b, 0)
        lmap = lambda b, j: (0, 0, 0)
    else:
        grid = (Bk, nq)
        qmap = lambda b, j: (b, j, 0, 0)
        xmap = lambda b, j: (b, j, 0)
        omap = xmap
        lmap = lambda b, j: (j, 0, 0)
    in_specs = [
        pl.BlockSpec((1, 1, tq), lmap),
        pl.BlockSpec((1, tq, D), xmap),
        pl.BlockSpec((1, 1, _NQ, tq), qmap),
        pl.BlockSpec((1, 1, _NQI, tq), qmap),
        pl.BlockSpec((1, 1, IDX_HEADS, tq), qmap),
        pl.BlockSpec((1, A_KV_HEADS, L, A_HEAD_DIM), lambda b, j: (b, 0, 0, 0)),
        pl.BlockSpec((1, A_KV_HEADS, V_AUG_ROWS, L), lambda b, j: (b, 0, 0, 0)),
        pl.BlockSpec((1, L, IDX_DIM), lambda b, j: (b, 0, 0)),
        pl.BlockSpec((_NQ, D), lambda b, j: (0, 0)),
    ]
    kern = functools.partial(_dsa_attn_kernel, topk=topk, causal=causal,
                             n_chunks_total=L // DSA_KEY_CHUNK, rows_out=rows_out)
    return pl.pallas_call(
        kern, grid=grid, in_specs=in_specs, name="dsa_attend",
        out_specs=pl.BlockSpec((1, rows_out, D), omap),
        out_shape=jax.ShapeDtypeStruct(x.shape, F32),
        scratch_shapes=[pltpu.VMEM((L, tq), I32), pltpu.VMEM((SUBLANES, tq), I32),
                        pltpu.VMEM((_NQ, tq), F32), pltpu.VMEM((tq, D), F32),
                        pltpu.VMEM((A_KV_HEADS, V_AUG_ROWS, A_GROUP * tq), F32),
                        pltpu.VMEM((L, tq), BF16),
                        pltpu.VMEM((A_KV_HEADS, DSA_KEY_CHUNK, A_GROUP * tq), F32),
                        pltpu.VMEM((A_KV_HEADS, DSA_KEY_CHUNK, A_GROUP * tq), F32),
                        pltpu.VMEM((L, tq), I16), pltpu.VMEM((L, tq), I16),
                        pltpu.VMEM((DSA_KEY_CHUNK, IDX_HEADS * tq), F32),
                        pltpu.VMEM((DSA_KEY_CHUNK, IDX_HEADS * tq), F32)],
        compiler_params=_params("arbitrary", "arbitrary"),
    )(lim, x, qT, qiT, wT, kg, vT, kib, wo)


MLP_FF_CHUNK = 1024


def _mlp_kernel(x_ref, g_ref, wu_ref, wd_ref, gf_ref, out_ref, *, final_norm):
    x = x_ref[...]
    xn = _rms(x, g_ref[...]).astype(BF16)
    acc = x
    for f in range(0, wu_ref.shape[1], MLP_FF_CHUNK):
        h = jnp.maximum(_dot(xn, wu_ref[:, f:f + MLP_FF_CHUNK]), 0.0)
        acc = acc + _dot((h * h).astype(BF16), wd_ref[f:f + MLP_FF_CHUNK, :])
    if final_norm:
        acc = _rms(acc, gf_ref[...])
    out_ref[...] = acc


def _mlp(x, gain, w_up, w_down, gain_final, *, tm, final_norm):
    M, D = x.shape
    FF = w_up.shape[1]
    const = lambda i: (0, 0)
    return pl.pallas_call(
        functools.partial(_mlp_kernel, final_norm=final_norm), name="mlp",
        grid=(M // tm,),
        in_specs=[
            pl.BlockSpec((tm, D), lambda i: (i, 0)),
            pl.BlockSpec((1, D), const),
            pl.BlockSpec((D, FF), const, pipeline_mode=pl.Buffered(1)),
            pl.BlockSpec((FF, D), const, pipeline_mode=pl.Buffered(1)),
            pl.BlockSpec((1, D), const),
        ],
        out_specs=pl.BlockSpec((tm, D), lambda i: (i, 0)),
        out_shape=jax.ShapeDtypeStruct((M, D), F32),
        compiler_params=_params("arbitrary"),
    )(x, gain.reshape(1, D), w_up.astype(BF16), w_down.astype(BF16), gain_final.reshape(1, D))


_GQK = GLA_HEADS * GLA_DK
_GV = GLA_HEADS * GLA_DV


def _gla_proj_kernel(x_ref, g_ref, w_ref, wa_ref, ba_ref, q_ref, k_ref, v_ref, gate_ref, la_ref):
    xn = _rms(x_ref[...], g_ref[...]).astype(BF16)
    y = _dot(xn, w_ref[...])
    D = gate_ref.shape[-1]
    q_ref[...] = y[:, 0:_GQK]
    k_ref[...] = y[:, _GQK:2 * _GQK]
    v_ref[...] = y[:, 2 * _GQK:2 * _GQK + _GV].astype(BF16)
    o = 2 * _GQK + _GV
    gate_ref[...] = y[:, o:o + D]
    a = y[:, o + D:o + D + GLA_RANK].astype(BF16)
    z = _dot(a, wa_ref[...]) + ba_ref[...]
    la_ref[...] = (jnp.minimum(z, 0.0) - jnp.log1p(jnp.exp(-jnp.abs(z)))) * (1.0 / GLA_TAU)


def _gla_project(x, gain, w_in, w_a2, b_a, *, tm):
    M, D = x.shape
    N = w_in.shape[1]
    const = lambda i: (0, 0)
    row = lambda i: (i, 0)
    return pl.pallas_call(
        _gla_proj_kernel, grid=(M // tm,), name="gla_project",
        in_specs=[
            pl.BlockSpec((tm, D), row),
            pl.BlockSpec((1, D), const),
            pl.BlockSpec((D, N), const),
            pl.BlockSpec((GLA_RANK, _GQK), const),
            pl.BlockSpec((1, _GQK), const),
        ],
        out_specs=(
            pl.BlockSpec((tm, _GQK), row), pl.BlockSpec((tm, _GQK), row), pl.BlockSpec((tm, _GV), row),
            pl.BlockSpec((tm, D), row), pl.BlockSpec((tm, _GQK), row)),
        out_shape=(
            jax.ShapeDtypeStruct((M, _GQK), F32), jax.ShapeDtypeStruct((M, _GQK), F32),
            jax.ShapeDtypeStruct((M, _GV), BF16), jax.ShapeDtypeStruct((M, D), F32),
            jax.ShapeDtypeStruct((M, _GQK), F32)),
        compiler_params=_params("arbitrary"),
    )(x, gain.reshape(1, D), w_in.astype(BF16), w_a2.astype(BF16), b_a.reshape(1, _GQK))


def _split3(a):
    a0 = a.astype(BF16)
    r = a - a0.astype(F32)
    a1 = r.astype(BF16)
    a2 = (r - a1.astype(F32)).astype(BF16)
    return a0, a1, a2


def _gla_kernel(x_ref, q_ref, k_ref, v_ref, gate_ref, la_ref, s0_ref, ng_ref, wo_ref,
                out_ref, sout_ref, s_scr, a_scr, o_scr):
    c = pl.program_id(1)
    C = q_ref.shape[1]

    @pl.when(c == 0)
    def _():
        s_scr[...] = s0_ref[0]

    la = la_ref[0]
    row = lax.broadcasted_iota(I32, (C, C), 0)
    col = lax.broadcasted_iota(I32, (C, C), 1)
    tri = jnp.where(row >= col, 1.0, 0.0).astype(BF16)
    l0, l1, l2 = _split3(la)
    b = _dot(tri, l0) + _dot(tri, l1) + _dot(tri, l2)
    q = q_ref[0] * (GLA_DK ** -0.5)
    k = k_ref[0]
    trow = lax.broadcasted_iota(I32, (C, 1), 0)

    def block_ref(values, size, pick):
        v3 = values.reshape(C // size, size, values.shape[-1])
        return jnp.broadcast_to(v3[:, pick:pick + 1, :], v3.shape).reshape(values.shape)

    def accumulate(qh, kh, mask, first):
        for h in range(GLA_HEADS):
            sl = slice(h * GLA_DK, (h + 1) * GLA_DK)
            blk = jnp.where(mask, _dot_nt(qh[:, sl], kh[:, sl]), 0.0)
            if first:
                a_scr[h] = blk
            else:
                a_scr[h] = a_scr[h] + blk

    d = min(GLA_DIAG, C)
    before = block_ref(b - la, d, 0)
    qh = (q * jnp.exp(b - before)).astype(BF16)
    kh = (k * jnp.exp(before - b)).astype(BF16)
    sh = d.bit_length() - 1
    accumulate(qh, kh, jnp.logical_and((row >> sh) == (col >> sh), row >= col), True)
    half = d
    while half < C:
        upper = (trow & (2 * half - 1)) >= half
        split = block_ref(b, 2 * half, half - 1)
        qh = (q * jnp.exp(jnp.where(upper, b - split, -jnp.inf))).astype(BF16)
        kh = (k * jnp.exp(jnp.where(upper, -jnp.inf, split - b))).astype(BF16)
        sh = (2 * half).bit_length() - 1
        accumulate(qh, kh, (row >> sh) == (col >> sh), False)
        half *= 2

    b_end = b[C - 1:C, :]
    q_in = (q * jnp.exp(b)).astype(BF16)
    k_out = k * jnp.exp(b_end - b)
    ng = ng_ref[...]
    for h in range(GLA_HEADS):
        sl = slice(h * GLA_DK, (h + 1) * GLA_DK)
        vh = v_ref[0, :, h * GLA_DV:(h + 1) * GLA_DV]
        s_old = s_scr[h]
        o = _dot(q_in[:, sl], s_old.astype(BF16)) + _dot(a_scr[h].astype(BF16), vh)
        dec = jnp.transpose(jnp.broadcast_to(jnp.exp(b_end[:, sl]), (GLA_DK, GLA_DK)))
        dec = jnp.concatenate([dec] * (GLA_DV // GLA_DK), axis=1)
        s_scr[h] = dec * s_old + _dot(jnp.transpose(k_out[:, sl]).astype(BF16), vh)
        o = _rms(o, ng)
        gt = gate_ref[0, :, h * GLA_DV:(h + 1) * GLA_DV]
        o_scr[:, h * GLA_DV:(h + 1) * GLA_DV] = (o * (gt / (1.0 + jnp.exp(-gt)))).astype(BF16)

    out_ref[0] = x_ref[0] + _dot(o_scr[...], wo_ref[...])

    @pl.when(c == pl.num_programs(1) - 1)
    def _():
        sout_ref[0] = s_scr[...]


def _gla(x, q, k, v, gate, la, s0, norm_g, w_o, *, chunk):
    B, T, D = x.shape
    n = T // chunk
    tok = lambda b, c: (b, c, 0)
    const = lambda b, c: (0, 0)
    st = lambda b, c: (b, 0, 0, 0)
    return pl.pallas_call(
        _gla_kernel, grid=(B, n), name="gla",
        in_specs=[
            pl.BlockSpec((1, chunk, D), tok),
            pl.BlockSpec((1, chunk, _GQK), tok),
            pl.BlockSpec((1, chunk, _GQK), tok),
            pl.BlockSpec((1, chunk, _GV), tok),
            pl.BlockSpec((1, chunk, D), tok),
            pl.BlockSpec((1, chunk, _GQK), tok),
            pl.BlockSpec((1, GLA_HEADS, GLA_DK, GLA_DV), st),
            pl.BlockSpec((1, GLA_DV), const),
            pl.BlockSpec((_GV, D), const),
        ],
        out_specs=(pl.BlockSpec((1, chunk, D), tok), pl.BlockSpec((1, GLA_HEADS, GLA_DK, GLA_DV), st)),
        out_shape=(jax.ShapeDtypeStruct((B, T, D), F32),
                   jax.ShapeDtypeStruct((B, GLA_HEADS, GLA_DK, GLA_DV), F32)),
        scratch_shapes=[pltpu.VMEM((GLA_HEADS, GLA_DK, GLA_DV), F32),
                        pltpu.VMEM((GLA_HEADS, chunk, chunk), F32),
                        pltpu.VMEM((chunk, _GV), BF16)],
        compiler_params=_params("arbitrary", "arbitrary"),
    )(x, q, k, v, gate, la, s0, norm_g.reshape(1, GLA_DV), w_o.astype(BF16))


def _pick_tile(n, candidates):
    for t in candidates:
        if n % t == 0:
            return t
    return n


def _round_up(n, m):
    return (n + m - 1) // m * m


def kernel(x_prompt, x_sample, cache_k, cache_v, cache_kidx, state_gla, norm_mix, norm_ffn, norm_final,
           a_w_in, a_w_o, b_w_in, b_w_a2, b_b_a, b_norm, b_w_o, ffn_w_up, ffn_w_down):
    B, S, D = x_prompt.shape
    Bs, Ts, _ = x_sample.shape
    past = cache_k.shape[2]
    Ms = Bs * Ts
    tq = DSA_Q_TILE
    assert S % tq == 0 and Ms == tq and Ts % SUBLANES == 0

    wo_a = a_w_o[0].astype(BF16)
    tm = _pick_tile(S, (512, 256, 128))
    qT, qiT, wT, vT, k_p, v_p, ki_p, kg, kib = _dsa_project(x_prompt, norm_mix[0], a_w_in[0], tm=tm, tq=tq)
    pos = jnp.arange(S, dtype=I32)
    lim_p = ((pos // STREAM_CHUNK + 1) * STREAM_CHUNK).reshape(S // tq, 1, tq)
    L_p = _round_up(S, 2 * DSA_KEY_CHUNK)
    if L_p != S:
        pad = L_p - S
        kg = jnp.pad(kg, ((0, 0), (0, 0), (0, pad), (0, 0)))
        vT = jnp.pad(vT, ((0, 0), (0, 0), (0, 0), (0, pad)))
        kib = jnp.pad(kib, ((0, 0), (0, pad), (0, 0)))
    hp = _dsa_attend(x_prompt, qT, qiT, wT, lim_p, kg, vT, kib, wo_a,
                     topk=min(TOPK_MAX, S // 4), causal=True, shared_queries=False, rows_out=tq)
    xs = x_sample.reshape(1, Ms, D)
    qT_s, qiT_s, wT_s, _, k_s, v_s, ki_s, _, _ = _dsa_project(xs, norm_mix[0], a_w_in[0], tm=Ms, tq=tq)
    k_s = k_s.reshape(Bs, Ts, _NKV)
    v_s = v_s.reshape(Bs, Ts, _NKV)
    ki_s = ki_s.reshape(Bs, Ts, IDX_DIM)
    L_real = past + Ts
    L_s = _round_up(L_real, 2 * DSA_KEY_CHUNK)
    padn = L_s - L_real
    k_all = jnp.concatenate([cache_k[0].reshape(Bs, past, _NKV), k_s, jnp.zeros((Bs, padn, _NKV), F32)], axis=1)
    v_all = jnp.concatenate([cache_v[0].reshape(Bs, past, _NKV), v_s, jnp.zeros((Bs, padn, _NKV), F32)], axis=1)
    ki_all = jnp.concatenate([cache_kidx[0], ki_s, jnp.zeros((Bs, padn, IDX_DIM), F32)], axis=1)
    kg_s = k_all.reshape(Bs, L_s, A_KV_HEADS, A_HEAD_DIM).transpose(0, 2, 1, 3).astype(BF16)
    vT_s = jnp.concatenate(
        [v_all.reshape(Bs, L_s, A_KV_HEADS, A_HEAD_DIM).transpose(0, 2, 3, 1),
         jnp.ones((Bs, A_KV_HEADS, V_AUG_ROWS - A_HEAD_DIM, L_s), F32)], axis=2).astype(BF16)
    kib_s = ki_all.astype(BF16)
    pos_s = past + jnp.arange(Ts, dtype=I32)
    lim_s = jnp.minimum((pos_s // STREAM_CHUNK + 1) * STREAM_CHUNK, L_real)
    lim_s = jnp.tile(lim_s, Bs).reshape(1, 1, Ms)
    hs = _dsa_attend(xs, qT_s, qiT_s, wT_s, lim_s, kg_s, vT_s, kib_s, wo_a,
                     topk=min(TOPK_MAX, L_real // 4), causal=False, shared_queries=True, rows_out=Ts)

    tmm = _pick_tile(B * S, (512, 256, 128))
    hp = _mlp(hp.reshape(B * S, D), norm_ffn[0], ffn_w_up[0], ffn_w_down[0], norm_final, tm=tmm, final_norm=False)
    hs = _mlp(hs.reshape(Ms, D), norm_ffn[0], ffn_w_up[0], ffn_w_down[0], norm_final, tm=Ms, final_norm=False)

    q, k, v, gate, la = _gla_project(hp, norm_mix[1], b_w_in[0], b_w_a2[0], b_b_a[0], tm=tmm)
    chunk = _pick_tile(S, (256, 128, 64))
    r3 = lambda a: a.reshape(B, S, a.shape[-1])
    s0 = jnp.zeros((B, GLA_HEADS, GLA_DK, GLA_DV), F32)
    hp, s_p = _gla(hp.reshape(B, S, D), r3(q), r3(k), r3(v), r3(gate), r3(la), s0, b_norm[0], b_w_o[0], chunk=chunk)
    q, k, v, gate, la = _gla_project(hs, norm_mix[1], b_w_in[0], b_w_a2[0], b_b_a[0], tm=Ms)
    r3 = lambda a: a.reshape(Bs, Ts, a.shape[-1])
    hs, s_s = _gla(hs.reshape(Bs, Ts, D), r3(q), r3(k), r3(v), r3(gate), r3(la), state_gla[0], b_norm[0], b_w_o[0],
                   chunk=Ts)

    y_p = _mlp(hp.reshape(B * S, D), norm_ffn[1], ffn_w_up[1], ffn_w_down[1], norm_final, tm=tmm, final_norm=True)
    y_s = _mlp(hs.reshape(Ms, D), norm_ffn[1], ffn_w_up[1], ffn_w_down[1], norm_final, tm=Ms, final_norm=True)

    return (y_p.reshape(B, S, D), y_s.reshape(Bs, Ts, D),
            k_p.reshape(1, B, S, A_KV_HEADS, A_HEAD_DIM), v_p.reshape(1, B, S, A_KV_HEADS, A_HEAD_DIM),
            ki_p.reshape(1, B, S, IDX_DIM), s_p[None],
            k_s.reshape(1, Bs, Ts, A_KV_HEADS, A_HEAD_DIM), v_s.reshape(1, Bs, Ts, A_KV_HEADS, A_HEAD_DIM),
            ki_s.reshape(1, Bs, Ts, IDX_DIM), s_s[None])
```

```python
import functools

import jax
import jax.numpy as jnp
from jax import lax
from jax.experimental import pallas as pl
from jax.experimental.pallas import tpu as pltpu

F32 = jnp.float32
BF16 = jnp.bfloat16
I32 = jnp.int32
I16 = jnp.int16

EPS = 1e-6
STREAM_CHUNK = 64
TOPK_MAX = 256
A_HEADS = 16
A_KV_HEADS = 4
A_GROUP = A_HEADS // A_KV_HEADS
A_HEAD_DIM = 64
IDX_HEADS = 8
IDX_DIM = 64
GLA_HEADS = 4
GLA_DK = 128
GLA_DV = 256
GLA_RANK = 16
GLA_TAU = 16.0

LANES = 128
SUBLANES = 8
INT_MIN = -(2 ** 31)
NEG_BIG = -1e30

DSA_Q_TILE = LANES
DSA_KEY_CHUNK = 256
GLA_DIAG = 8
VMEM_LIMIT = 56 * 1024 * 1024


def _params(*sem):
    return pltpu.CompilerParams(dimension_semantics=sem, vmem_limit_bytes=VMEM_LIMIT)


def _rms(x, g):
    ms = jnp.mean(x * x, axis=-1, keepdims=True)
    return x * lax.rsqrt(ms + EPS) * g


def _dot(a, b):
    return jnp.dot(a, b, preferred_element_type=F32)


def _dot_nt(a, b):
    return lax.dot_general(a, b, (((1,), (1,)), ((), ())), preferred_element_type=F32)


_NQ = A_HEADS * A_HEAD_DIM
_NKV = A_KV_HEADS * A_HEAD_DIM
_NQI = IDX_HEADS * IDX_DIM
_T_ROWS = _NQ + _NQI + _NKV + 16
BF16_SUBLANES = 16
V_AUG_ROWS = A_HEAD_DIM + BF16_SUBLANES
LOG2E = 1.4426950408889634
MASKED = -1e30
M_INIT = -1e29


def _dsa_proj_kernel(x_ref, g_ref, wt_ref, wn_ref,
                     qT_ref, qiT_ref, wT_ref, vT_ref, k_ref, v_ref, ki_ref, kg_ref, kib_ref, *, tq):
    xn = _rms(x_ref[0], g_ref[...]).astype(BF16)
    tm = xn.shape[0]
    yT = _dot_nt(wt_ref[...], xn)
    y = _dot(xn, wn_ref[...])
    for j in range(tm // tq):
        sl = slice(j * tq, (j + 1) * tq)
        qT_ref[0, j] = yT[0:_NQ, sl].astype(BF16)
        qiT_ref[0, j] = yT[_NQ:_NQ + _NQI, sl].astype(BF16)
        wT_ref[0, j] = yT[_NQ + _NQI + _NKV:_NQ + _NQI + _NKV + IDX_HEADS, sl]
    for h in range(A_KV_HEADS):
        r0 = _NQ + _NQI + h * A_HEAD_DIM
        vT_ref[0, h, 0:A_HEAD_DIM, :] = yT[r0:r0 + A_HEAD_DIM, :].astype(BF16)
        vT_ref[0, h, A_HEAD_DIM:V_AUG_ROWS, :] = jnp.ones((V_AUG_ROWS - A_HEAD_DIM, tm), BF16)
    k = y[:, 0:_NKV]
    ki = y[:, 2 * _NKV:2 * _NKV + IDX_DIM]
    k_ref[0] = k
    v_ref[0] = y[:, _NKV:2 * _NKV]
    ki_ref[0] = ki
    for h in range(A_KV_HEADS):
        kg_ref[0, h] = k[:, h * A_HEAD_DIM:(h + 1) * A_HEAD_DIM].astype(BF16)
    kib_ref[0] = ki.astype(BF16)


def _dsa_project(x, gain, w_in, *, tm, tq):
    B, S, D = x.shape
    o = 0
    parts = []
    for n in (_NQ, _NKV, _NKV, _NQI, IDX_DIM, IDX_HEADS):
        parts.append(w_in[:, o:o + n])
        o += n
    w_q, w_k, w_v, w_qi, w_ki, w_wt = parts
    wt_all = jnp.concatenate(
        [w_q * (A_HEAD_DIM ** -0.5 * LOG2E), w_qi, w_v, w_wt, jnp.zeros((D, 16 - IDX_HEADS), F32)],
        axis=1).T.astype(BF16)
    wn_all = jnp.concatenate([w_k, w_v, w_ki], axis=1).astype(BF16)
    nq = S // tq
    nt = S // tm
    jq = tm // tq
    out_shape = (
        jax.ShapeDtypeStruct((B, nq, _NQ, tq), BF16),
        jax.ShapeDtypeStruct((B, nq, _NQI, tq), BF16),
        jax.ShapeDtypeStruct((B, nq, IDX_HEADS, tq), F32),
        jax.ShapeDtypeStruct((B, A_KV_HEADS, V_AUG_ROWS, S), BF16),
        jax.ShapeDtypeStruct((B, S, _NKV), F32),
        jax.ShapeDtypeStruct((B, S, _NKV), F32),
        jax.ShapeDtypeStruct((B, S, IDX_DIM), F32),
        jax.ShapeDtypeStruct((B, A_KV_HEADS, S, A_HEAD_DIM), BF16),
        jax.ShapeDtypeStruct((B, S, IDX_DIM), BF16),
    )
    out_specs = (
        pl.BlockSpec((1, jq, _NQ, tq), lambda b, i: (b, i, 0, 0)),
        pl.BlockSpec((1, jq, _NQI, tq), lambda b, i: (b, i, 0, 0)),
        pl.BlockSpec((1, jq, IDX_HEADS, tq), lambda b, i: (b, i, 0, 0)),
        pl.BlockSpec((1, A_KV_HEADS, V_AUG_ROWS, tm), lambda b, i: (b, 0, 0, i)),
        pl.BlockSpec((1, tm, _NKV), lambda b, i: (b, i, 0)),
        pl.BlockSpec((1, tm, _NKV), lambda b, i: (b, i, 0)),
        pl.BlockSpec((1, tm, IDX_DIM), lambda b, i: (b, i, 0)),
        pl.BlockSpec((1, A_KV_HEADS, tm, A_HEAD_DIM), lambda b, i: (b, 0, i, 0)),
        pl.BlockSpec((1, tm, IDX_DIM), lambda b, i: (b, i, 0)),
    )
    in_specs = [
        pl.BlockSpec((1, tm, D), lambda b, i: (b, i, 0)),
        pl.BlockSpec((1, D), lambda b, i: (0, 0)),
        pl.BlockSpec((_T_ROWS, D), lambda b, i: (0, 0)),
        pl.BlockSpec((D, 2 * _NKV + IDX_DIM), lambda b, i: (0, 0)),
    ]
    return pl.pallas_call(
        functools.partial(_dsa_proj_kernel, tq=tq), name="dsa_project",
        grid=(B, nt), in_specs=in_specs, out_specs=out_specs, out_shape=out_shape,
        compiler_params=_params("arbitrary", "arbitrary"),
    )(x, gain.reshape(1, D), wt_all, wn_all)


def _dsa_attn_kernel(lim_ref, x_ref, qT_ref, qiT_ref, wT_ref, kg_ref, vT_ref, kib_ref, wo_ref,
                     out_ref, key_scr, tie_scr, oT_scr, res_scr, acc_scr, bias_scr, sa_scr, sb_scr, hi_scr, lo_scr,
                     ia_scr, ib_scr,
                     *, topk, causal, n_chunks_total, rows_out):
    tq = qT_ref.shape[-1]
    lc = DSA_KEY_CHUNK
    if causal:
        n_chunks = jnp.minimum(2 * (((pl.program_id(1) + 1) * tq + 2 * lc - 1) // (2 * lc)), n_chunks_total)
    else:
        n_chunks = n_chunks_total
    lim = lim_ref[0]

    def chunk_start(c):
        return pl.multiple_of(c * lc, lc)

    def key_index(c):
        return c * lc + lax.broadcasted_iota(I32, (lc, tq), 0)

    qi_all = jnp.concatenate(
        [qiT_ref[0, 0, h * IDX_DIM:(h + 1) * IDX_DIM, :] for h in range(IDX_HEADS)], axis=1)
    w_rows = wT_ref[0, 0] * ((IDX_DIM ** -0.5) * (IDX_HEADS ** -0.5))

    def idx_logits(c):
        return _dot(kib_ref[0, pl.ds(chunk_start(c), lc), :], qi_all)

    def store_keys(s_ref, c):
        off = chunk_start(c)
        score = jnp.maximum(s_ref[:, 0:tq], 0.0) * w_rows[0:1, :]
        for h in range(1, IDX_HEADS):
            score = score + jnp.maximum(s_ref[:, h * tq:(h + 1) * tq], 0.0) * w_rows[h:h + 1, :]
        bits = pltpu.bitcast(score, I32)
        key = jnp.where(bits < 0, -(bits & 0x7FFFFFFF), bits)
        key = jnp.where(key_index(c) < lim, key, INT_MIN)
        key_scr[pl.ds(off, lc), :] = key
        hi_scr[pl.ds(off, lc), :] = (key >> 16).astype(I16)

    def score_body(c, carry):
        ia_scr[...] = idx_logits(c)
        store_keys(ia_scr, c)
        return carry

    lax.fori_loop(0, n_chunks, score_body, 0)

    def count16(ref, pred):
        rows = BF16_SUBLANES
        ONE_BF16 = jnp.ones((rows, tq), jnp.bfloat16)
        ZERO_BF16 = jnp.zeros((rows, tq), jnp.bfloat16)

        def body(i, acc):
            blk = ref[pl.ds(pl.multiple_of(i * (2 * lc), 2 * lc), 2 * lc), :]
            flags = [jnp.where(pred(blk[r * rows:(r + 1) * rows, :]), ONE_BF16, ZERO_BF16)
                     for r in range(2 * lc // rows)]
            while len(flags) > 1:
                flags = [a + b for a, b in zip(flags[0::2], flags[1::2])]
            return acc + flags[0].astype(F32)

        acc = lax.fori_loop(0, n_chunks // 2, body, jnp.zeros((rows, tq), F32))
        return jnp.sum(acc, axis=0, keepdims=True).astype(I32)

    def wide16(v):
        return jnp.broadcast_to(v, (BF16_SUBLANES, tq)).astype(I16)

    def search16(ref, need, c_all):
        def bit_body(i, carry):
            t, cnt = carry
            cand = t + jnp.left_shift(jnp.int32(1), 15 - i)
            cw = wide16(cand)
            c = count16(ref, lambda blk: blk >= cw)
            ok = c >= need
            return jnp.where(ok, cand, t), jnp.where(ok, c, cnt)
        return lax.fori_loop(0, 16, bit_body, (jnp.full((1, tq), -32768, I32), c_all))

    def count(pred):
        def body(c, acc):
            kk = key_scr[pl.ds(chunk_start(c), lc), :]
            m = jnp.where(pred(kk, c), 1, 0).astype(I32)
            return acc + jnp.sum(m.reshape(lc // SUBLANES, SUBLANES, tq), axis=0)
        acc = lax.fori_loop(0, n_chunks, body, jnp.zeros((SUBLANES, tq), I32))
        return jnp.sum(acc, axis=0, keepdims=True)

    n_rows = jnp.zeros((1, tq), I32) + n_chunks * lc
    t_hi, c_hi = search16(hi_scr, topk, n_rows)
    t_hi_w = wide16(t_hi)
    c_above = count16(hi_scr, lambda blk: blk > t_hi_w)

    def low_body(c, carry):
        off = chunk_start(c)
        kk = key_scr[pl.ds(off, lc), :]
        low = (kk & 0xFFFF) - 32768
        lo_scr[pl.ds(off, lc), :] = jnp.where((kk >> 16) == t_hi, low, -32768).astype(I16)
        return carry

    lax.fori_loop(0, n_chunks, low_body, 0)
    t_lo, c_lo = search16(lo_scr, topk - c_above, c_hi - c_above)
    thr = jnp.left_shift(t_hi, 16) | (t_lo + 32768)
    cge = c_above + c_lo

    n_idx_bits = max(1, (n_chunks_total * lc - 1).bit_length())
    tie_scr[...] = jnp.full((SUBLANES, tq), 2 ** 30, I32)
    surplus = jnp.logical_and(cge > topk, thr > INT_MIN)

    @pl.when(jnp.max(jnp.where(surplus, 1, 0)) > 0)
    def _():
        cgt = count(lambda kk, _: kk > thr)
        want = topk - cgt
        bound = jnp.zeros((1, tq), I32)
        for bit in range(n_idx_bits - 1, -1, -1):
            cand = bound + (1 << bit)
            c = count(lambda kk, cc: jnp.logical_and(kk == thr, key_index(cc) < cand))
            bound = jnp.where(c < want, cand, bound)
        tie_scr[...] = jnp.broadcast_to(bound, (SUBLANES, tq))

    tie_bound = tie_scr[0:1, :]

    def bias_body(c, carry):
        off = chunk_start(c)
        kk = key_scr[pl.ds(off, lc), :]
        tie = jnp.where(key_index(c) <= tie_bound, 0.0, MASKED)
        b = jnp.where(kk > thr, 0.0, jnp.where(kk == thr, tie, MASKED))
        b = jnp.where(kk > INT_MIN, b, MASKED)
        bias_scr[pl.ds(off, lc), :] = b.astype(BF16)
        return carry

    lax.fori_loop(0, n_chunks, bias_body, 0)

    gw = A_GROUP * tq
    eye = jnp.where(lax.broadcasted_iota(I32, (tq, gw), 0) == (lax.broadcasted_iota(I32, (tq, gw), 1) & (tq - 1)),
                    1.0, 0.0).astype(BF16)
    q_aug = [jnp.concatenate(
        [eye, jnp.concatenate(
            [qT_ref[0, 0, (g * A_GROUP + h) * A_HEAD_DIM:(g * A_GROUP + h + 1) * A_HEAD_DIM, :]
             for h in range(A_GROUP)], axis=1)], axis=0) for g in range(A_KV_HEADS)]
    acc_scr[...] = jnp.zeros(acc_scr.shape, F32)

    def logits(c, g):
        off = chunk_start(c)
        k_aug = jnp.concatenate([bias_scr[pl.ds(off, lc), :], kg_ref[0, g, pl.ds(off, lc), :]], axis=1)
        return _dot(k_aug, q_aug[g])

    def softmax_pv(s, c, g, m):
        mn = jnp.maximum(m, jnp.max(s, axis=0, keepdims=True))
        alpha = jnp.exp2(m - mn)
        p = jnp.exp2(s - mn).astype(BF16)
        vt = vT_ref[0, g, :, pl.ds(chunk_start(c), lc)]
        acc_scr[g] = alpha * acc_scr[g] + _dot(vt, p)
        return mn

    for g in range(A_KV_HEADS):
        sa_scr[g] = logits(0, g)

    def att_body(i, ms):
        c0 = 2 * i
        ms = list(ms)
        for g in range(A_KV_HEADS):
            sb_scr[g] = logits(c0 + 1, g)
            ms[g] = softmax_pv(sa_scr[g], c0, g, ms[g])
        c2 = jnp.minimum(c0 + 2, n_chunks - 1)
        for g in range(A_KV_HEADS):
            sa_scr[g] = logits(c2, g)
            ms[g] = softmax_pv(sb_scr[g], c0 + 1, g, ms[g])
        return tuple(ms)

    m0 = tuple(jnp.full((1, gw), M_INIT, F32) for _ in range(A_KV_HEADS))
    lax.fori_loop(0, n_chunks // 2, att_body, m0)
    for g in range(A_KV_HEADS):
        acc = acc_scr[g]
        o = acc[0:A_HEAD_DIM, :] / acc[A_HEAD_DIM:A_HEAD_DIM + 1, :]
        for h in range(A_GROUP):
            r0 = (g * A_GROUP + h) * A_HEAD_DIM
            oT_scr[r0:r0 + A_HEAD_DIM, :] = o[:, h * tq:(h + 1) * tq]

    o = jnp.transpose(oT_scr[...]).astype(BF16)
    res = x_ref[0] + _dot(o, wo_ref[...])
    if rows_out == tq:
        out_ref[0] = res
    else:
        res_scr[...] = res
        r0 = pl.multiple_of(pl.program_id(0) * rows_out, rows_out)
        out_ref[0] = res_scr[pl.ds(r0, rows_out), :]


def _dsa_attend(x, qT, qiT, wT, lim, kg, vT, kib, wo, *, topk, causal, shared_queries, rows_out):
    Bk, _, L, _ = kg.shape
    D = x.shape[-1]
    tq = qT.shape[-1]
    nq = qT.shape[1]
    assert L % (2 * DSA_KEY_CHUNK) == 0
    if shared_queries:
        grid = (Bk, 1)
        qmap = lambda b, j: (0, 0, 0, 0)
        xmap = lambda b, j: (0, 0, 0)
        omap = lambda b, j: (0, b, 0)
        lmap = lambda b, j: (0, 0, 0)
    else:
        grid = (Bk, nq)
        qmap = lambda b, j: (b, j, 0, 0)
        xmap = lambda b, j: (b, j, 0)
        omap = xmap
        lmap = lambda b, j: (j, 0, 0)
    in_specs = [
        pl.BlockSpec((1, 1, tq), lmap),
        pl.BlockSpec((1, tq, D), xmap),
        pl.BlockSpec((1, 1, _NQ, tq), qmap),
        pl.BlockSpec((1, 1, _NQI, tq), qmap),
        pl.BlockSpec((1, 1, IDX_HEADS, tq), qmap),
        pl.BlockSpec((1, A_KV_HEADS, L, A_HEAD_DIM), lambda b, j: (b, 0, 0, 0)),
        pl.BlockSpec((1, A_KV_HEADS, V_AUG_ROWS, L), lambda b, j: (b, 0, 0, 0)),
        pl.BlockSpec((1, L, IDX_DIM), lambda b, j: (b, 0, 0)),
        pl.BlockSpec((_NQ, D), lambda b, j: (0, 0)),
    ]
    kern = functools.partial(_dsa_attn_kernel, topk=topk, causal=causal,
                             n_chunks_total=L // DSA_KEY_CHUNK, rows_out=rows_out)
    return pl.pallas_call(
        kern, grid=grid, in_specs=in_specs, name="dsa_attend",
        out_specs=pl.BlockSpec((1, rows_out, D), omap),
        out_shape=jax.ShapeDtypeStruct(x.shape, F32),
        scratch_shapes=[pltpu.VMEM((L, tq), I32), pltpu.VMEM((SUBLANES, tq), I32),
                        pltpu.VMEM((_NQ, tq), F32), pltpu.VMEM((tq, D), F32),
                        pltpu.VMEM((A_KV_HEADS, V_AUG_ROWS, A_GROUP * tq), F32),
                        pltpu.VMEM((L, tq), BF16),
                        pltpu.VMEM((A_KV_HEADS, DSA_KEY_CHUNK, A_GROUP * tq), F32),
                        pltpu.VMEM((A_KV_HEADS, DSA_KEY_CHUNK, A_GROUP * tq), F32),
                        pltpu.VMEM((L, tq), I16), pltpu.VMEM((L, tq), I16),
                        pltpu.VMEM((DSA_KEY_CHUNK, IDX_HEADS * tq), F32),
                        pltpu.VMEM((DSA_KEY_CHUNK, IDX_HEADS * tq), F32)],
        compiler_params=_params("arbitrary", "arbitrary"),
    )(lim, x, qT, qiT, wT, kg, vT, kib, wo)


MLP_FF_CHUNK = 1024


def _mlp_kernel(x_ref, g_ref, wu_ref, wd_ref, gf_ref, out_ref, *, final_norm):
    x = x_ref[...]
    xn = _rms(x, g_ref[...]).astype(BF16)
    acc = x
    for f in range(0, wu_ref.shape[1], MLP_FF_CHUNK):
        h = jnp.maximum(_dot(xn, wu_ref[:, f:f + MLP_FF_CHUNK]), 0.0)
        acc = acc + _dot((h * h).astype(BF16), wd_ref[f:f + MLP_FF_CHUNK, :])
    if final_norm:
        acc = _rms(acc, gf_ref[...])
    out_ref[...] = acc


def _mlp(x, gain, w_up, w_down, gain_final, *, tm, final_norm):
    M, D = x.shape
    FF = w_up.shape[1]
    const = lambda i: (0, 0)
    return pl.pallas_call(
        functools.partial(_mlp_kernel, final_norm=final_norm), name="mlp",
        grid=(M // tm,),
        in_specs=[
            pl.BlockSpec((tm, D), lambda i: (i, 0)),
            pl.BlockSpec((1, D), const),
            pl.BlockSpec((D, FF), const, pipeline_mode=pl.Buffered(1)),
            pl.BlockSpec((FF, D), const, pipeline_mode=pl.Buffered(1)),
            pl.BlockSpec((1, D), const),
        ],
        out_specs=pl.BlockSpec((tm, D), lambda i: (i, 0)),
        out_shape=jax.ShapeDtypeStruct((M, D), F32),
        compiler_params=_params("arbitrary"),
    )(x, gain.reshape(1, D), w_up.astype(BF16), w_down.astype(BF16), gain_final.reshape(1, D))


_GQK = GLA_HEADS * GLA_DK
_GV = GLA_HEADS * GLA_DV


def _gla_proj_kernel(x_ref, g_ref, w_ref, wa_ref, ba_ref, q_ref, k_ref, v_ref, gate_ref, la_ref):
    xn = _rms(x_ref[...], g_ref[...]).astype(BF16)
    y = _dot(xn, w_ref[...])
    D = gate_ref.shape[-1]
    q_ref[...] = y[:, 0:_GQK]
    k_ref[...] = y[:, _GQK:2 * _GQK]
    v_ref[...] = y[:, 2 * _GQK:2 * _GQK + _GV].astype(BF16)
    o = 2 * _GQK + _GV
    gate_ref[...] = y[:, o:o + D]
    a = y[:, o + D:o + D + GLA_RANK].astype(BF16)
    z = _dot(a, wa_ref[...]) + ba_ref[...]
    la_ref[...] = (jnp.minimum(z, 0.0) - jnp.log1p(jnp.exp(-jnp.abs(z)))) * (1.0 / GLA_TAU)


def _gla_project(x, gain, w_in, w_a2, b_a, *, tm):
    M, D = x.shape
    N = w_in.shape[1]
    const = lambda i: (0, 0)
    row = lambda i: (i, 0)
    return pl.pallas_call(
        _gla_proj_kernel, grid=(M // tm,), name="gla_project",
        in_specs=[
            pl.BlockSpec((tm, D), row),
            pl.BlockSpec((1, D), const),
            pl.BlockSpec((D, N), const),
            pl.BlockSpec((GLA_RANK, _GQK), const),
            pl.BlockSpec((1, _GQK), const),
        ],
        out_specs=(
            pl.BlockSpec((tm, _GQK), row), pl.BlockSpec((tm, _GQK), row), pl.BlockSpec((tm, _GV), row),
            pl.BlockSpec((tm, D), row), pl.BlockSpec((tm, _GQK), row)),
        out_shape=(
            jax.ShapeDtypeStruct((M, _GQK), F32), jax.ShapeDtypeStruct((M, _GQK), F32),
            jax.ShapeDtypeStruct((M, _GV), BF16), jax.ShapeDtypeStruct((M, D), F32),
            jax.ShapeDtypeStruct((M, _GQK), F32)),
        compiler_params=_params("arbitrary"),
    )(x, gain.reshape(1, D), w_in.astype(BF16), w_a2.astype(BF16), b_a.reshape(1, _GQK))


def _split3(a):
    a0 = a.astype(BF16)
    r = a - a0.astype(F32)
    a1 = r.astype(BF16)
    a2 = (r - a1.astype(F32)).astype(BF16)
    return a0, a1, a2


def _gla_kernel(x_ref, q_ref, k_ref, v_ref, gate_ref, la_ref, s0_ref, ng_ref, wo_ref,
                out_ref, sout_ref, s_scr, a_scr, o_scr):
    c = pl.program_id(1)
    C = q_ref.shape[1]

    @pl.when(c == 0)
    def _():
        s_scr[...] = s0_ref[0]

    la = la_ref[0]
    row = lax.broadcasted_iota(I32, (C, C), 0)
    col = lax.broadcasted_iota(I32, (C, C), 1)
    tri = jnp.where(row >= col, 1.0, 0.0).astype(BF16)
    l0, l1, l2 = _split3(la)
    b = _dot(tri, l0) + _dot(tri, l1) + _dot(tri, l2)
    q = q_ref[0] * (GLA_DK ** -0.5)
    k = k_ref[0]
    trow = lax.broadcasted_iota(I32, (C, 1), 0)

    def block_ref(values, size, pick):
        v3 = values.reshape(C // size, size, values.shape[-1])
        return jnp.broadcast_to(v3[:, pick:pick + 1, :], v3.shape).reshape(values.shape)

    def accumulate(qh, kh, mask, first):
        for h in range(GLA_HEADS):
            sl = slice(h * GLA_DK, (h + 1) * GLA_DK)
            blk = jnp.where(mask, _dot_nt(qh[:, sl], kh[:, sl]), 0.0)
            if first:
                a_scr[h] = blk
            else:
                a_scr[h] = a_scr[h] + blk

    d = min(GLA_DIAG, C)
    before = block_ref(b - la, d, 0)
    qh = (q * jnp.exp(b - before)).astype(BF16)
    kh = (k * jnp.exp(before - b)).astype(BF16)
    sh = d.bit_length() - 1
    accumulate(qh, kh, jnp.logical_and((row >> sh) == (col >> sh), row >= col), True)
    half = d
    while half < C:
        upper = (trow & (2 * half - 1)) >= half
        split = block_ref(b, 2 * half, half - 1)
        qh = (q * jnp.exp(jnp.where(upper, b - split, -jnp.inf))).astype(BF16)
        kh = (k * jnp.exp(jnp.where(upper, -jnp.inf, split - b))).astype(BF16)
        sh = (2 * half).bit_length() - 1
        accumulate(qh, kh, (row >> sh) == (col >> sh), False)
        half *= 2

    b_end = b[C - 1:C, :]
    q_in = (q * jnp.exp(b)).astype(BF16)
    k_out = k * jnp.exp(b_end - b)
    ng = ng_ref[...]
    for h in range(GLA_HEADS):
        sl = slice(h * GLA_DK, (h + 1) * GLA_DK)
        vh = v_ref[0, :, h * GLA_DV:(h + 1) * GLA_DV]
        s_old = s_scr[h]
        o = _dot(q_in[:, sl], s_old.astype(BF16)) + _dot(a_scr[h].astype(BF16), vh)
        dec = jnp.transpose(jnp.broadcast_to(jnp.exp(b_end[:, sl]), (GLA_DK, GLA_DK)))
        dec = jnp.concatenate([dec] * (GLA_DV // GLA_DK), axis=1)
        s_scr[h] = dec * s_old + _dot(jnp.transpose(k_out[:, sl]).astype(BF16), vh)
        o = _rms(o, ng)
        gt = gate_ref[0, :, h * GLA_DV:(h + 1) * GLA_DV]
        o_scr[:, h * GLA_DV:(h + 1) * GLA_DV] = (o * (gt / (1.0 + jnp.exp(-gt)))).astype(BF16)

    out_ref[0] = x_ref[0] + _dot(o_scr[...], wo_ref[...])

    @pl.when(c == pl.num_programs(1) - 1)
    def _():
        sout_ref[0] = s_scr[...]


def _gla(x, q, k, v, gate, la, s0, norm_g, w_o, *, chunk):
    B, T, D = x.shape
    n = T // chunk
    tok = lambda b, c: (b, c, 0)
    const = lambda b, c: (0, 0)
    st = lambda b, c: (b, 0, 0, 0)
    return pl.pallas_call(
        _gla_kernel, grid=(B, n), name="gla",
        in_specs=[
            pl.BlockSpec((1, chunk, D), tok),
            pl.BlockSpec((1, chunk, _GQK), tok),
            pl.BlockSpec((1, chunk, _GQK), tok),
            pl.BlockSpec((1, chunk, _GV), tok),
            pl.BlockSpec((1, chunk, D), tok),
            pl.BlockSpec((1, chunk, _GQK), tok),
            pl.BlockSpec((1, GLA_HEADS, GLA_DK, GLA_DV), st),
            pl.BlockSpec((1, GLA_DV), const),
            pl.BlockSpec((_GV, D), const),
        ],
        out_specs=(pl.BlockSpec((1, chunk, D), tok), pl.BlockSpec((1, GLA_HEADS, GLA_DK, GLA_DV), st)),
        out_shape=(jax.ShapeDtypeStruct((B, T, D), F32),
                   jax.ShapeDtypeStruct((B, GLA_HEADS, GLA_DK, GLA_DV), F32)),
        scratch_shapes=[pltpu.VMEM((GLA_HEADS, GLA_DK, GLA_DV), F32),
                        pltpu.VMEM((GLA_HEADS, chunk, chunk), F32),
                        pltpu.VMEM((chunk, _GV), BF16)],
        compiler_params=_params("arbitrary", "arbitrary"),
    )(x, q, k, v, gate, la, s0, norm_g.reshape(1, GLA_DV), w_o.astype(BF16))


def _pick_tile(n, candidates):
    for t in candidates:
        if n % t == 0:
            return t
    return n


def _round_up(n, m):
    return (n + m - 1) // m * m


def kernel(x_prompt, x_sample, cache_k, cache_v, cache_kidx, state_gla, norm_mix, norm_ffn, norm_final,
           a_w_in, a_w_o, b_w_in, b_w_a2, b_b_a, b_norm, b_w_o, ffn_w_up, ffn_w_down):
    B, S, D = x_prompt.shape
    Bs, Ts, _ = x_sample.shape
    past = cache_k.shape[2]
    Ms = Bs * Ts
    tq = DSA_Q_TILE
    assert S % tq == 0 and Ms == tq and Ts % SUBLANES == 0

    wo_a = a_w_o[0].astype(BF16)
    tm = _pick_tile(S, (512, 256, 128))
    qT, qiT, wT, vT, k_p, v_p, ki_p, kg, kib = _dsa_project(x_prompt, norm_mix[0], a_w_in[0], tm=tm, tq=tq)
    pos = jnp.arange(S, dtype=I32)
    lim_p = ((pos // STREAM_CHUNK + 1) * STREAM_CHUNK).reshape(S // tq, 1, tq)
    L_p = _round_up(S, 2 * DSA_KEY_CHUNK)
    if L_p != S:
        pad = L_p - S
        kg = jnp.pad(kg, ((0, 0), (0, 0), (0, pad), (0, 0)))
        vT = jnp.pad(vT, ((0, 0), (0, 0), (0, 0), (0, pad)))
        kib = jnp.pad(kib, ((0, 0), (0, pad), (0, 0)))
    hp = _dsa_attend(x_prompt, qT, qiT, wT, lim_p, kg, vT, kib, wo_a,
                     topk=min(TOPK_MAX, S // 4), causal=True, shared_queries=False, rows_out=tq)
    xs = x_sample.reshape(1, Ms, D)
    qT_s, qiT_s, wT_s, _, k_s, v_s, ki_s, _, _ = _dsa_project(xs, norm_mix[0], a_w_in[0], tm=Ms, tq=tq)
    k_s = k_s.reshape(Bs, Ts, _NKV)
    v_s = v_s.reshape(Bs, Ts, _NKV)
    ki_s = ki_s.reshape(Bs, Ts, IDX_DIM)
    L_real = past + Ts
    L_s = _round_up(L_real, 2 * DSA_KEY_CHUNK)
    padn = L_s - L_real
    k_all = jnp.concatenate([cache_k[0].reshape(Bs, past, _NKV), k_s, jnp.zeros((Bs, padn, _NKV), F32)], axis=1)
    v_all = jnp.concatenate([cache_v[0].reshape(Bs, past, _NKV), v_s, jnp.zeros((Bs, padn, _NKV), F32)], axis=1)
    ki_all = jnp.concatenate([cache_kidx[0], ki_s, jnp.zeros((Bs, padn, IDX_DIM), F32)], axis=1)
    kg_s = k_all.reshape(Bs, L_s, A_KV_HEADS, A_HEAD_DIM).transpose(0, 2, 1, 3).astype(BF16)
    vT_s = jnp.concatenate(
        [v_all.reshape(Bs, L_s, A_KV_HEADS, A_HEAD_DIM).transpose(0, 2, 3, 1),
         jnp.ones((Bs, A_KV_HEADS, V_AUG_ROWS - A_HEAD_DIM, L_s), F32)], axis=2).astype(BF16)
    kib_s = ki_all.astype(BF16)
    pos_s = past + jnp.arange(Ts, dtype=I32)
    lim_s = jnp.minimum((pos_s // STREAM_CHUNK + 1) * STREAM_CHUNK, L_real)
    lim_s = jnp.tile(lim_s, Bs).reshape(1, 1, Ms)
    hs = _dsa_attend(xs, qT_s, qiT_s, wT_s, lim_s, kg_s, vT_s, kib_s, wo_a,
                     topk=min(TOPK_MAX, L_real // 4), causal=False, shared_queries=True, rows_out=Ts)

    tmm = _pick_tile(B * S, (512, 256, 128))
    hp = _mlp(hp.reshape(B * S, D), norm_ffn[0], ffn_w_up[0], ffn_w_down[0], norm_final, tm=tmm, final_norm=False)
    hs = _mlp(hs.reshape(Ms, D), norm_ffn[0], ffn_w_up[0], ffn_w_down[0], norm_final, tm=Ms, final_norm=False)

    q, k, v, gate, la = _gla_project(hp, norm_mix[1], b_w_in[0], b_w_a2[0], b_b_a[0], tm=tmm)
    chunk = _pick_tile(S, (256, 128, 64))
    r3 = lambda a: a.reshape(B, S, a.shape[-1])
    s0 = jnp.zeros((B, GLA_HEADS, GLA_DK, GLA_DV), F32)
    hp, s_p = _gla(hp.reshape(B, S, D), r3(q), r3(k), r3(v), r3(gate), r3(la), s0, b_norm[0], b_w_o[0], chunk=chunk)
    q, k, v, gate, la = _gla_project(hs, norm_mix[1], b_w_in[0], b_w_a2[0], b_b_a[0], tm=Ms)
    r3 = lambda a: a.reshape(Bs, Ts, a.shape[-1])
    hs, s_s = _gla(hs.reshape(Bs, Ts, D), r3(q), r3(k), r3(v), r3(gate), r3(la), state_gla[0], b_norm[0], b_w_o[0],
                   chunk=Ts)

    y_p = _mlp(hp.reshape(B * S, D), norm_ffn[1], ffn_w_up[1], ffn_w_down[1], norm_final, tm=tmm, final_norm=True)
    y_s = _mlp(hs.reshape(Ms, D), norm_ffn[1], ffn_w_up[1], ffn_w_down[1], norm_final, tm=Ms, final_norm=True)

    return (y_p.reshape(B, S, D), y_s.reshape(Bs, Ts, D),
            k_p.reshape(1, B, S, A_KV_HEADS, A_HEAD_DIM), v_p.reshape(1, B, S, A_KV_HEADS, A_HEAD_DIM),
            ki_p.reshape(1, B, S, IDX_DIM), s_p[None],
            k_s.reshape(1, Bs, Ts, A_KV_HEADS, A_HEAD_DIM), v_s.reshape(1, Bs, Ts, A_KV_HEADS, A_HEAD_DIM),
            ki_s.reshape(1, Bs, Ts, IDX_DIM), s_s[None])
```

```python
import functools

import jax
import jax.numpy as jnp
from jax import lax
from jax.experimental import pallas as pl
from jax.experimental.pallas import tpu as pltpu

F32 = jnp.float32
BF16 = jnp.bfloat16
I32 = jnp.int32

EPS = 1e-6
STREAM_CHUNK = 64
TOPK_MAX = 256
A_HEADS = 16
A_KV_HEADS = 4
A_GROUP = A_HEADS // A_KV_HEADS
A_HEAD_DIM = 64
IDX_HEADS = 8
IDX_DIM = 64
GLA_HEADS = 4
GLA_DK = 128
GLA_DV = 256
GLA_RANK = 16
GLA_TAU = 16.0

LANES = 128
SUBLANES = 8
INT_MIN = -(2 ** 31)
NEG_BIG = -1e30

DSA_Q_TILE = LANES
DSA_KEY_CHUNK = 256
GLA_DIAG = 8
VMEM_LIMIT = 56 * 1024 * 1024


def _params(*sem):
    return pltpu.CompilerParams(dimension_semantics=sem, vmem_limit_bytes=VMEM_LIMIT)


def _rms(x, g):
    ms = jnp.mean(x * x, axis=-1, keepdims=True)
    return x * lax.rsqrt(ms + EPS) * g


def _dot(a, b):
    return jnp.dot(a, b, preferred_element_type=F32)


def _dot_nt(a, b):
    return lax.dot_general(a, b, (((1,), (1,)), ((), ())), preferred_element_type=F32)


_NQ = A_HEADS * A_HEAD_DIM
_NKV = A_KV_HEADS * A_HEAD_DIM
_NQI = IDX_HEADS * IDX_DIM
_T_ROWS = _NQ + _NQI + _NKV + 16
BF16_SUBLANES = 16
V_AUG_ROWS = A_HEAD_DIM + BF16_SUBLANES
LOG2E = 1.4426950408889634
MASKED = -1e30
M_INIT = -1e29


def _dsa_proj_kernel(x_ref, g_ref, wt_ref, wn_ref,
                     qT_ref, qiT_ref, wT_ref, vT_ref, k_ref, v_ref, ki_ref, kg_ref, kib_ref, *, tq):
    xn = _rms(x_ref[0], g_ref[...]).astype(BF16)
    tm = xn.shape[0]
    yT = _dot_nt(wt_ref[...], xn)
    y = _dot(xn, wn_ref[...])
    for j in range(tm // tq):
        sl = slice(j * tq, (j + 1) * tq)
        qT_ref[0, j] = yT[0:_NQ, sl].astype(BF16)
        qiT_ref[0, j] = yT[_NQ:_NQ + _NQI, sl].astype(BF16)
        wT_ref[0, j] = yT[_NQ + _NQI + _NKV:_NQ + _NQI + _NKV + IDX_HEADS, sl]
    for h in range(A_KV_HEADS):
        r0 = _NQ + _NQI + h * A_HEAD_DIM
        vT_ref[0, h, 0:A_HEAD_DIM, :] = yT[r0:r0 + A_HEAD_DIM, :].astype(BF16)
        vT_ref[0, h, A_HEAD_DIM:V_AUG_ROWS, :] = jnp.ones((V_AUG_ROWS - A_HEAD_DIM, tm), BF16)
    k = y[:, 0:_NKV]
    ki = y[:, 2 * _NKV:2 * _NKV + IDX_DIM]
    k_ref[0] = k
    v_ref[0] = y[:, _NKV:2 * _NKV]
    ki_ref[0] = ki
    for h in range(A_KV_HEADS):
        kg_ref[0, h] = k[:, h * A_HEAD_DIM:(h + 1) * A_HEAD_DIM].astype(BF16)
    kib_ref[0] = ki.astype(BF16)


def _dsa_project(x, gain, w_in, *, tm, tq):
    B, S, D = x.shape
    o = 0
    parts = []
    for n in (_NQ, _NKV, _NKV, _NQI, IDX_DIM, IDX_HEADS):
        parts.append(w_in[:, o:o + n])
        o += n
    w_q, w_k, w_v, w_qi, w_ki, w_wt = parts
    wt_all = jnp.concatenate(
        [w_q * (A_HEAD_DIM ** -0.5 * LOG2E), w_qi, w_v, w_wt, jnp.zeros((D, 16 - IDX_HEADS), F32)],
        axis=1).T.astype(BF16)
    wn_all = jnp.concatenate([w_k, w_v, w_ki], axis=1).astype(BF16)
    nq = S // tq
    nt = S // tm
    jq = tm // tq
    out_shape = (
        jax.ShapeDtypeStruct((B, nq, _NQ, tq), BF16),
        jax.ShapeDtypeStruct((B, nq, _NQI, tq), BF16),
        jax.ShapeDtypeStruct((B, nq, IDX_HEADS, tq), F32),
        jax.ShapeDtypeStruct((B, A_KV_HEADS, V_AUG_ROWS, S), BF16),
        jax.ShapeDtypeStruct((B, S, _NKV), F32),
        jax.ShapeDtypeStruct((B, S, _NKV), F32),
        jax.ShapeDtypeStruct((B, S, IDX_DIM), F32),
        jax.ShapeDtypeStruct((B, A_KV_HEADS, S, A_HEAD_DIM), BF16),
        jax.ShapeDtypeStruct((B, S, IDX_DIM), BF16),
    )
    out_specs = (
        pl.BlockSpec((1, jq, _NQ, tq), lambda b, i: (b, i, 0, 0)),
        pl.BlockSpec((1, jq, _NQI, tq), lambda b, i: (b, i, 0, 0)),
        pl.BlockSpec((1, jq, IDX_HEADS, tq), lambda b, i: (b, i, 0, 0)),
        pl.BlockSpec((1, A_KV_HEADS, V_AUG_ROWS, tm), lambda b, i: (b, 0, 0, i)),
        pl.BlockSpec((1, tm, _NKV), lambda b, i: (b, i, 0)),
        pl.BlockSpec((1, tm, _NKV), lambda b, i: (b, i, 0)),
        pl.BlockSpec((1, tm, IDX_DIM), lambda b, i: (b, i, 0)),
        pl.BlockSpec((1, A_KV_HEADS, tm, A_HEAD_DIM), lambda b, i: (b, 0, i, 0)),
        pl.BlockSpec((1, tm, IDX_DIM), lambda b, i: (b, i, 0)),
    )
    in_specs = [
        pl.BlockSpec((1, tm, D), lambda b, i: (b, i, 0)),
        pl.BlockSpec((1, D), lambda b, i: (0, 0)),
        pl.BlockSpec((_T_ROWS, D), lambda b, i: (0, 0)),
        pl.BlockSpec((D, 2 * _NKV + IDX_DIM), lambda b, i: (0, 0)),
    ]
    return pl.pallas_call(
        functools.partial(_dsa_proj_kernel, tq=tq), name="dsa_project",
        grid=(B, nt), in_specs=in_specs, out_specs=out_specs, out_shape=out_shape,
        compiler_params=_params("arbitrary", "arbitrary"),
    )(x, gain.reshape(1, D), wt_all, wn_all)


def _dsa_attn_kernel(lim_ref, x_ref, qT_ref, qiT_ref, wT_ref, kg_ref, vT_ref, kib_ref, wo_ref,
                     out_ref, key_scr, tie_scr, oT_scr, res_scr, acc_scr, bias_scr, sa_scr, sb_scr,
                     ia_scr, ib_scr,
                     *, topk, causal, n_chunks_total, rows_out):
    tq = qT_ref.shape[-1]
    lc = DSA_KEY_CHUNK
    if causal:
        n_chunks = jnp.minimum(2 * (((pl.program_id(1) + 1) * tq + 2 * lc - 1) // (2 * lc)), n_chunks_total)
    else:
        n_chunks = n_chunks_total
    lim = lim_ref[0]

    def chunk_start(c):
        return pl.multiple_of(c * lc, lc)

    def key_index(c):
        return c * lc + lax.broadcasted_iota(I32, (lc, tq), 0)

    qi_all = jnp.concatenate(
        [qiT_ref[0, 0, h * IDX_DIM:(h + 1) * IDX_DIM, :] for h in range(IDX_HEADS)], axis=1)
    w_rows = wT_ref[0, 0] * ((IDX_DIM ** -0.5) * (IDX_HEADS ** -0.5))

    def idx_logits(c):
        return _dot(kib_ref[0, pl.ds(chunk_start(c), lc), :], qi_all)

    def store_keys(s_ref, c):
        off = chunk_start(c)
        score = jnp.maximum(s_ref[:, 0:tq], 0.0) * w_rows[0:1, :]
        for h in range(1, IDX_HEADS):
            score = score + jnp.maximum(s_ref[:, h * tq:(h + 1) * tq], 0.0) * w_rows[h:h + 1, :]
        key_scr[pl.ds(off, lc), :] = jnp.where(key_index(c) < lim, score, jnp.nan)

    ia_scr[...] = idx_logits(0)

    def score_body(i, carry):
        c0 = 2 * i
        ib_scr[...] = idx_logits(c0 + 1)
        store_keys(ia_scr, c0)
        ia_scr[...] = idx_logits(jnp.minimum(c0 + 2, n_chunks - 1))
        store_keys(ib_scr, c0 + 1)
        return carry

    lax.fori_loop(0, n_chunks // 2, score_body, 0)

    def count(pred):
        def body(i, acc):
            flags = []
            for j in range(2):
                c = 2 * i + j
                kk = key_scr[pl.ds(chunk_start(c), lc), :]
                m = jnp.where(pred(kk, c), 1, 0).astype(I32)
                flags += [m[r * SUBLANES:(r + 1) * SUBLANES, :] for r in range(lc // SUBLANES)]
            while len(flags) > 1:
                flags = [a + b for a, b in zip(flags[0::2], flags[1::2])]
            return acc + flags[0]
        acc = lax.fori_loop(0, n_chunks // 2, body, jnp.zeros((SUBLANES, tq), I32))
        return jnp.sum(acc, axis=0, keepdims=True)

    def code_to_float(code):
        bits = jnp.where(code < 0, (-code) | INT_MIN, code)
        return pltpu.bitcast(bits, F32)

    def bit_body(i, carry):
        code, cge = carry
        cand = code + jnp.left_shift(jnp.int32(1), 31 - i)
        cand_f = code_to_float(cand)
        c = count(lambda kk, _: kk >= cand_f)
        ok = c >= topk
        return jnp.where(ok, cand, code), jnp.where(ok, c, cge)

    code0 = jnp.full((1, tq), INT_MIN, I32)
    cge0 = jnp.zeros((1, tq), I32) + n_chunks * lc
    code, cge = lax.fori_loop(0, 32, bit_body, (code0, cge0))
    thr = jnp.where(code > INT_MIN, code_to_float(code), -jnp.inf)

    n_idx_bits = max(1, (n_chunks_total * lc - 1).bit_length())
    tie_scr[...] = jnp.full((SUBLANES, tq), 2 ** 30, I32)
    surplus = jnp.logical_and(cge > topk, code > INT_MIN)

    @pl.when(jnp.max(jnp.where(surplus, 1, 0)) > 0)
    def _():
        cgt = count(lambda kk, _: kk > thr)
        want = topk - cgt
        bound = jnp.zeros((1, tq), I32)
        for bit in range(n_idx_bits - 1, -1, -1):
            cand = bound + (1 << bit)
            c = count(lambda kk, cc: jnp.logical_and(kk == thr, key_index(cc) < cand))
            bound = jnp.where(c < want, cand, bound)
        tie_scr[...] = jnp.broadcast_to(bound, (SUBLANES, tq))

    tie_bound = tie_scr[0:1, :]

    def bias_body(c, carry):
        off = chunk_start(c)
        kk = key_scr[pl.ds(off, lc), :]
        tie = jnp.where(key_index(c) <= tie_bound, 0.0, MASKED)
        b = jnp.where(kk > thr, 0.0, jnp.where(kk == thr, tie, MASKED))
        bias_scr[pl.ds(off, lc), :] = b.astype(BF16)
        return carry

    lax.fori_loop(0, n_chunks, bias_body, 0)

    gw = A_GROUP * tq
    eye = jnp.where(lax.broadcasted_iota(I32, (tq, gw), 0) == (lax.broadcasted_iota(I32, (tq, gw), 1) & (tq - 1)),
                    1.0, 0.0).astype(BF16)
    q_aug = [jnp.concatenate(
        [eye, jnp.concatenate(
            [qT_ref[0, 0, (g * A_GROUP + h) * A_HEAD_DIM:(g * A_GROUP + h + 1) * A_HEAD_DIM, :]
             for h in range(A_GROUP)], axis=1)], axis=0) for g in range(A_KV_HEADS)]
    acc_scr[...] = jnp.zeros(acc_scr.shape, F32)

    def logits(c, g):
        off = chunk_start(c)
        k_aug = jnp.concatenate([bias_scr[pl.ds(off, lc), :], kg_ref[0, g, pl.ds(off, lc), :]], axis=1)
        return _dot(k_aug, q_aug[g])

    def softmax_pv(s, c, g, m):
        mn = jnp.maximum(m, jnp.max(s, axis=0, keepdims=True))
        alpha = jnp.exp2(m - mn)
        p = jnp.exp2(s - mn).astype(BF16)
        vt = vT_ref[0, g, :, pl.ds(chunk_start(c), lc)]
        acc_scr[g] = alpha * acc_scr[g] + _dot(vt, p)
        return mn

    for g in range(A_KV_HEADS):
        sa_scr[g] = logits(0, g)

    def att_body(i, ms):
        c0 = 2 * i
        ms = list(ms)
        for g in range(A_KV_HEADS):
            sb_scr[g] = logits(c0 + 1, g)
            ms[g] = softmax_pv(sa_scr[g], c0, g, ms[g])
        c2 = jnp.minimum(c0 + 2, n_chunks - 1)
        for g in range(A_KV_HEADS):
            sa_scr[g] = logits(c2, g)
            ms[g] = softmax_pv(sb_scr[g], c0 + 1, g, ms[g])
        return tuple(ms)

    m0 = tuple(jnp.full((1, gw), M_INIT, F32) for _ in range(A_KV_HEADS))
    lax.fori_loop(0, n_chunks // 2, att_body, m0)
    for g in range(A_KV_HEADS):
        acc = acc_scr[g]
        o = acc[0:A_HEAD_DIM, :] / acc[A_HEAD_DIM:A_HEAD_DIM + 1, :]
        for h in range(A_GROUP):
            r0 = (g * A_GROUP + h) * A_HEAD_DIM
            oT_scr[r0:r0 + A_HEAD_DIM, :] = o[:, h * tq:(h + 1) * tq]

    o = jnp.transpose(oT_scr[...]).astype(BF16)
    res = x_ref[0] + _dot(o, wo_ref[...])
    if rows_out == tq:
        out_ref[0] = res
    else:
        res_scr[...] = res
        r0 = pl.multiple_of(pl.program_id(0) * rows_out, rows_out)
        out_ref[0] = res_scr[pl.ds(r0, rows_out), :]


def _dsa_attend(x, qT, qiT, wT, lim, kg, vT, kib, wo, *, topk, causal, shared_queries, rows_out):
    Bk, _, L, _ = kg.shape
    D = x.shape[-1]
    tq = qT.shape[-1]
    nq = qT.shape[1]
    assert L % (2 * DSA_KEY_CHUNK) == 0
    if shared_queries:
        grid = (Bk, 1)
        qmap = lambda b, j: (0, 0, 0, 0)
        xmap = lambda b, j: (0, 0, 0)
        omap = lambda b, j: (0, b, 0)
        lmap = lambda b, j: (0, 0, 0)
    else:
        grid = (Bk, nq)
        qmap = lambda b, j: (b, j, 0, 0)
        xmap = lambda b, j: (b, j, 0)
        omap = xmap
        lmap = lambda b, j: (j, 0, 0)
    in_specs = [
        pl.BlockSpec((1, 1, tq), lmap),
        pl.BlockSpec((1, tq, D), xmap),
        pl.BlockSpec((1, 1, _NQ, tq), qmap),
        pl.BlockSpec((1, 1, _NQI, tq), qmap),
        pl.BlockSpec((1, 1, IDX_HEADS, tq), qmap),
        pl.BlockSpec((1, A_KV_HEADS, L, A_HEAD_DIM), lambda b, j: (b, 0, 0, 0)),
        pl.BlockSpec((1, A_KV_HEADS, V_AUG_ROWS, L), lambda b, j: (b, 0, 0, 0)),
        pl.BlockSpec((1, L, IDX_DIM), lambda b, j: (b, 0, 0)),
        pl.BlockSpec((_NQ, D), lambda b, j: (0, 0)),
    ]
    kern = functools.partial(_dsa_attn_kernel, topk=topk, causal=causal,
                             n_chunks_total=L // DSA_KEY_CHUNK, rows_out=rows_out)
    return pl.pallas_call(
        kern, grid=grid, in_specs=in_specs, name="dsa_attend",
        out_specs=pl.BlockSpec((1, rows_out, D), omap),
        out_shape=jax.ShapeDtypeStruct(x.shape, F32),
        scratch_shapes=[pltpu.VMEM((L, tq), F32), pltpu.VMEM((SUBLANES, tq), I32),
                        pltpu.VMEM((_NQ, tq), F32), pltpu.VMEM((tq, D), F32),
                        pltpu.VMEM((A_KV_HEADS, V_AUG_ROWS, A_GROUP * tq), F32),
                        pltpu.VMEM((L, tq), BF16),
                        pltpu.VMEM((A_KV_HEADS, DSA_KEY_CHUNK, A_GROUP * tq), F32),
                        pltpu.VMEM((A_KV_HEADS, DSA_KEY_CHUNK, A_GROUP * tq), F32),
                        pltpu.VMEM((DSA_KEY_CHUNK, IDX_HEADS * tq), F32),
                        pltpu.VMEM((DSA_KEY_CHUNK, IDX_HEADS * tq), F32)],
        compiler_params=_params("arbitrary", "arbitrary"),
    )(lim, x, qT, qiT, wT, kg, vT, kib, wo)


MLP_FF_CHUNK = 1024


def _mlp_kernel(x_ref, g_ref, wu_ref, wd_ref, gf_ref, out_ref, *, final_norm):
    x = x_ref[...]
    xn = _rms(x, g_ref[...]).astype(BF16)
    acc = x
    for f in range(0, wu_ref.shape[1], MLP_FF_CHUNK):
        h = jnp.maximum(_dot(xn, wu_ref[:, f:f + MLP_FF_CHUNK]), 0.0)
        acc = acc + _dot((h * h).astype(BF16), wd_ref[f:f + MLP_FF_CHUNK, :])
    if final_norm:
        acc = _rms(acc, gf_ref[...])
    out_ref[...] = acc


def _mlp(x, gain, w_up, w_down, gain_final, *, tm, final_norm):
    M, D = x.shape
    FF = w_up.shape[1]
    const = lambda i: (0, 0)
    return pl.pallas_call(
        functools.partial(_mlp_kernel, final_norm=final_norm), name="mlp",
        grid=(M // tm,),
        in_specs=[
            pl.BlockSpec((tm, D), lambda i: (i, 0)),
            pl.BlockSpec((1, D), const),
            pl.BlockSpec((D, FF), const, pipeline_mode=pl.Buffered(1)),
            pl.BlockSpec((FF, D), const, pipeline_mode=pl.Buffered(1)),
            pl.BlockSpec((1, D), const),
        ],
        out_specs=pl.BlockSpec((tm, D), lambda i: (i, 0)),
        out_shape=jax.ShapeDtypeStruct((M, D), F32),
        compiler_params=_params("arbitrary"),
    )(x, gain.reshape(1, D), w_up.astype(BF16), w_down.astype(BF16), gain_final.reshape(1, D))


_GQK = GLA_HEADS * GLA_DK
_GV = GLA_HEADS * GLA_DV


def _gla_proj_kernel(x_ref, g_ref, w_ref, wa_ref, ba_ref, q_ref, k_ref, v_ref, gate_ref, la_ref):
    xn = _rms(x_ref[...], g_ref[...]).astype(BF16)
    y = _dot(xn, w_ref[...])
    D = gate_ref.shape[-1]
    q_ref[...] = y[:, 0:_GQK]
    k_ref[...] = y[:, _GQK:2 * _GQK]
    v_ref[...] = y[:, 2 * _GQK:2 * _GQK + _GV].astype(BF16)
    o = 2 * _GQK + _GV
    gate_ref[...] = y[:, o:o + D]
    a = y[:, o + D:o + D + GLA_RANK].astype(BF16)
    z = _dot(a, wa_ref[...]) + ba_ref[...]
    la_ref[...] = (jnp.minimum(z, 0.0) - jnp.log1p(jnp.exp(-jnp.abs(z)))) * (1.0 / GLA_TAU)


def _gla_project(x, gain, w_in, w_a2, b_a, *, tm):
    M, D = x.shape
    N = w_in.shape[1]
    const = lambda i: (0, 0)
    row = lambda i: (i, 0)
    return pl.pallas_call(
        _gla_proj_kernel, grid=(M // tm,), name="gla_project",
        in_specs=[
            pl.BlockSpec((tm, D), row),
            pl.BlockSpec((1, D), const),
            pl.BlockSpec((D, N), const),
            pl.BlockSpec((GLA_RANK, _GQK), const),
            pl.BlockSpec((1, _GQK), const),
        ],
        out_specs=(
            pl.BlockSpec((tm, _GQK), row), pl.BlockSpec((tm, _GQK), row), pl.BlockSpec((tm, _GV), row),
            pl.BlockSpec((tm, D), row), pl.BlockSpec((tm, _GQK), row)),
        out_shape=(
            jax.ShapeDtypeStruct((M, _GQK), F32), jax.ShapeDtypeStruct((M, _GQK), F32),
            jax.ShapeDtypeStruct((M, _GV), BF16), jax.ShapeDtypeStruct((M, D), F32),
            jax.ShapeDtypeStruct((M, _GQK), F32)),
        compiler_params=_params("arbitrary"),
    )(x, gain.reshape(1, D), w_in.astype(BF16), w_a2.astype(BF16), b_a.reshape(1, _GQK))


def _split3(a):
    a0 = a.astype(BF16)
    r = a - a0.astype(F32)
    a1 = r.astype(BF16)
    a2 = (r - a1.astype(F32)).astype(BF16)
    return a0, a1, a2


def _gla_kernel(x_ref, q_ref, k_ref, v_ref, gate_ref, la_ref, s0_ref, ng_ref, wo_ref,
                out_ref, sout_ref, s_scr, a_scr, o_scr):
    c = pl.program_id(1)
    C = q_ref.shape[1]

    @pl.when(c == 0)
    def _():
        s_scr[...] = s0_ref[0]

    la = la_ref[0]
    row = lax.broadcasted_iota(I32, (C, C), 0)
    col = lax.broadcasted_iota(I32, (C, C), 1)
    tri = jnp.where(row >= col, 1.0, 0.0).astype(BF16)
    l0, l1, l2 = _split3(la)
    b = _dot(tri, l0) + _dot(tri, l1) + _dot(tri, l2)
    q = q_ref[0] * (GLA_DK ** -0.5)
    k = k_ref[0]
    trow = lax.broadcasted_iota(I32, (C, 1), 0)

    def block_ref(values, size, pick):
        v3 = values.reshape(C // size, size, values.shape[-1])
        return jnp.broadcast_to(v3[:, pick:pick + 1, :], v3.shape).reshape(values.shape)

    def accumulate(qh, kh, mask, first):
        for h in range(GLA_HEADS):
            sl = slice(h * GLA_DK, (h + 1) * GLA_DK)
            blk = jnp.where(mask, _dot_nt(qh[:, sl], kh[:, sl]), 0.0)
            if first:
                a_scr[h] = blk
            else:
                a_scr[h] = a_scr[h] + blk

    d = min(GLA_DIAG, C)
    before = block_ref(b - la, d, 0)
    qh = (q * jnp.exp(b - before)).astype(BF16)
    kh = (k * jnp.exp(before - b)).astype(BF16)
    sh = d.bit_length() - 1
    accumulate(qh, kh, jnp.logical_and((row >> sh) == (col >> sh), row >= col), True)
    half = d
    while half < C:
        upper = (trow & (2 * half - 1)) >= half
        split = block_ref(b, 2 * half, half - 1)
        qh = (q * jnp.exp(jnp.where(upper, b - split, -jnp.inf))).astype(BF16)
        kh = (k * jnp.exp(jnp.where(upper, -jnp.inf, split - b))).astype(BF16)
        sh = (2 * half).bit_length() - 1
        accumulate(qh, kh, (row >> sh) == (col >> sh), False)
        half *= 2

    b_end = b[C - 1:C, :]
    q_in = (q * jnp.exp(b)).astype(BF16)
    k_out = k * jnp.exp(b_end - b)
    ng = ng_ref[...]
    for h in range(GLA_HEADS):
        sl = slice(h * GLA_DK, (h + 1) * GLA_DK)
        vh = v_ref[0, :, h * GLA_DV:(h + 1) * GLA_DV]
        s_old = s_scr[h]
        o = _dot(q_in[:, sl], s_old.astype(BF16)) + _dot(a_scr[h].astype(BF16), vh)
        dec = jnp.transpose(jnp.broadcast_to(jnp.exp(b_end[:, sl]), (GLA_DK, GLA_DK)))
        dec = jnp.concatenate([dec] * (GLA_DV // GLA_DK), axis=1)
        s_scr[h] = dec * s_old + _dot(jnp.transpose(k_out[:, sl]).astype(BF16), vh)
        o = _rms(o, ng)
        gt = gate_ref[0, :, h * GLA_DV:(h + 1) * GLA_DV]
        o_scr[:, h * GLA_DV:(h + 1) * GLA_DV] = (o * (gt / (1.0 + jnp.exp(-gt)))).astype(BF16)

    out_ref[0] = x_ref[0] + _dot(o_scr[...], wo_ref[...])

    @pl.when(c == pl.num_programs(1) - 1)
    def _():
        sout_ref[0] = s_scr[...]


def _gla(x, q, k, v, gate, la, s0, norm_g, w_o, *, chunk):
    B, T, D = x.shape
    n = T // chunk
    tok = lambda b, c: (b, c, 0)
    const = lambda b, c: (0, 0)
    st = lambda b, c: (b, 0, 0, 0)
    return pl.pallas_call(
        _gla_kernel, grid=(B, n), name="gla",
        in_specs=[
            pl.BlockSpec((1, chunk, D), tok),
            pl.BlockSpec((1, chunk, _GQK), tok),
            pl.BlockSpec((1, chunk, _GQK), tok),
            pl.BlockSpec((1, chunk, _GV), tok),
            pl.BlockSpec((1, chunk, D), tok),
            pl.BlockSpec((1, chunk, _GQK), tok),
            pl.BlockSpec((1, GLA_HEADS, GLA_DK, GLA_DV), st),
            pl.BlockSpec((1, GLA_DV), const),
            pl.BlockSpec((_GV, D), const),
        ],
        out_specs=(pl.BlockSpec((1, chunk, D), tok), pl.BlockSpec((1, GLA_HEADS, GLA_DK, GLA_DV), st)),
        out_shape=(jax.ShapeDtypeStruct((B, T, D), F32),
                   jax.ShapeDtypeStruct((B, GLA_HEADS, GLA_DK, GLA_DV), F32)),
        scratch_shapes=[pltpu.VMEM((GLA_HEADS, GLA_DK, GLA_DV), F32),
                        pltpu.VMEM((GLA_HEADS, chunk, chunk), F32),
                        pltpu.VMEM((chunk, _GV), BF16)],
        compiler_params=_params("arbitrary", "arbitrary"),
    )(x, q, k, v, gate, la, s0, norm_g.reshape(1, GLA_DV), w_o.astype(BF16))


def _pick_tile(n, candidates):
    for t in candidates:
        if n % t == 0:
            return t
    return n


def _round_up(n, m):
    return (n + m - 1) // m * m


def kernel(x_prompt, x_sample, cache_k, cache_v, cache_kidx, state_gla, norm_mix, norm_ffn, norm_final,
           a_w_in, a_w_o, b_w_in, b_w_a2, b_b_a, b_norm, b_w_o, ffn_w_up, ffn_w_down):
    B, S, D = x_prompt.shape
    Bs, Ts, _ = x_sample.shape
    past = cache_k.shape[2]
    Ms = Bs * Ts
    tq = DSA_Q_TILE
    assert S % tq == 0 and Ms == tq and Ts % SUBLANES == 0

    wo_a = a_w_o[0].astype(BF16)
    tm = _pick_tile(S, (512, 256, 128))
    qT, qiT, wT, vT, k_p, v_p, ki_p, kg, kib = _dsa_project(x_prompt, norm_mix[0], a_w_in[0], tm=tm, tq=tq)
    pos = jnp.arange(S, dtype=I32)
    lim_p = ((pos // STREAM_CHUNK + 1) * STREAM_CHUNK).reshape(S // tq, 1, tq)
    L_p = _round_up(S, 2 * DSA_KEY_CHUNK)
    if L_p != S:
        pad = L_p - S
        kg = jnp.pad(kg, ((0, 0), (0, 0), (0, pad), (0, 0)))
        vT = jnp.pad(vT, ((0, 0), (0, 0), (0, 0), (0, pad)))
        kib = jnp.pad(kib, ((0, 0), (0, pad), (0, 0)))
    hp = _dsa_attend(x_prompt, qT, qiT, wT, lim_p, kg, vT, kib, wo_a,
                     topk=min(TOPK_MAX, S // 4), causal=True, shared_queries=False, rows_out=tq)
    xs = x_sample.reshape(1, Ms, D)
    qT_s, qiT_s, wT_s, _, k_s, v_s, ki_s, _, _ = _dsa_project(xs, norm_mix[0], a_w_in[0], tm=Ms, tq=tq)
    k_s = k_s.reshape(Bs, Ts, _NKV)
    v_s = v_s.reshape(Bs, Ts, _NKV)
    ki_s = ki_s.reshape(Bs, Ts, IDX_DIM)
    L_real = past + Ts
    L_s = _round_up(L_real, 2 * DSA_KEY_CHUNK)
    padn = L_s - L_real
    k_all = jnp.concatenate([cache_k[0].reshape(Bs, past, _NKV), k_s, jnp.zeros((Bs, padn, _NKV), F32)], axis=1)
    v_all = jnp.concatenate([cache_v[0].reshape(Bs, past, _NKV), v_s, jnp.zeros((Bs, padn, _NKV), F32)], axis=1)
    ki_all = jnp.concatenate([cache_kidx[0], ki_s, jnp.zeros((Bs, padn, IDX_DIM), F32)], axis=1)
    kg_s = k_all.reshape(Bs, L_s, A_KV_HEADS, A_HEAD_DIM).transpose(0, 2, 1, 3).astype(BF16)
    vT_s = jnp.concatenate(
        [v_all.reshape(Bs, L_s, A_KV_HEADS, A_HEAD_DIM).transpose(0, 2, 3, 1),
         jnp.ones((Bs, A_KV_HEADS, V_AUG_ROWS - A_HEAD_DIM, L_s), F32)], axis=2).astype(BF16)
    kib_s = ki_all.astype(BF16)
    pos_s = past + jnp.arange(Ts, dtype=I32)
    lim_s = jnp.minimum((pos_s // STREAM_CHUNK + 1) * STREAM_CHUNK, L_real)
    lim_s = jnp.tile(lim_s, Bs).reshape(1, 1, Ms)
    hs = _dsa_attend(xs, qT_s, qiT_s, wT_s, lim_s, kg_s, vT_s, kib_s, wo_a,
                     topk=min(TOPK_MAX, L_real // 4), causal=False, shared_queries=True, rows_out=Ts)

    tmm = _pick_tile(B * S, (512, 256, 128))
    hp = _mlp(hp.reshape(B * S, D), norm_ffn[0], ffn_w_up[0], ffn_w_down[0], norm_final, tm=tmm, final_norm=False)
    hs = _mlp(hs.reshape(Ms, D), norm_ffn[0], ffn_w_up[0], ffn_w_down[0], norm_final, tm=Ms, final_norm=False)

    q, k, v, gate, la = _gla_project(hp, norm_mix[1], b_w_in[0], b_w_a2[0], b_b_a[0], tm=tmm)
    chunk = _pick_tile(S, (256, 128, 64))
    r3 = lambda a: a.reshape(B, S, a.shape[-1])
    s0 = jnp.zeros((B, GLA_HEADS, GLA_DK, GLA_DV), F32)
    hp, s_p = _gla(hp.reshape(B, S, D), r3(q), r3(k), r3(v), r3(gate), r3(la), s0, b_norm[0], b_w_o[0], chunk=chunk)
    q, k, v, gate, la = _gla_project(hs, norm_mix[1], b_w_in[0], b_w_a2[0], b_b_a[0], tm=Ms)
    r3 = lambda a: a.reshape(Bs, Ts, a.shape[-1])
    hs, s_s = _gla(hs.reshape(Bs, Ts, D), r3(q), r3(k), r3(v), r3(gate), r3(la), state_gla[0], b_norm[0], b_w_o[0],
                   chunk=Ts)

    y_p = _mlp(hp.reshape(B * S, D), norm_ffn[1], ffn_w_up[1], ffn_w_down[1], norm_final, tm=tmm, final_norm=True)
    y_s = _mlp(hs.reshape(Ms, D), norm_ffn[1], ffn_w_up[1], ffn_w_down[1], norm_final, tm=Ms, final_norm=True)

    return (y_p.reshape(B, S, D), y_s.reshape(Bs, Ts, D),
            k_p.reshape(1, B, S, A_KV_HEADS, A_HEAD_DIM), v_p.reshape(1, B, S, A_KV_HEADS, A_HEAD_DIM),
            ki_p.reshape(1, B, S, IDX_DIM), s_p[None],
            k_s.reshape(1, Bs, Ts, A_KV_HEADS, A_HEAD_DIM), v_s.reshape(1, Bs, Ts, A_KV_HEADS, A_HEAD_DIM),
            ki_s.reshape(1, Bs, Ts, IDX_DIM), s_s[None])
```

```python
import functools

import jax
import jax.numpy as jnp
from jax import lax
from jax.experimental import pallas as pl
from jax.experimental.pallas import tpu as pltpu

F32 = jnp.float32
BF16 = jnp.bfloat16
I32 = jnp.int32

EPS = 1e-6
STREAM_CHUNK = 64
TOPK_MAX = 256
A_HEADS = 16
A_KV_HEADS = 4
A_GROUP = A_HEADS // A_KV_HEADS
A_HEAD_DIM = 64
IDX_HEADS = 8
IDX_DIM = 64
GLA_HEADS = 4
GLA_DK = 128
GLA_DV = 256
GLA_RANK = 16
GLA_TAU = 16.0

LANES = 128
SUBLANES = 8
INT_MIN = -(2 ** 31)
NEG_BIG = -1e30

DSA_Q_TILE = LANES
DSA_KEY_CHUNK = 256
GLA_DIAG = 16
VMEM_LIMIT = 56 * 1024 * 1024


def _params(*sem):
    return pltpu.CompilerParams(dimension_semantics=sem, vmem_limit_bytes=VMEM_LIMIT)


def _rms(x, g):
    ms = jnp.mean(x * x, axis=-1, keepdims=True)
    return x * lax.rsqrt(ms + EPS) * g


def _dot(a, b):
    return jnp.dot(a, b, preferred_element_type=F32)


def _dot_nt(a, b):
    return lax.dot_general(a, b, (((1,), (1,)), ((), ())), preferred_element_type=F32)


_NQ = A_HEADS * A_HEAD_DIM
_NKV = A_KV_HEADS * A_HEAD_DIM
_NQI = IDX_HEADS * IDX_DIM
_T_ROWS = _NQ + _NQI + _NKV + 16
BF16_SUBLANES = 16
V_AUG_ROWS = A_HEAD_DIM + BF16_SUBLANES
LOG2E = 1.4426950408889634
MASKED = -1e30
M_INIT = -1e29


def _dsa_proj_kernel(x_ref, g_ref, wt_ref, wn_ref,
                     qT_ref, qiT_ref, wT_ref, vT_ref, k_ref, v_ref, ki_ref, kg_ref, kib_ref, *, tq):
    xn = _rms(x_ref[0], g_ref[...]).astype(BF16)
    tm = xn.shape[0]
    yT = _dot_nt(wt_ref[...], xn)
    y = _dot(xn, wn_ref[...])
    for j in range(tm // tq):
        sl = slice(j * tq, (j + 1) * tq)
        qT_ref[0, j] = yT[0:_NQ, sl].astype(BF16)
        qiT_ref[0, j] = yT[_NQ:_NQ + _NQI, sl].astype(BF16)
        wT_ref[0, j] = yT[_NQ + _NQI + _NKV:_NQ + _NQI + _NKV + IDX_HEADS, sl]
    for h in range(A_KV_HEADS):
        r0 = _NQ + _NQI + h * A_HEAD_DIM
        vT_ref[0, h, 0:A_HEAD_DIM, :] = yT[r0:r0 + A_HEAD_DIM, :].astype(BF16)
        vT_ref[0, h, A_HEAD_DIM:V_AUG_ROWS, :] = jnp.ones((V_AUG_ROWS - A_HEAD_DIM, tm), BF16)
    k = y[:, 0:_NKV]
    ki = y[:, 2 * _NKV:2 * _NKV + IDX_DIM]
    k_ref[0] = k
    v_ref[0] = y[:, _NKV:2 * _NKV]
    ki_ref[0] = ki
    for h in range(A_KV_HEADS):
        kg_ref[0, h] = k[:, h * A_HEAD_DIM:(h + 1) * A_HEAD_DIM].astype(BF16)
    kib_ref[0] = ki.astype(BF16)


def _dsa_project(x, gain, w_in, *, tm, tq):
    B, S, D = x.shape
    o = 0
    parts = []
    for n in (_NQ, _NKV, _NKV, _NQI, IDX_DIM, IDX_HEADS):
        parts.append(w_in[:, o:o + n])
        o += n
    w_q, w_k, w_v, w_qi, w_ki, w_wt = parts
    wt_all = jnp.concatenate(
        [w_q * (A_HEAD_DIM ** -0.5 * LOG2E), w_qi, w_v, w_wt, jnp.zeros((D, 16 - IDX_HEADS), F32)],
        axis=1).T.astype(BF16)
    wn_all = jnp.concatenate([w_k, w_v, w_ki], axis=1).astype(BF16)
    nq = S // tq
    nt = S // tm
    jq = tm // tq
    out_shape = (
        jax.ShapeDtypeStruct((B, nq, _NQ, tq), BF16),
        jax.ShapeDtypeStruct((B, nq, _NQI, tq), BF16),
        jax.ShapeDtypeStruct((B, nq, IDX_HEADS, tq), F32),
        jax.ShapeDtypeStruct((B, A_KV_HEADS, V_AUG_ROWS, S), BF16),
        jax.ShapeDtypeStruct((B, S, _NKV), F32),
        jax.ShapeDtypeStruct((B, S, _NKV), F32),
        jax.ShapeDtypeStruct((B, S, IDX_DIM), F32),
        jax.ShapeDtypeStruct((B, A_KV_HEADS, S, A_HEAD_DIM), BF16),
        jax.ShapeDtypeStruct((B, S, IDX_DIM), BF16),
    )
    out_specs = (
        pl.BlockSpec((1, jq, _NQ, tq), lambda b, i: (b, i, 0, 0)),
        pl.BlockSpec((1, jq, _NQI, tq), lambda b, i: (b, i, 0, 0)),
        pl.BlockSpec((1, jq, IDX_HEADS, tq), lambda b, i: (b, i, 0, 0)),
        pl.BlockSpec((1, A_KV_HEADS, V_AUG_ROWS, tm), lambda b, i: (b, 0, 0, i)),
        pl.BlockSpec((1, tm, _NKV), lambda b, i: (b, i, 0)),
        pl.BlockSpec((1, tm, _NKV), lambda b, i: (b, i, 0)),
        pl.BlockSpec((1, tm, IDX_DIM), lambda b, i: (b, i, 0)),
        pl.BlockSpec((1, A_KV_HEADS, tm, A_HEAD_DIM), lambda b, i: (b, 0, i, 0)),
        pl.BlockSpec((1, tm, IDX_DIM), lambda b, i: (b, i, 0)),
    )
    in_specs = [
        pl.BlockSpec((1, tm, D), lambda b, i: (b, i, 0)),
        pl.BlockSpec((1, D), lambda b, i: (0, 0)),
        pl.BlockSpec((_T_ROWS, D), lambda b, i: (0, 0)),
        pl.BlockSpec((D, 2 * _NKV + IDX_DIM), lambda b, i: (0, 0)),
    ]
    return pl.pallas_call(
        functools.partial(_dsa_proj_kernel, tq=tq), name="dsa_project",
        grid=(B, nt), in_specs=in_specs, out_specs=out_specs, out_shape=out_shape,
        compiler_params=_params("arbitrary", "arbitrary"),
    )(x, gain.reshape(1, D), wt_all, wn_all)


def _dsa_attn_kernel(lim_ref, x_ref, qT_ref, qiT_ref, wT_ref, kg_ref, vT_ref, kib_ref, wo_ref,
                     out_ref, key_scr, tie_scr, oT_scr, res_scr, acc_scr, bias_scr, sa_scr, sb_scr,
                     ia_scr, ib_scr,
                     *, topk, causal, n_chunks_total, rows_out):
    tq = qT_ref.shape[-1]
    lc = DSA_KEY_CHUNK
    if causal:
        n_chunks = jnp.minimum(2 * (((pl.program_id(1) + 1) * tq + 2 * lc - 1) // (2 * lc)), n_chunks_total)
    else:
        n_chunks = n_chunks_total
    lim = lim_ref[0]

    def chunk_start(c):
        return pl.multiple_of(c * lc, lc)

    def key_index(c):
        return c * lc + lax.broadcasted_iota(I32, (lc, tq), 0)

    qi_all = jnp.concatenate(
        [qiT_ref[0, 0, h * IDX_DIM:(h + 1) * IDX_DIM, :] for h in range(IDX_HEADS)], axis=1)
    w_rows = wT_ref[0, 0] * ((IDX_DIM ** -0.5) * (IDX_HEADS ** -0.5))

    def idx_logits(c):
        return _dot(kib_ref[0, pl.ds(chunk_start(c), lc), :], qi_all)

    def store_keys(s_ref, c):
        off = chunk_start(c)
        score = jnp.maximum(s_ref[:, 0:tq], 0.0) * w_rows[0:1, :]
        for h in range(1, IDX_HEADS):
            score = score + jnp.maximum(s_ref[:, h * tq:(h + 1) * tq], 0.0) * w_rows[h:h + 1, :]
        key_scr[pl.ds(off, lc), :] = jnp.where(key_index(c) < lim, score, -jnp.inf)

    ia_scr[...] = idx_logits(0)

    def score_body(i, carry):
        c0 = 2 * i
        ib_scr[...] = idx_logits(c0 + 1)
        store_keys(ia_scr, c0)
        ia_scr[...] = idx_logits(jnp.minimum(c0 + 2, n_chunks - 1))
        store_keys(ib_scr, c0 + 1)
        return carry

    lax.fori_loop(0, n_chunks // 2, score_body, 0)

    def count(pred):
        def body(i, acc):
            flags = []
            for j in range(2):
                c = 2 * i + j
                kk = key_scr[pl.ds(chunk_start(c), lc), :]
                m = jnp.where(pred(kk, c), 1, 0).astype(I32)
                flags += [m[r * SUBLANES:(r + 1) * SUBLANES, :] for r in range(lc // SUBLANES)]
            while len(flags) > 1:
                flags = [a + b for a, b in zip(flags[0::2], flags[1::2])]
            return acc + flags[0]
        acc = lax.fori_loop(0, n_chunks // 2, body, jnp.zeros((SUBLANES, tq), I32))
        return jnp.sum(acc, axis=0, keepdims=True)

    def code_to_float(code):
        bits = jnp.where(code < 0, (-code) | INT_MIN, code)
        return pltpu.bitcast(bits, F32)

    def bit_body(i, carry):
        code, cge = carry
        cand = code + jnp.left_shift(jnp.int32(1), 31 - i)
        cand_f = code_to_float(cand)
        c = count(lambda kk, _: kk >= cand_f)
        c = c - jnp.where(cand_f == -jnp.inf, n_inadmissible, 0)
        ok = c >= topk
        return jnp.where(ok, cand, code), jnp.where(ok, c, cge)

    code0 = jnp.full((1, tq), INT_MIN, I32)
    cge0 = jnp.zeros((1, tq), I32) + n_chunks * lc
    n_inadmissible = n_chunks * lc - lim
    code, cge = lax.fori_loop(0, 32, bit_body, (code0, cge0))
    thr = jnp.where(code > INT_MIN, code_to_float(code), -jnp.inf)

    n_idx_bits = max(1, (n_chunks_total * lc - 1).bit_length())
    tie_scr[...] = jnp.broadcast_to(jnp.where(thr == -jnp.inf, lim - 1, 2 ** 30), (SUBLANES, tq))
    surplus = jnp.logical_and(cge > topk, code > INT_MIN)

    @pl.when(jnp.max(jnp.where(surplus, 1, 0)) > 0)
    def _():
        cgt = count(lambda kk, _: kk > thr)
        want = topk - cgt
        bound = jnp.zeros((1, tq), I32)
        for bit in range(n_idx_bits - 1, -1, -1):
            cand = bound + (1 << bit)
            c = count(lambda kk, cc: jnp.logical_and(kk == thr, key_index(cc) < cand))
            bound = jnp.where(c < want, cand, bound)
        tie_scr[...] = jnp.broadcast_to(bound, (SUBLANES, tq))

    tie_bound = tie_scr[0:1, :]

    def bias_body(c, carry):
        off = chunk_start(c)
        kk = key_scr[pl.ds(off, lc), :]
        tie = jnp.where(key_index(c) <= tie_bound, 0.0, MASKED)
        b = jnp.where(kk > thr, 0.0, jnp.where(kk == thr, tie, MASKED))
        bias_scr[pl.ds(off, lc), :] = b.astype(BF16)
        return carry

    lax.fori_loop(0, n_chunks, bias_body, 0)

    gw = A_GROUP * tq
    eye = jnp.where(lax.broadcasted_iota(I32, (tq, gw), 0) == (lax.broadcasted_iota(I32, (tq, gw), 1) & (tq - 1)),
                    1.0, 0.0).astype(BF16)
    q_aug = [jnp.concatenate(
        [eye, jnp.concatenate(
            [qT_ref[0, 0, (g * A_GROUP + h) * A_HEAD_DIM:(g * A_GROUP + h + 1) * A_HEAD_DIM, :]
             for h in range(A_GROUP)], axis=1)], axis=0) for g in range(A_KV_HEADS)]
    acc_scr[...] = jnp.zeros(acc_scr.shape, F32)

    def logits(c, g):
        off = chunk_start(c)
        k_aug = jnp.concatenate([bias_scr[pl.ds(off, lc), :], kg_ref[0, g, pl.ds(off, lc), :]], axis=1)
        return _dot(k_aug, q_aug[g])

    def softmax_pv(s, c, g, m):
        mn = jnp.maximum(m, jnp.max(s, axis=0, keepdims=True))
        alpha = jnp.exp2(m - mn)
        p = jnp.exp2(s - mn).astype(BF16)
        vt = vT_ref[0, g, :, pl.ds(chunk_start(c), lc)]
        acc_scr[g] = alpha * acc_scr[g] + _dot(vt, p)
        return mn

    for g in range(A_KV_HEADS):
        sa_scr[g] = logits(0, g)

    def att_body(i, ms):
        c0 = 2 * i
        ms = list(ms)
        for g in range(A_KV_HEADS):
            sb_scr[g] = logits(c0 + 1, g)
            ms[g] = softmax_pv(sa_scr[g], c0, g, ms[g])
        c2 = jnp.minimum(c0 + 2, n_chunks - 1)
        for g in range(A_KV_HEADS):
            sa_scr[g] = logits(c2, g)
            ms[g] = softmax_pv(sb_scr[g], c0 + 1, g, ms[g])
        return tuple(ms)

    m0 = tuple(jnp.full((1, gw), M_INIT, F32) for _ in range(A_KV_HEADS))
    lax.fori_loop(0, n_chunks // 2, att_body, m0)
    for g in range(A_KV_HEADS):
        acc = acc_scr[g]
        o = acc[0:A_HEAD_DIM, :] / acc[A_HEAD_DIM:A_HEAD_DIM + 1, :]
        for h in range(A_GROUP):
            r0 = (g * A_GROUP + h) * A_HEAD_DIM
            oT_scr[r0:r0 + A_HEAD_DIM, :] = o[:, h * tq:(h + 1) * tq]

    o = jnp.transpose(oT_scr[...]).astype(BF16)
    res = x_ref[0] + _dot(o, wo_ref[...])
    if rows_out == tq:
        out_ref[0] = res
    else:
        res_scr[...] = res
        r0 = pl.multiple_of(pl.program_id(0) * rows_out, rows_out)
        out_ref[0] = res_scr[pl.ds(r0, rows_out), :]


def _dsa_attend(x, qT, qiT, wT, lim, kg, vT, kib, wo, *, topk, causal, shared_queries, rows_out):
    Bk, _, L, _ = kg.shape
    D = x.shape[-1]
    tq = qT.shape[-1]
    nq = qT.shape[1]
    assert L % (2 * DSA_KEY_CHUNK) == 0
    if shared_queries:
        grid = (Bk, 1)
        qmap = lambda b, j: (0, 0, 0, 0)
        xmap = lambda b, j: (0, 0, 0)
        omap = lambda b, j: (0, b, 0)
        lmap = lambda b, j: (0, 0, 0)
    else:
        grid = (Bk, nq)
        qmap = lambda b, j: (b, j, 0, 0)
        xmap = lambda b, j: (b, j, 0)
        omap = xmap
        lmap = lambda b, j: (j, 0, 0)
    in_specs = [
        pl.BlockSpec((1, 1, tq), lmap),
        pl.BlockSpec((1, tq, D), xmap),
        pl.BlockSpec((1, 1, _NQ, tq), qmap),
        pl.BlockSpec((1, 1, _NQI, tq), qmap),
        pl.BlockSpec((1, 1, IDX_HEADS, tq), qmap),
        pl.BlockSpec((1, A_KV_HEADS, L, A_HEAD_DIM), lambda b, j: (b, 0, 0, 0)),
        pl.BlockSpec((1, A_KV_HEADS, V_AUG_ROWS, L), lambda b, j: (b, 0, 0, 0)),
        pl.BlockSpec((1, L, IDX_DIM), lambda b, j: (b, 0, 0)),
        pl.BlockSpec((_NQ, D), lambda b, j: (0, 0)),
    ]
    kern = functools.partial(_dsa_attn_kernel, topk=topk, causal=causal,
                             n_chunks_total=L // DSA_KEY_CHUNK, rows_out=rows_out)
    return pl.pallas_call(
        kern, grid=grid, in_specs=in_specs, name="dsa_attend",
        out_specs=pl.BlockSpec((1, rows_out, D), omap),
        out_shape=jax.ShapeDtypeStruct(x.shape, F32),
        scratch_shapes=[pltpu.VMEM((L, tq), F32), pltpu.VMEM((SUBLANES, tq), I32),
                        pltpu.VMEM((_NQ, tq), F32), pltpu.VMEM((tq, D), F32),
                        pltpu.VMEM((A_KV_HEADS, V_AUG_ROWS, A_GROUP * tq), F32),
                        pltpu.VMEM((L, tq), BF16),
                        pltpu.VMEM((A_KV_HEADS, DSA_KEY_CHUNK, A_GROUP * tq), F32),
                        pltpu.VMEM((A_KV_HEADS, DSA_KEY_CHUNK, A_GROUP * tq), F32),
                        pltpu.VMEM((DSA_KEY_CHUNK, IDX_HEADS * tq), F32),
                        pltpu.VMEM((DSA_KEY_CHUNK, IDX_HEADS * tq), F32)],
        compiler_params=_params("arbitrary", "arbitrary"),
    )(lim, x, qT, qiT, wT, kg, vT, kib, wo)


MLP_FF_CHUNK = 1024


def _mlp_kernel(x_ref, g_ref, wu_ref, wd_ref, gf_ref, out_ref, *, final_norm):
    x = x_ref[...]
    xn = _rms(x, g_ref[...]).astype(BF16)
    acc = x
    for f in range(0, wu_ref.shape[1], MLP_FF_CHUNK):
        h = jnp.maximum(_dot(xn, wu_ref[:, f:f + MLP_FF_CHUNK]), 0.0)
        acc = acc + _dot((h * h).astype(BF16), wd_ref[f:f + MLP_FF_CHUNK, :])
    if final_norm:
        acc = _rms(acc, gf_ref[...])
    out_ref[...] = acc


def _mlp(x, gain, w_up, w_down, gain_final, *, tm, final_norm):
    M, D = x.shape
    FF = w_up.shape[1]
    const = lambda i: (0, 0)
    return pl.pallas_call(
        functools.partial(_mlp_kernel, final_norm=final_norm), name="mlp",
        grid=(M // tm,),
        in_specs=[
            pl.BlockSpec((tm, D), lambda i: (i, 0)),
            pl.BlockSpec((1, D), const),
            pl.BlockSpec((D, FF), const, pipeline_mode=pl.Buffered(1)),
            pl.BlockSpec((FF, D), const, pipeline_mode=pl.Buffered(1)),
            pl.BlockSpec((1, D), const),
        ],
        out_specs=pl.BlockSpec((tm, D), lambda i: (i, 0)),
        out_shape=jax.ShapeDtypeStruct((M, D), F32),
        compiler_params=_params("arbitrary"),
    )(x, gain.reshape(1, D), w_up.astype(BF16), w_down.astype(BF16), gain_final.reshape(1, D))


_GQK = GLA_HEADS * GLA_DK
_GV = GLA_HEADS * GLA_DV


def _gla_proj_kernel(x_ref, g_ref, w_ref, wa_ref, ba_ref, q_ref, k_ref, v_ref, gate_ref, la_ref):
    xn = _rms(x_ref[...], g_ref[...]).astype(BF16)
    y = _dot(xn, w_ref[...])
    D = gate_ref.shape[-1]
    q_ref[...] = y[:, 0:_GQK]
    k_ref[...] = y[:, _GQK:2 * _GQK]
    v_ref[...] = y[:, 2 * _GQK:2 * _GQK + _GV].astype(BF16)
    o = 2 * _GQK + _GV
    gate_ref[...] = y[:, o:o + D]
    a = y[:, o + D:o + D + GLA_RANK].astype(BF16)
    z = _dot(a, wa_ref[...]) + ba_ref[...]
    la_ref[...] = (jnp.minimum(z, 0.0) - jnp.log1p(jnp.exp(-jnp.abs(z)))) * (1.0 / GLA_TAU)


def _gla_project(x, gain, w_in, w_a2, b_a, *, tm):
    M, D = x.shape
    N = w_in.shape[1]
    const = lambda i: (0, 0)
    row = lambda i: (i, 0)
    return pl.pallas_call(
        _gla_proj_kernel, grid=(M // tm,), name="gla_project",
        in_specs=[
            pl.BlockSpec((tm, D), row),
            pl.BlockSpec((1, D), const),
            pl.BlockSpec((D, N), const),
            pl.BlockSpec((GLA_RANK, _GQK), const),
            pl.BlockSpec((1, _GQK), const),
        ],
        out_specs=(
            pl.BlockSpec((tm, _GQK), row), pl.BlockSpec((tm, _GQK), row), pl.BlockSpec((tm, _GV), row),
            pl.BlockSpec((tm, D), row), pl.BlockSpec((tm, _GQK), row)),
        out_shape=(
            jax.ShapeDtypeStruct((M, _GQK), F32), jax.ShapeDtypeStruct((M, _GQK), F32),
            jax.ShapeDtypeStruct((M, _GV), BF16), jax.ShapeDtypeStruct((M, D), F32),
            jax.ShapeDtypeStruct((M, _GQK), F32)),
        compiler_params=_params("arbitrary"),
    )(x, gain.reshape(1, D), w_in.astype(BF16), w_a2.astype(BF16), b_a.reshape(1, _GQK))


def _split3(a):
    a0 = a.astype(BF16)
    r = a - a0.astype(F32)
    a1 = r.astype(BF16)
    a2 = (r - a1.astype(F32)).astype(BF16)
    return a0, a1, a2


def _gla_kernel(x_ref, q_ref, k_ref, v_ref, gate_ref, la_ref, s0_ref, ng_ref, wo_ref,
                out_ref, sout_ref, s_scr, a_scr, o_scr):
    c = pl.program_id(1)
    C = q_ref.shape[1]

    @pl.when(c == 0)
    def _():
        s_scr[...] = s0_ref[0]

    la = la_ref[0]
    row = lax.broadcasted_iota(I32, (C, C), 0)
    col = lax.broadcasted_iota(I32, (C, C), 1)
    tri = jnp.where(row >= col, 1.0, 0.0).astype(BF16)
    l0, l1, l2 = _split3(la)
    b = _dot(tri, l0) + _dot(tri, l1) + _dot(tri, l2)
    q = q_ref[0] * (GLA_DK ** -0.5)
    k = k_ref[0]
    trow = lax.broadcasted_iota(I32, (C, 1), 0)

    def block_ref(values, size, pick):
        v3 = values.reshape(C // size, size, values.shape[-1])
        return jnp.broadcast_to(v3[:, pick:pick + 1, :], v3.shape).reshape(values.shape)

    def accumulate(qh, kh, mask, first):
        for h in range(GLA_HEADS):
            sl = slice(h * GLA_DK, (h + 1) * GLA_DK)
            blk = _dot_nt(qh[:, sl], kh[:, sl])
            if mask is not None:
                blk = jnp.where(mask, blk, 0.0)
            if first:
                a_scr[h] = blk
            else:
                a_scr[h] = a_scr[h] + blk

    d = min(GLA_DIAG, C)
    before = block_ref(b - la, d, 0)
    qh = (q * jnp.exp(b - before)).astype(BF16)
    kh = (k * jnp.exp(before - b)).astype(BF16)
    sh = d.bit_length() - 1
    accumulate(qh, kh, jnp.logical_and((row >> sh) == (col >> sh), row >= col), True)
    half = d
    while half < C:
        upper = (trow & (2 * half - 1)) >= half
        split = block_ref(b, 2 * half, half - 1)
        qh = (q * jnp.exp(jnp.where(upper, b - split, -jnp.inf))).astype(BF16)
        kh = (k * jnp.exp(jnp.where(upper, -jnp.inf, split - b))).astype(BF16)
        sh = (2 * half).bit_length() - 1
        accumulate(qh, kh, (row >> sh) == (col >> sh) if 2 * half < C else None, False)
        half *= 2

    b_end = b[C - 1:C, :]
    q_in = (q * jnp.exp(b)).astype(BF16)
    k_out = k * jnp.exp(b_end - b)
    ng = ng_ref[...]
    for h in range(GLA_HEADS):
        sl = slice(h * GLA_DK, (h + 1) * GLA_DK)
        vh = v_ref[0, :, h * GLA_DV:(h + 1) * GLA_DV]
        s_old = s_scr[h]
        o = _dot(q_in[:, sl], s_old.astype(BF16)) + _dot(a_scr[h].astype(BF16), vh)
        dec = jnp.transpose(jnp.broadcast_to(jnp.exp(b_end[:, sl]), (GLA_DK, GLA_DK)))
        dec = jnp.concatenate([dec] * (GLA_DV // GLA_DK), axis=1)
        s_scr[h] = dec * s_old + _dot(jnp.transpose(k_out[:, sl]).astype(BF16), vh)
        o = _rms(o, ng)
        gt = gate_ref[0, :, h * GLA_DV:(h + 1) * GLA_DV]
        o_scr[:, h * GLA_DV:(h + 1) * GLA_DV] = (o * (gt / (1.0 + jnp.exp(-gt)))).astype(BF16)

    out_ref[0] = x_ref[0] + _dot(o_scr[...], wo_ref[...])

    @pl.when(c == pl.num_programs(1) - 1)
    def _():
        sout_ref[0] = s_scr[...]


def _gla(x, q, k, v, gate, la, s0, norm_g, w_o, *, chunk):
    B, T, D = x.shape
    n = T // chunk
    tok = lambda b, c: (b, c, 0)
    const = lambda b, c: (0, 0)
    st = lambda b, c: (b, 0, 0, 0)
    return pl.pallas_call(
        _gla_kernel, grid=(B, n), name="gla",
        in_specs=[
            pl.BlockSpec((1, chunk, D), tok),
            pl.BlockSpec((1, chunk, _GQK), tok),
            pl.BlockSpec((1, chunk, _GQK), tok),
            pl.BlockSpec((1, chunk, _GV), tok),
            pl.BlockSpec((1, chunk, D), tok),
            pl.BlockSpec((1, chunk, _GQK), tok),
            pl.BlockSpec((1, GLA_HEADS, GLA_DK, GLA_DV), st),
            pl.BlockSpec((1, GLA_DV), const),
            pl.BlockSpec((_GV, D), const),
        ],
        out_specs=(pl.BlockSpec((1, chunk, D), tok), pl.BlockSpec((1, GLA_HEADS, GLA_DK, GLA_DV), st)),
        out_shape=(jax.ShapeDtypeStruct((B, T, D), F32),
                   jax.ShapeDtypeStruct((B, GLA_HEADS, GLA_DK, GLA_DV), F32)),
        scratch_shapes=[pltpu.VMEM((GLA_HEADS, GLA_DK, GLA_DV), F32),
                        pltpu.VMEM((GLA_HEADS, chunk, chunk), F32),
                        pltpu.VMEM((chunk, _GV), BF16)],
        compiler_params=_params("arbitrary", "arbitrary"),
    )(x, q, k, v, gate, la, s0, norm_g.reshape(1, GLA_DV), w_o.astype(BF16))


def _pick_tile(n, candidates):
    for t in candidates:
        if n % t == 0:
            return t
    return n


def _round_up(n, m):
    return (n + m - 1) // m * m


def kernel(x_prompt, x_sample, cache_k, cache_v, cache_kidx, state_gla, norm_mix, norm_ffn, norm_final,
           a_w_in, a_w_o, b_w_in, b_w_a2, b_b_a, b_norm, b_w_o, ffn_w_up, ffn_w_down):
    B, S, D = x_prompt.shape
    Bs, Ts, _ = x_sample.shape
    past = cache_k.shape[2]
    Ms = Bs * Ts
    tq = DSA_Q_TILE
    assert S % tq == 0 and Ms == tq and Ts % SUBLANES == 0

    wo_a = a_w_o[0].astype(BF16)
    tm = _pick_tile(S, (512, 256, 128))
    qT, qiT, wT, vT, k_p, v_p, ki_p, kg, kib = _dsa_project(x_prompt, norm_mix[0], a_w_in[0], tm=tm, tq=tq)
    pos = jnp.arange(S, dtype=I32)
    lim_p = ((pos // STREAM_CHUNK + 1) * STREAM_CHUNK).reshape(S // tq, 1, tq)
    L_p = _round_up(S, 2 * DSA_KEY_CHUNK)
    if L_p != S:
        pad = L_p - S
        kg = jnp.pad(kg, ((0, 0), (0, 0), (0, pad), (0, 0)))
        vT = jnp.pad(vT, ((0, 0), (0, 0), (0, 0), (0, pad)))
        kib = jnp.pad(kib, ((0, 0), (0, pad), (0, 0)))
    hp = _dsa_attend(x_prompt, qT, qiT, wT, lim_p, kg, vT, kib, wo_a,
                     topk=min(TOPK_MAX, S // 4), causal=True, shared_queries=False, rows_out=tq)
    xs = x_sample.reshape(1, Ms, D)
    qT_s, qiT_s, wT_s, _, k_s, v_s, ki_s, _, _ = _dsa_project(xs, norm_mix[0], a_w_in[0], tm=Ms, tq=tq)
    k_s = k_s.reshape(Bs, Ts, _NKV)
    v_s = v_s.reshape(Bs, Ts, _NKV)
    ki_s = ki_s.reshape(Bs, Ts, IDX_DIM)
    L_real = past + Ts
    L_s = _round_up(L_real, 2 * DSA_KEY_CHUNK)
    padn = L_s - L_real
    k_all = jnp.concatenate([cache_k[0].reshape(Bs, past, _NKV), k_s, jnp.zeros((Bs, padn, _NKV), F32)], axis=1)
    v_all = jnp.concatenate([cache_v[0].reshape(Bs, past, _NKV), v_s, jnp.zeros((Bs, padn, _NKV), F32)], axis=1)
    ki_all = jnp.concatenate([cache_kidx[0], ki_s, jnp.zeros((Bs, padn, IDX_DIM), F32)], axis=1)
    kg_s = k_all.reshape(Bs, L_s, A_KV_HEADS, A_HEAD_DIM).transpose(0, 2, 1, 3).astype(BF16)
    vT_s = jnp.concatenate(
        [v_all.reshape(Bs, L_s, A_KV_HEADS, A_HEAD_DIM).transpose(0, 2, 3, 1),
         jnp.ones((Bs, A_KV_HEADS, V_AUG_ROWS - A_HEAD_DIM, L_s), F32)], axis=2).astype(BF16)
    kib_s = ki_all.astype(BF16)
    pos_s = past + jnp.arange(Ts, dtype=I32)
    lim_s = jnp.minimum((pos_s // STREAM_CHUNK + 1) * STREAM_CHUNK, L_real)
    lim_s = jnp.tile(lim_s, Bs).reshape(1, 1, Ms)
    hs = _dsa_attend(xs, qT_s, qiT_s, wT_s, lim_s, kg_s, vT_s, kib_s, wo_a,
                     topk=min(TOPK_MAX, L_real // 4), causal=False, shared_queries=True, rows_out=Ts)

    tmm = _pick_tile(B * S, (512, 256, 128))
    hp = _mlp(hp.reshape(B * S, D), norm_ffn[0], ffn_w_up[0], ffn_w_down[0], norm_final, tm=tmm, final_norm=False)
    hs = _mlp(hs.reshape(Ms, D), norm_ffn[0], ffn_w_up[0], ffn_w_down[0], norm_final, tm=Ms, final_norm=False)

    q, k, v, gate, la = _gla_project(hp, norm_mix[1], b_w_in[0], b_w_a2[0], b_b_a[0], tm=tmm)
    chunk = _pick_tile(S, (256, 128, 64))
    r3 = lambda a: a.reshape(B, S, a.shape[-1])
    s0 = jnp.zeros((B, GLA_HEADS, GLA_DK, GLA_DV), F32)
    hp, s_p = _gla(hp.reshape(B, S, D), r3(q), r3(k), r3(v), r3(gate), r3(la), s0, b_norm[0], b_w_o[0], chunk=chunk)
    q, k, v, gate, la = _gla_project(hs, norm_mix[1], b_w_in[0], b_w_a2[0], b_b_a[0], tm=Ms)
    r3 = lambda a: a.reshape(Bs, Ts, a.shape[-1])
    hs, s_s = _gla(hs.reshape(Bs, Ts, D), r3(q), r3(k), r3(v), r3(gate), r3(la), state_gla[0], b_norm[0], b_w_o[0],
                   chunk=Ts)

    y_p = _mlp(hp.reshape(B * S, D), norm_ffn[1], ffn_w_up[1], ffn_w_down[1], norm_final, tm=tmm, final_norm=True)
    y_s = _mlp(hs.reshape(Ms, D), norm_ffn[1], ffn_w_up[1], ffn_w_down[1], norm_final, tm=Ms, final_norm=True)

    return (y_p.reshape(B, S, D), y_s.reshape(Bs, Ts, D),
            k_p.reshape(1, B, S, A_KV_HEADS, A_HEAD_DIM), v_p.reshape(1, B, S, A_KV_HEADS, A_HEAD_DIM),
            ki_p.reshape(1, B, S, IDX_DIM), s_p[None],
            k_s.reshape(1, Bs, Ts, A_KV_HEADS, A_HEAD_DIM), v_s.reshape(1, Bs, Ts, A_KV_HEADS, A_HEAD_DIM),
            ki_s.reshape(1, Bs, Ts, IDX_DIM), s_s[None])
```

```python
import functools

import jax
import jax.numpy as jnp
from jax import lax
from jax.experimental import pallas as pl
from jax.experimental.pallas import tpu as pltpu

F32 = jnp.float32
BF16 = jnp.bfloat16
I32 = jnp.int32

EPS = 1e-6
STREAM_CHUNK = 64
TOPK_MAX = 256
A_HEADS = 16
A_KV_HEADS = 4
A_GROUP = A_HEADS // A_KV_HEADS
A_HEAD_DIM = 64
IDX_HEADS = 8
IDX_DIM = 64
GLA_HEADS = 4
GLA_DK = 128
GLA_DV = 256
GLA_RANK = 16
GLA_TAU = 16.0

LANES = 128
SUBLANES = 8
INT_MIN = -(2 ** 31)
NEG_BIG = -1e30

DSA_Q_TILE = LANES
DSA_KEY_CHUNK = 256
GLA_DIAG = 16
VMEM_LIMIT = 56 * 1024 * 1024


def _params(*sem):
    return pltpu.CompilerParams(dimension_semantics=sem, vmem_limit_bytes=VMEM_LIMIT)


def _rms(x, g):
    ms = jnp.mean(x * x, axis=-1, keepdims=True)
    return x * lax.rsqrt(ms + EPS) * g


def _dot(a, b):
    return jnp.dot(a, b, preferred_element_type=F32)


def _dot_nt(a, b):
    return lax.dot_general(a, b, (((1,), (1,)), ((), ())), preferred_element_type=F32)


_NQ = A_HEADS * A_HEAD_DIM
_NKV = A_KV_HEADS * A_HEAD_DIM
_NQI = IDX_HEADS * IDX_DIM
_T_ROWS = _NQ + _NQI + _NKV + 16
BF16_SUBLANES = 16
V_AUG_ROWS = A_HEAD_DIM + BF16_SUBLANES
LOG2E = 1.4426950408889634
MASKED = -1e30
M_INIT = -1e29


def _dsa_proj_kernel(x_ref, g_ref, wt_ref, wn_ref,
                     qT_ref, qiT_ref, wT_ref, vT_ref, k_ref, v_ref, ki_ref, kg_ref, kib_ref, *, tq):
    xn = _rms(x_ref[0], g_ref[...]).astype(BF16)
    tm = xn.shape[0]
    yT = _dot_nt(wt_ref[...], xn)
    y = _dot(xn, wn_ref[...])
    for j in range(tm // tq):
        sl = slice(j * tq, (j + 1) * tq)
        qT_ref[0, j] = yT[0:_NQ, sl].astype(BF16)
        qiT_ref[0, j] = yT[_NQ:_NQ + _NQI, sl].astype(BF16)
        wT_ref[0, j] = yT[_NQ + _NQI + _NKV:_NQ + _NQI + _NKV + IDX_HEADS, sl]
    for h in range(A_KV_HEADS):
        r0 = _NQ + _NQI + h * A_HEAD_DIM
        vT_ref[0, h, 0:A_HEAD_DIM, :] = yT[r0:r0 + A_HEAD_DIM, :].astype(BF16)
        vT_ref[0, h, A_HEAD_DIM:V_AUG_ROWS, :] = jnp.ones((V_AUG_ROWS - A_HEAD_DIM, tm), BF16)
    k = y[:, 0:_NKV]
    ki = y[:, 2 * _NKV:2 * _NKV + IDX_DIM]
    k_ref[0] = k
    v_ref[0] = y[:, _NKV:2 * _NKV]
    ki_ref[0] = ki
    for h in range(A_KV_HEADS):
        kg_ref[0, h] = k[:, h * A_HEAD_DIM:(h + 1) * A_HEAD_DIM].astype(BF16)
    kib_ref[0] = ki.astype(BF16)


def _dsa_project(x, gain, w_in, *, tm, tq):
    B, S, D = x.shape
    o = 0
    parts = []
    for n in (_NQ, _NKV, _NKV, _NQI, IDX_DIM, IDX_HEADS):
        parts.append(w_in[:, o:o + n])
        o += n
    w_q, w_k, w_v, w_qi, w_ki, w_wt = parts
    wt_all = jnp.concatenate(
        [w_q * (A_HEAD_DIM ** -0.5 * LOG2E), w_qi, w_v, w_wt, jnp.zeros((D, 16 - IDX_HEADS), F32)],
        axis=1).T.astype(BF16)
    wn_all = jnp.concatenate([w_k, w_v, w_ki], axis=1).astype(BF16)
    nq = S // tq
    nt = S // tm
    jq = tm // tq
    out_shape = (
        jax.ShapeDtypeStruct((B, nq, _NQ, tq), BF16),
        jax.ShapeDtypeStruct((B, nq, _NQI, tq), BF16),
        jax.ShapeDtypeStruct((B, nq, IDX_HEADS, tq), F32),
        jax.ShapeDtypeStruct((B, A_KV_HEADS, V_AUG_ROWS, S), BF16),
        jax.ShapeDtypeStruct((B, S, _NKV), F32),
        jax.ShapeDtypeStruct((B, S, _NKV), F32),
        jax.ShapeDtypeStruct((B, S, IDX_DIM), F32),
        jax.ShapeDtypeStruct((B, A_KV_HEADS, S, A_HEAD_DIM), BF16),
        jax.ShapeDtypeStruct((B, S, IDX_DIM), BF16),
    )
    out_specs = (
        pl.BlockSpec((1, jq, _NQ, tq), lambda b, i: (b, i, 0, 0)),
        pl.BlockSpec((1, jq, _NQI, tq), lambda b, i: (b, i, 0, 0)),
        pl.BlockSpec((1, jq, IDX_HEADS, tq), lambda b, i: (b, i, 0, 0)),
        pl.BlockSpec((1, A_KV_HEADS, V_AUG_ROWS, tm), lambda b, i: (b, 0, 0, i)),
        pl.BlockSpec((1, tm, _NKV), lambda b, i: (b, i, 0)),
        pl.BlockSpec((1, tm, _NKV), lambda b, i: (b, i, 0)),
        pl.BlockSpec((1, tm, IDX_DIM), lambda b, i: (b, i, 0)),
        pl.BlockSpec((1, A_KV_HEADS, tm, A_HEAD_DIM), lambda b, i: (b, 0, i, 0)),
        pl.BlockSpec((1, tm, IDX_DIM), lambda b, i: (b, i, 0)),
    )
    in_specs = [
        pl.BlockSpec((1, tm, D), lambda b, i: (b, i, 0)),
        pl.BlockSpec((1, D), lambda b, i: (0, 0)),
        pl.BlockSpec((_T_ROWS, D), lambda b, i: (0, 0)),
        pl.BlockSpec((D, 2 * _NKV + IDX_DIM), lambda b, i: (0, 0)),
    ]
    return pl.pallas_call(
        functools.partial(_dsa_proj_kernel, tq=tq), name="dsa_project",
        grid=(B, nt), in_specs=in_specs, out_specs=out_specs, out_shape=out_shape,
        compiler_params=_params("arbitrary", "arbitrary"),
    )(x, gain.reshape(1, D), wt_all, wn_all)


def _stage_keys_kernel(ck_ref, cv_ref, cki_ref, nk_ref, nv_ref, nki_ref, kg_ref, vT_ref, kib_ref, kv_scr):
    past = ck_ref.shape[1]
    new = nk_ref.shape[1]
    L = kv_scr.shape[0]

    def gather_rows(cache_ref, new_ref, dst_ref, cast):
        dst_ref[0:past, :] = cast(cache_ref[0])
        dst_ref[past:past + new, :] = cast(new_ref[0])
        if past + new < L:
            dst_ref[past + new:L, :] = jnp.zeros((L - past - new, dst_ref.shape[-1]), dst_ref.dtype)

    gather_rows(ck_ref, nk_ref, kv_scr, lambda a: a)
    k_all = kv_scr[...]
    for h in range(A_KV_HEADS):
        kg_ref[0, h] = k_all[:, h * A_HEAD_DIM:(h + 1) * A_HEAD_DIM].astype(BF16)
    gather_rows(cv_ref, nv_ref, kv_scr, lambda a: a)
    v_t = jnp.transpose(kv_scr[...])
    for h in range(A_KV_HEADS):
        vT_ref[0, h, 0:A_HEAD_DIM, :] = v_t[h * A_HEAD_DIM:(h + 1) * A_HEAD_DIM, :].astype(BF16)
        vT_ref[0, h, A_HEAD_DIM:V_AUG_ROWS, :] = jnp.ones((V_AUG_ROWS - A_HEAD_DIM, L), BF16)
    gather_rows(cki_ref, nki_ref, kib_ref.at[0], lambda a: a.astype(BF16))


def _stage_sample_keys(cache_k, cache_v, cache_ki, new_k, new_v, new_ki, L):
    Bs, past, _ = cache_k.shape
    new = new_k.shape[1]
    row = lambda b: (b, 0, 0)
    return pl.pallas_call(
        _stage_keys_kernel, grid=(Bs,), name="stage_sample_keys",
        in_specs=[
            pl.BlockSpec((1, past, _NKV), row), pl.BlockSpec((1, past, _NKV), row),
            pl.BlockSpec((1, past, IDX_DIM), row),
            pl.BlockSpec((1, new, _NKV), row), pl.BlockSpec((1, new, _NKV), row),
            pl.BlockSpec((1, new, IDX_DIM), row),
        ],
        out_specs=(
            pl.BlockSpec((1, A_KV_HEADS, L, A_HEAD_DIM), lambda b: (b, 0, 0, 0)),
            pl.BlockSpec((1, A_KV_HEADS, V_AUG_ROWS, L), lambda b: (b, 0, 0, 0)),
            pl.BlockSpec((1, L, IDX_DIM), row),
        ),
        out_shape=(
            jax.ShapeDtypeStruct((Bs, A_KV_HEADS, L, A_HEAD_DIM), BF16),
            jax.ShapeDtypeStruct((Bs, A_KV_HEADS, V_AUG_ROWS, L), BF16),
            jax.ShapeDtypeStruct((Bs, L, IDX_DIM), BF16),
        ),
        scratch_shapes=[pltpu.VMEM((L, _NKV), F32)],
        compiler_params=_params("arbitrary"),
    )(cache_k, cache_v, cache_ki, new_k, new_v, new_ki)


def _dsa_attn_kernel(lim_ref, x_ref, qT_ref, qiT_ref, wT_ref, kg_ref, vT_ref, kib_ref, wo_ref,
                     out_ref, key_scr, tie_scr, oT_scr, res_scr, acc_scr, bias_scr, sa_scr, sb_scr,
                     ia_scr, ib_scr,
                     *, topk, causal, n_chunks_total, rows_out):
    tq = qT_ref.shape[-1]
    lc = DSA_KEY_CHUNK
    if causal:
        n_chunks = jnp.minimum(2 * (((pl.program_id(1) + 1) * tq + 2 * lc - 1) // (2 * lc)), n_chunks_total)
    else:
        n_chunks = n_chunks_total
    lim = lim_ref[0]

    def chunk_start(c):
        return pl.multiple_of(c * lc, lc)

    def key_index(c):
        return c * lc + lax.broadcasted_iota(I32, (lc, tq), 0)

    qi_all = jnp.concatenate(
        [qiT_ref[0, 0, h * IDX_DIM:(h + 1) * IDX_DIM, :] for h in range(IDX_HEADS)], axis=1)
    w_rows = wT_ref[0, 0] * ((IDX_DIM ** -0.5) * (IDX_HEADS ** -0.5))

    def idx_logits(c):
        return _dot(kib_ref[0, pl.ds(chunk_start(c), lc), :], qi_all)

    def store_keys(s_ref, c):
        off = chunk_start(c)
        score = jnp.maximum(s_ref[:, 0:tq], 0.0) * w_rows[0:1, :]
        for h in range(1, IDX_HEADS):
            score = score + jnp.maximum(s_ref[:, h * tq:(h + 1) * tq], 0.0) * w_rows[h:h + 1, :]
        key_scr[pl.ds(off, lc), :] = jnp.where(key_index(c) < lim, score, -jnp.inf)

    ia_scr[...] = idx_logits(0)

    def score_body(i, carry):
        c0 = 2 * i
        ib_scr[...] = idx_logits(c0 + 1)
        store_keys(ia_scr, c0)
        ia_scr[...] = idx_logits(jnp.minimum(c0 + 2, n_chunks - 1))
        store_keys(ib_scr, c0 + 1)
        return carry

    lax.fori_loop(0, n_chunks // 2, score_body, 0)

    def count(pred):
        def body(i, acc):
            flags = []
            for j in range(2):
                c = 2 * i + j
                kk = key_scr[pl.ds(chunk_start(c), lc), :]
                m = jnp.where(pred(kk, c), 1, 0).astype(I32)
                flags += [m[r * SUBLANES:(r + 1) * SUBLANES, :] for r in range(lc // SUBLANES)]
            while len(flags) > 1:
                flags = [a + b for a, b in zip(flags[0::2], flags[1::2])]
            return acc + flags[0]
        acc = lax.fori_loop(0, n_chunks // 2, body, jnp.zeros((SUBLANES, tq), I32))
        return jnp.sum(acc, axis=0, keepdims=True)

    def code_to_float(code):
        bits = jnp.where(code < 0, (-code) | INT_MIN, code)
        return pltpu.bitcast(bits, F32)

    def bit_body(i, carry):
        code, cge = carry
        cand = code + jnp.left_shift(jnp.int32(1), 31 - i)
        cand_f = code_to_float(cand)
        c = count(lambda kk, _: kk >= cand_f)
        c = c - jnp.where(cand_f == -jnp.inf, n_inadmissible, 0)
        ok = c >= topk
        return jnp.where(ok, cand, code), jnp.where(ok, c, cge)

    code0 = jnp.full((1, tq), INT_MIN, I32)
    cge0 = jnp.zeros((1, tq), I32) + n_chunks * lc
    n_inadmissible = n_chunks * lc - lim
    code, cge = lax.fori_loop(0, 32, bit_body, (code0, cge0))
    thr = jnp.where(code > INT_MIN, code_to_float(code), -jnp.inf)

    n_idx_bits = max(1, (n_chunks_total * lc - 1).bit_length())
    tie_scr[...] = jnp.broadcast_to(jnp.where(thr == -jnp.inf, lim - 1, 2 ** 30), (SUBLANES, tq))
    surplus = jnp.logical_and(cge > topk, code > INT_MIN)

    @pl.when(jnp.max(jnp.where(surplus, 1, 0)) > 0)
    def _():
        cgt = count(lambda kk, _: kk > thr)
        want = topk - cgt
        bound = jnp.zeros((1, tq), I32)
        for bit in range(n_idx_bits - 1, -1, -1):
            cand = bound + (1 << bit)
            c = count(lambda kk, cc: jnp.logical_and(kk == thr, key_index(cc) < cand))
            bound = jnp.where(c < want, cand, bound)
        tie_scr[...] = jnp.broadcast_to(bound, (SUBLANES, tq))

    tie_bound = tie_scr[0:1, :]

    def bias_body(c, carry):
        off = chunk_start(c)
        kk = key_scr[pl.ds(off, lc), :]
        tie = jnp.where(key_index(c) <= tie_bound, 0.0, MASKED)
        b = jnp.where(kk > thr, 0.0, jnp.where(kk == thr, tie, MASKED))
        bias_scr[pl.ds(off, lc), :] = b.astype(BF16)
        return carry

    lax.fori_loop(0, n_chunks, bias_body, 0)

    gw = A_GROUP * tq
    eye = jnp.where(lax.broadcasted_iota(I32, (tq, gw), 0) == (lax.broadcasted_iota(I32, (tq, gw), 1) & (tq - 1)),
                    1.0, 0.0).astype(BF16)
    q_aug = [jnp.concatenate(
        [eye, jnp.concatenate(
            [qT_ref[0, 0, (g * A_GROUP + h) * A_HEAD_DIM:(g * A_GROUP + h + 1) * A_HEAD_DIM, :]
             for h in range(A_GROUP)], axis=1)], axis=0) for g in range(A_KV_HEADS)]
    acc_scr[...] = jnp.zeros(acc_scr.shape, F32)

    def logits(c, g):
        off = chunk_start(c)
        k_aug = jnp.concatenate([bias_scr[pl.ds(off, lc), :], kg_ref[0, g, pl.ds(off, lc), :]], axis=1)
        return _dot(k_aug, q_aug[g])

    def softmax_pv(s, c, g, m):
        mn = jnp.maximum(m, jnp.max(s, axis=0, keepdims=True))
        alpha = jnp.exp2(m - mn)
        p = jnp.exp2(s - mn).astype(BF16)
        vt = vT_ref[0, g, :, pl.ds(chunk_start(c), lc)]
        acc_scr[g] = alpha * acc_scr[g] + _dot(vt, p)
        return mn

    for g in range(A_KV_HEADS):
        sa_scr[g] = logits(0, g)

    def att_body(i, ms):
        c0 = 2 * i
        ms = list(ms)
        for g in range(A_KV_HEADS):
            sb_scr[g] = logits(c0 + 1, g)
            ms[g] = softmax_pv(sa_scr[g], c0, g, ms[g])
        c2 = jnp.minimum(c0 + 2, n_chunks - 1)
        for g in range(A_KV_HEADS):
            sa_scr[g] = logits(c2, g)
            ms[g] = softmax_pv(sb_scr[g], c0 + 1, g, ms[g])
        return tuple(ms)

    m0 = tuple(jnp.full((1, gw), M_INIT, F32) for _ in range(A_KV_HEADS))
    lax.fori_loop(0, n_chunks // 2, att_body, m0)
    for g in range(A_KV_HEADS):
        acc = acc_scr[g]
        o = acc[0:A_HEAD_DIM, :] / acc[A_HEAD_DIM:A_HEAD_DIM + 1, :]
        for h in range(A_GROUP):
            r0 = (g * A_GROUP + h) * A_HEAD_DIM
            oT_scr[r0:r0 + A_HEAD_DIM, :] = o[:, h * tq:(h + 1) * tq]

    o = jnp.transpose(oT_scr[...]).astype(BF16)
    res = x_ref[0] + _dot(o, wo_ref[...])
    if rows_out == tq:
        out_ref[0] = res
    else:
        res_scr[...] = res
        r0 = pl.multiple_of(pl.program_id(0) * rows_out, rows_out)
        out_ref[0] = res_scr[pl.ds(r0, rows_out), :]


def _dsa_attend(x, qT, qiT, wT, lim, kg, vT, kib, wo, *, topk, causal, shared_queries, rows_out):
    Bk, _, L, _ = kg.shape
    D = x.shape[-1]
    tq = qT.shape[-1]
    nq = qT.shape[1]
    assert L % (2 * DSA_KEY_CHUNK) == 0
    if shared_queries:
        grid = (Bk, 1)
        qmap = lambda b, j: (0, 0, 0, 0)
        xmap = lambda b, j: (0, 0, 0)
        omap = lambda b, j: (0, b, 0)
        lmap = lambda b, j: (0, 0, 0)
    else:
        grid = (Bk, nq)
        qmap = lambda b, j: (b, j, 0, 0)
        xmap = lambda b, j: (b, j, 0)
        omap = xmap
        lmap = lambda b, j: (j, 0, 0)
    in_specs = [
        pl.BlockSpec((1, 1, tq), lmap),
        pl.BlockSpec((1, tq, D), xmap),
        pl.BlockSpec((1, 1, _NQ, tq), qmap),
        pl.BlockSpec((1, 1, _NQI, tq), qmap),
        pl.BlockSpec((1, 1, IDX_HEADS, tq), qmap),
        pl.BlockSpec((1, A_KV_HEADS, L, A_HEAD_DIM), lambda b, j: (b, 0, 0, 0)),
        pl.BlockSpec((1, A_KV_HEADS, V_AUG_ROWS, L), lambda b, j: (b, 0, 0, 0)),
        pl.BlockSpec((1, L, IDX_DIM), lambda b, j: (b, 0, 0)),
        pl.BlockSpec((None, _NQ, D), lambda b, j: (0, 0, 0)),
    ]
    kern = functools.partial(_dsa_attn_kernel, topk=topk, causal=causal,
                             n_chunks_total=L // DSA_KEY_CHUNK, rows_out=rows_out)
    return pl.pallas_call(
        kern, grid=grid, in_specs=in_specs, name="dsa_attend",
        out_specs=pl.BlockSpec((1, rows_out, D), omap),
        out_shape=jax.ShapeDtypeStruct(x.shape, F32),
        scratch_shapes=[pltpu.VMEM((L, tq), F32), pltpu.VMEM((SUBLANES, tq), I32),
                        pltpu.VMEM((_NQ, tq), F32), pltpu.VMEM((tq, D), F32),
                        pltpu.VMEM((A_KV_HEADS, V_AUG_ROWS, A_GROUP * tq), F32),
                        pltpu.VMEM((L, tq), BF16),
                        pltpu.VMEM((A_KV_HEADS, DSA_KEY_CHUNK, A_GROUP * tq), F32),
                        pltpu.VMEM((A_KV_HEADS, DSA_KEY_CHUNK, A_GROUP * tq), F32),
                        pltpu.VMEM((DSA_KEY_CHUNK, IDX_HEADS * tq), F32),
                        pltpu.VMEM((DSA_KEY_CHUNK, IDX_HEADS * tq), F32)],
        compiler_params=_params("arbitrary", "arbitrary"),
    )(lim, x, qT, qiT, wT, kg, vT, kib, wo)


CAST_STEPS = 8


def _cast_kernel(*refs):
    n = len(refs) // 2
    for src, dst in zip(refs[:n], refs[n:]):
        dst[...] = src[...].astype(BF16)


def _cast_bf16(*mats):
    specs = [pl.BlockSpec((m.shape[0], m.shape[1] // CAST_STEPS, m.shape[2]), lambda i: (0, i, 0)) for m in mats]
    return pl.pallas_call(
        _cast_kernel, grid=(CAST_STEPS,), name="cast_bf16",
        in_specs=specs, out_specs=tuple(specs),
        out_shape=tuple(jax.ShapeDtypeStruct(m.shape, BF16) for m in mats),
        compiler_params=_params("arbitrary"),
    )(*mats)


MLP_FF_CHUNK = 1024


def _mlp_kernel(x_ref, g_ref, wu_ref, wd_ref, gf_ref, out_ref, *, final_norm):
    x = x_ref[...]
    xn = _rms(x, g_ref[...]).astype(BF16)
    acc = x
    for f in range(0, wu_ref.shape[1], MLP_FF_CHUNK):
        h = jnp.maximum(_dot(xn, wu_ref[:, f:f + MLP_FF_CHUNK]), 0.0)
        acc = acc + _dot((h * h).astype(BF16), wd_ref[f:f + MLP_FF_CHUNK, :])
    if final_norm:
        acc = _rms(acc, gf_ref[...])
    out_ref[...] = acc


def _mlp(x, gain, w_up, w_down, layer, gain_final, *, tm, final_norm):
    M, D = x.shape
    FF = w_up.shape[2]
    const = lambda i: (0, 0)
    pick = lambda i: (layer, 0, 0)
    return pl.pallas_call(
        functools.partial(_mlp_kernel, final_norm=final_norm), name="mlp",
        grid=(M // tm,),
        in_specs=[
            pl.BlockSpec((tm, D), lambda i: (i, 0)),
            pl.BlockSpec((1, D), const),
            pl.BlockSpec((None, D, FF), pick, pipeline_mode=pl.Buffered(1)),
            pl.BlockSpec((None, FF, D), pick, pipeline_mode=pl.Buffered(1)),
            pl.BlockSpec((1, D), const),
        ],
        out_specs=pl.BlockSpec((tm, D), lambda i: (i, 0)),
        out_shape=jax.ShapeDtypeStruct((M, D), F32),
        compiler_params=_params("arbitrary"),
    )(x, gain.reshape(1, D), w_up, w_down, gain_final.reshape(1, D))


_GQK = GLA_HEADS * GLA_DK
_GV = GLA_HEADS * GLA_DV


def _gla_proj_kernel(x_ref, g_ref, w_ref, wa_ref, ba_ref, q_ref, k_ref, v_ref, gate_ref, la_ref):
    xn = _rms(x_ref[...], g_ref[...]).astype(BF16)
    y = _dot(xn, w_ref[...])
    D = gate_ref.shape[-1]
    q_ref[...] = y[:, 0:_GQK]
    k_ref[...] = y[:, _GQK:2 * _GQK]
    v_ref[...] = y[:, 2 * _GQK:2 * _GQK + _GV].astype(BF16)
    o = 2 * _GQK + _GV
    gate_ref[...] = y[:, o:o + D]
    a = y[:, o + D:o + D + GLA_RANK].astype(BF16)
    z = _dot(a, wa_ref[...]) + ba_ref[...]
    la_ref[...] = (jnp.minimum(z, 0.0) - jnp.log1p(jnp.exp(-jnp.abs(z)))) * (1.0 / GLA_TAU)


def _gla_project(x, gain, w_in, w_a2, b_a, *, tm):
    M, D = x.shape
    N = w_in.shape[2]
    const = lambda i: (0, 0)
    row = lambda i: (i, 0)
    return pl.pallas_call(
        _gla_proj_kernel, grid=(M // tm,), name="gla_project",
        in_specs=[
            pl.BlockSpec((tm, D), row),
            pl.BlockSpec((1, D), const),
            pl.BlockSpec((None, D, N), lambda i: (0, 0, 0)),
            pl.BlockSpec((GLA_RANK, _GQK), const),
            pl.BlockSpec((1, _GQK), const),
        ],
        out_specs=(
            pl.BlockSpec((tm, _GQK), row), pl.BlockSpec((tm, _GQK), row), pl.BlockSpec((tm, _GV), row),
            pl.BlockSpec((tm, D), row), pl.BlockSpec((tm, _GQK), row)),
        out_shape=(
            jax.ShapeDtypeStruct((M, _GQK), F32), jax.ShapeDtypeStruct((M, _GQK), F32),
            jax.ShapeDtypeStruct((M, _GV), BF16), jax.ShapeDtypeStruct((M, D), F32),
            jax.ShapeDtypeStruct((M, _GQK), F32)),
        compiler_params=_params("arbitrary"),
    )(x, gain.reshape(1, D), w_in, w_a2.astype(BF16), b_a.reshape(1, _GQK))


def _split3(a):
    a0 = a.astype(BF16)
    r = a - a0.astype(F32)
    a1 = r.astype(BF16)
    a2 = (r - a1.astype(F32)).astype(BF16)
    return a0, a1, a2


def _gla_kernel(x_ref, q_ref, k_ref, v_ref, gate_ref, la_ref, s0_ref, ng_ref, wo_ref,
                out_ref, sout_ref, s_scr, a_scr, o_scr):
    c = pl.program_id(1)
    C = q_ref.shape[1]

    @pl.when(c == 0)
    def _():
        s_scr[...] = s0_ref[0]

    la = la_ref[0]
    row = lax.broadcasted_iota(I32, (C, C), 0)
    col = lax.broadcasted_iota(I32, (C, C), 1)
    tri = jnp.where(row >= col, 1.0, 0.0).astype(BF16)
    l0, l1, l2 = _split3(la)
    b = _dot(tri, l0) + _dot(tri, l1) + _dot(tri, l2)
    q = q_ref[0] * (GLA_DK ** -0.5)
    k = k_ref[0]
    trow = lax.broadcasted_iota(I32, (C, 1), 0)

    def block_ref(values, size, pick):
        v3 = values.reshape(C // size, size, values.shape[-1])
        return jnp.broadcast_to(v3[:, pick:pick + 1, :], v3.shape).reshape(values.shape)

    def accumulate(qh, kh, mask, first):
        for h in range(GLA_HEADS):
            sl = slice(h * GLA_DK, (h + 1) * GLA_DK)
            blk = _dot_nt(qh[:, sl], kh[:, sl])
            if mask is not None:
                blk = jnp.where(mask, blk, 0.0)
            if first:
                a_scr[h] = blk
            else:
                a_scr[h] = a_scr[h] + blk

    d = min(GLA_DIAG, C)
    before = block_ref(b - la, d, 0)
    qh = (q * jnp.exp(b - before)).astype(BF16)
    kh = (k * jnp.exp(before - b)).astype(BF16)
    sh = d.bit_length() - 1
    accumulate(qh, kh, jnp.logical_and((row >> sh) == (col >> sh), row >= col), True)
    half = d
    while half < C:
        upper = (trow & (2 * half - 1)) >= half
        split = block_ref(b, 2 * half, half - 1)
        qh = (q * jnp.exp(jnp.where(upper, b - split, -jnp.inf))).astype(BF16)
        kh = (k * jnp.exp(jnp.where(upper, -jnp.inf, split - b))).astype(BF16)
        sh = (2 * half).bit_length() - 1
        accumulate(qh, kh, (row >> sh) == (col >> sh) if 2 * half < C else None, False)
        half *= 2

    b_end = b[C - 1:C, :]
    q_in = (q * jnp.exp(b)).astype(BF16)
    k_out = k * jnp.exp(b_end - b)
    ng = ng_ref[...]
    for h in range(GLA_HEADS):
        sl = slice(h * GLA_DK, (h + 1) * GLA_DK)
        vh = v_ref[0, :, h * GLA_DV:(h + 1) * GLA_DV]
        s_old = s_scr[h]
        o = _dot(q_in[:, sl], s_old.astype(BF16)) + _dot(a_scr[h].astype(BF16), vh)
        dec = jnp.transpose(jnp.broadcast_to(jnp.exp(b_end[:, sl]), (GLA_DK, GLA_DK)))
        dec = jnp.concatenate([dec] * (GLA_DV // GLA_DK), axis=1)
        s_scr[h] = dec * s_old + _dot(jnp.transpose(k_out[:, sl]).astype(BF16), vh)
        o = _rms(o, ng)
        gt = gate_ref[0, :, h * GLA_DV:(h + 1) * GLA_DV]
        o_scr[:, h * GLA_DV:(h + 1) * GLA_DV] = (o * (gt / (1.0 + jnp.exp(-gt)))).astype(BF16)

    out_ref[0] = x_ref[0] + _dot(o_scr[...], wo_ref[...])

    @pl.when(c == pl.num_programs(1) - 1)
    def _():
        sout_ref[0] = s_scr[...]


def _gla(x, q, k, v, gate, la, s0, norm_g, w_o, *, chunk):
    B, T, D = x.shape
    n = T // chunk
    tok = lambda b, c: (b, c, 0)
    const = lambda b, c: (0, 0)
    st = lambda b, c: (b, 0, 0, 0)
    return pl.pallas_call(
        _gla_kernel, grid=(B, n), name="gla",
        in_specs=[
            pl.BlockSpec((1, chunk, D), tok),
            pl.BlockSpec((1, chunk, _GQK), tok),
            pl.BlockSpec((1, chunk, _GQK), tok),
            pl.BlockSpec((1, chunk, _GV), tok),
            pl.BlockSpec((1, chunk, D), tok),
            pl.BlockSpec((1, chunk, _GQK), tok),
            pl.BlockSpec((1, GLA_HEADS, GLA_DK, GLA_DV), st),
            pl.BlockSpec((1, GLA_DV), const),
            pl.BlockSpec((None, _GV, D), lambda b, c: (0, 0, 0)),
        ],
        out_specs=(pl.BlockSpec((1, chunk, D), tok), pl.BlockSpec((1, GLA_HEADS, GLA_DK, GLA_DV), st)),
        out_shape=(jax.ShapeDtypeStruct((B, T, D), F32),
                   jax.ShapeDtypeStruct((B, GLA_HEADS, GLA_DK, GLA_DV), F32)),
        scratch_shapes=[pltpu.VMEM((GLA_HEADS, GLA_DK, GLA_DV), F32),
                        pltpu.VMEM((GLA_HEADS, chunk, chunk), F32),
                        pltpu.VMEM((chunk, _GV), BF16)],
        compiler_params=_params("arbitrary", "arbitrary"),
    )(x, q, k, v, gate, la, s0, norm_g.reshape(1, GLA_DV), w_o)


def _pick_tile(n, candidates):
    for t in candidates:
        if n % t == 0:
            return t
    return n


def _round_up(n, m):
    return (n + m - 1) // m * m


def kernel(x_prompt, x_sample, cache_k, cache_v, cache_kidx, state_gla, norm_mix, norm_ffn, norm_final,
           a_w_in, a_w_o, b_w_in, b_w_a2, b_b_a, b_norm, b_w_o, ffn_w_up, ffn_w_down):
    B, S, D = x_prompt.shape
    Bs, Ts, _ = x_sample.shape
    past = cache_k.shape[2]
    Ms = Bs * Ts
    tq = DSA_Q_TILE
    assert S % tq == 0 and Ms == tq and Ts % SUBLANES == 0

    wu_bf, wd_bf, bwin_bf, bwo_bf, wo_a = _cast_bf16(ffn_w_up, ffn_w_down, b_w_in, b_w_o, a_w_o)
    tm = _pick_tile(S, (512, 256, 128))
    qT, qiT, wT, vT, k_p, v_p, ki_p, kg, kib = _dsa_project(x_prompt, norm_mix[0], a_w_in[0], tm=tm, tq=tq)
    pos = jnp.arange(S, dtype=I32)
    lim_p = ((pos // STREAM_CHUNK + 1) * STREAM_CHUNK).reshape(S // tq, 1, tq)
    L_p = _round_up(S, 2 * DSA_KEY_CHUNK)
    if L_p != S:
        pad = L_p - S
        kg = jnp.pad(kg, ((0, 0), (0, 0), (0, pad), (0, 0)))
        vT = jnp.pad(vT, ((0, 0), (0, 0), (0, 0), (0, pad)))
        kib = jnp.pad(kib, ((0, 0), (0, pad), (0, 0)))
    hp = _dsa_attend(x_prompt, qT, qiT, wT, lim_p, kg, vT, kib, wo_a,
                     topk=min(TOPK_MAX, S // 4), causal=True, shared_queries=False, rows_out=tq)
    xs = x_sample.reshape(1, Ms, D)
    qT_s, qiT_s, wT_s, _, k_s, v_s, ki_s, _, _ = _dsa_project(xs, norm_mix[0], a_w_in[0], tm=Ms, tq=tq)
    k_s = k_s.reshape(Bs, Ts, _NKV)
    v_s = v_s.reshape(Bs, Ts, _NKV)
    ki_s = ki_s.reshape(Bs, Ts, IDX_DIM)
    L_real = past + Ts
    L_s = _round_up(L_real, 2 * DSA_KEY_CHUNK)
    kg_s, vT_s, kib_s = _stage_sample_keys(
        cache_k[0].reshape(Bs, past, _NKV), cache_v[0].reshape(Bs, past, _NKV), cache_kidx[0],
        k_s, v_s, ki_s, L_s)
    pos_s = past + jnp.arange(Ts, dtype=I32)
    lim_s = jnp.minimum((pos_s // STREAM_CHUNK + 1) * STREAM_CHUNK, L_real)
    lim_s = jnp.tile(lim_s, Bs).reshape(1, 1, Ms)
    hs = _dsa_attend(xs, qT_s, qiT_s, wT_s, lim_s, kg_s, vT_s, kib_s, wo_a,
                     topk=min(TOPK_MAX, L_real // 4), causal=False, shared_queries=True, rows_out=Ts)

    tmm = _pick_tile(B * S, (512, 256, 128))
    hp = _mlp(hp.reshape(B * S, D), norm_ffn[0], wu_bf, wd_bf, 0, norm_final, tm=tmm, final_norm=False)
    hs = _mlp(hs.reshape(Ms, D), norm_ffn[0], wu_bf, wd_bf, 0, norm_final, tm=Ms, final_norm=False)

    q, k, v, gate, la = _gla_project(hp, norm_mix[1], bwin_bf, b_w_a2[0], b_b_a[0], tm=tmm)
    chunk = _pick_tile(S, (256, 128, 64))
    r3 = lambda a: a.reshape(B, S, a.shape[-1])
    s0 = jnp.zeros((B, GLA_HEADS, GLA_DK, GLA_DV), F32)
    hp, s_p = _gla(hp.reshape(B, S, D), r3(q), r3(k), r3(v), r3(gate), r3(la), s0, b_norm[0], bwo_bf, chunk=chunk)
    q, k, v, gate, la = _gla_project(hs, norm_mix[1], bwin_bf, b_w_a2[0], b_b_a[0], tm=Ms)
    r3 = lambda a: a.reshape(Bs, Ts, a.shape[-1])
    hs, s_s = _gla(hs.reshape(Bs, Ts, D), r3(q), r3(k), r3(v), r3(gate), r3(la), state_gla[0], b_norm[0], bwo_bf,
                   chunk=Ts)

    y_p = _mlp(hp.reshape(B * S, D), norm_ffn[1], wu_bf, wd_bf, 1, norm_final, tm=tmm, final_norm=True)
    y_s = _mlp(hs.reshape(Ms, D), norm_ffn[1], wu_bf, wd_bf, 1, norm_final, tm=Ms, final_norm=True)

    return (y_p.reshape(B, S, D), y_s.reshape(Bs, Ts, D),
            k_p.reshape(1, B, S, A_KV_HEADS, A_HEAD_DIM), v_p.reshape(1, B, S, A_KV_HEADS, A_HEAD_DIM),
            ki_p.reshape(1, B, S, IDX_DIM), s_p[None],
            k_s.reshape(1, Bs, Ts, A_KV_HEADS, A_HEAD_DIM), v_s.reshape(1, Bs, Ts, A_KV_HEADS, A_HEAD_DIM),
            ki_s.reshape(1, Bs, Ts, IDX_DIM), s_s[None])
```

```python
import functools

import jax
import jax.numpy as jnp
from jax import lax
from jax.experimental import pallas as pl
from jax.experimental.pallas import tpu as pltpu

F32 = jnp.float32
BF16 = jnp.bfloat16
I32 = jnp.int32

EPS = 1e-6
STREAM_CHUNK = 64
TOPK_MAX = 256
A_HEADS = 16
A_KV_HEADS = 4
A_GROUP = A_HEADS // A_KV_HEADS
A_HEAD_DIM = 64
IDX_HEADS = 8
IDX_DIM = 64
GLA_HEADS = 4
GLA_DK = 128
GLA_DV = 256
GLA_RANK = 16
GLA_TAU = 16.0

LANES = 128
SUBLANES = 8
INT_MIN = -(2 ** 31)
NEG_BIG = -1e30

DSA_Q_TILE = LANES
DSA_KEY_CHUNK = 256
GLA_DIAG = 16
VMEM_LIMIT = 56 * 1024 * 1024


def _params(*sem):
    return pltpu.CompilerParams(dimension_semantics=sem, vmem_limit_bytes=VMEM_LIMIT)


def _rms(x, g):
    ms = jnp.mean(x * x, axis=-1, keepdims=True)
    return x * lax.rsqrt(ms + EPS) * g


def _dot(a, b):
    return jnp.dot(a, b, preferred_element_type=F32)


def _dot_nt(a, b):
    return lax.dot_general(a, b, (((1,), (1,)), ((), ())), preferred_element_type=F32)


_NQ = A_HEADS * A_HEAD_DIM
_NKV = A_KV_HEADS * A_HEAD_DIM
_NQI = IDX_HEADS * IDX_DIM
_T_ROWS = _NQ + _NQI + _NKV + 16
BF16_SUBLANES = 16
V_AUG_ROWS = A_HEAD_DIM + BF16_SUBLANES
LOG2E = 1.4426950408889634
MASKED = -1e30
M_INIT = -1e29


def _dsa_proj_kernel(x_ref, g_ref, wt_ref, wn_ref,
                     qT_ref, qiT_ref, wT_ref, vT_ref, k_ref, v_ref, ki_ref, kg_ref, kib_ref, *, tq):
    xn = _rms(x_ref[0], g_ref[...]).astype(BF16)
    tm = xn.shape[0]
    yT = _dot_nt(wt_ref[...], xn)
    y = _dot(xn, wn_ref[...])
    for j in range(tm // tq):
        sl = slice(j * tq, (j + 1) * tq)
        qT_ref[0, j] = yT[0:_NQ, sl].astype(BF16)
        qiT_ref[0, j] = yT[_NQ:_NQ + _NQI, sl].astype(BF16)
        wT_ref[0, j] = yT[_NQ + _NQI + _NKV:_NQ + _NQI + _NKV + IDX_HEADS, sl]
    for h in range(A_KV_HEADS):
        r0 = _NQ + _NQI + h * A_HEAD_DIM
        vT_ref[0, h, 0:A_HEAD_DIM, :] = yT[r0:r0 + A_HEAD_DIM, :].astype(BF16)
        vT_ref[0, h, A_HEAD_DIM:V_AUG_ROWS, :] = jnp.ones((V_AUG_ROWS - A_HEAD_DIM, tm), BF16)
    k = y[:, 0:_NKV]
    ki = y[:, 2 * _NKV:2 * _NKV + IDX_DIM]
    k_ref[0] = k
    v_ref[0] = y[:, _NKV:2 * _NKV]
    ki_ref[0] = ki
    for h in range(A_KV_HEADS):
        kg_ref[0, h] = k[:, h * A_HEAD_DIM:(h + 1) * A_HEAD_DIM].astype(BF16)
    kib_ref[0] = ki.astype(BF16)


def _dsa_project(x, gain, w_in, *, tm, tq):
    B, S, D = x.shape
    o = 0
    parts = []
    for n in (_NQ, _NKV, _NKV, _NQI, IDX_DIM, IDX_HEADS):
        parts.append(w_in[:, o:o + n])
        o += n
    w_q, w_k, w_v, w_qi, w_ki, w_wt = parts
    wt_all = jnp.concatenate(
        [w_q * (A_HEAD_DIM ** -0.5 * LOG2E), w_qi, w_v, w_wt, jnp.zeros((D, 16 - IDX_HEADS), F32)],
        axis=1).T.astype(BF16)
    wn_all = jnp.concatenate([w_k, w_v, w_ki], axis=1).astype(BF16)
    nq = S // tq
    nt = S // tm
    jq = tm // tq
    out_shape = (
        jax.ShapeDtypeStruct((B, nq, _NQ, tq), BF16),
        jax.ShapeDtypeStruct((B, nq, _NQI, tq), BF16),
        jax.ShapeDtypeStruct((B, nq, IDX_HEADS, tq), F32),
        jax.ShapeDtypeStruct((B, A_KV_HEADS, V_AUG_ROWS, S), BF16),
        jax.ShapeDtypeStruct((B, S, _NKV), F32),
        jax.ShapeDtypeStruct((B, S, _NKV), F32),
        jax.ShapeDtypeStruct((B, S, IDX_DIM), F32),
        jax.ShapeDtypeStruct((B, A_KV_HEADS, S, A_HEAD_DIM), BF16),
        jax.ShapeDtypeStruct((B, S, IDX_DIM), BF16),
    )
    out_specs = (
        pl.BlockSpec((1, jq, _NQ, tq), lambda b, i: (b, i, 0, 0)),
        pl.BlockSpec((1, jq, _NQI, tq), lambda b, i: (b, i, 0, 0)),
        pl.BlockSpec((1, jq, IDX_HEADS, tq), lambda b, i: (b, i, 0, 0)),
        pl.BlockSpec((1, A_KV_HEADS, V_AUG_ROWS, tm), lambda b, i: (b, 0, 0, i)),
        pl.BlockSpec((1, tm, _NKV), lambda b, i: (b, i, 0)),
        pl.BlockSpec((1, tm, _NKV), lambda b, i: (b, i, 0)),
        pl.BlockSpec((1, tm, IDX_DIM), lambda b, i: (b, i, 0)),
        pl.BlockSpec((1, A_KV_HEADS, tm, A_HEAD_DIM), lambda b, i: (b, 0, i, 0)),
        pl.BlockSpec((1, tm, IDX_DIM), lambda b, i: (b, i, 0)),
    )
    in_specs = [
        pl.BlockSpec((1, tm, D), lambda b, i: (b, i, 0)),
        pl.BlockSpec((1, D), lambda b, i: (0, 0)),
        pl.BlockSpec((_T_ROWS, D), lambda b, i: (0, 0)),
        pl.BlockSpec((D, 2 * _NKV + IDX_DIM), lambda b, i: (0, 0)),
    ]
    return pl.pallas_call(
        functools.partial(_dsa_proj_kernel, tq=tq), name="dsa_project",
        grid=(B, nt), in_specs=in_specs, out_specs=out_specs, out_shape=out_shape,
        compiler_params=_params("arbitrary", "arbitrary"),
    )(x, gain.reshape(1, D), wt_all, wn_all)


def _stage_keys_kernel(ck_ref, cv_ref, cki_ref, nk_ref, nv_ref, nki_ref, kg_ref, vT_ref, kib_ref, kv_scr):
    past = ck_ref.shape[1]
    new = nk_ref.shape[1]
    L = kv_scr.shape[0]

    def gather_rows(cache_ref, new_ref, dst_ref, cast):
        dst_ref[0:past, :] = cast(cache_ref[0])
        dst_ref[past:past + new, :] = cast(new_ref[0])
        if past + new < L:
            dst_ref[past + new:L, :] = jnp.zeros((L - past - new, dst_ref.shape[-1]), dst_ref.dtype)

    gather_rows(ck_ref, nk_ref, kv_scr, lambda a: a)
    k_all = kv_scr[...]
    for h in range(A_KV_HEADS):
        kg_ref[0, h] = k_all[:, h * A_HEAD_DIM:(h + 1) * A_HEAD_DIM].astype(BF16)
    gather_rows(cv_ref, nv_ref, kv_scr, lambda a: a)
    v_t = jnp.transpose(kv_scr[...])
    for h in range(A_KV_HEADS):
        vT_ref[0, h, 0:A_HEAD_DIM, :] = v_t[h * A_HEAD_DIM:(h + 1) * A_HEAD_DIM, :].astype(BF16)
        vT_ref[0, h, A_HEAD_DIM:V_AUG_ROWS, :] = jnp.ones((V_AUG_ROWS - A_HEAD_DIM, L), BF16)
    gather_rows(cki_ref, nki_ref, kib_ref.at[0], lambda a: a.astype(BF16))


def _stage_sample_keys(cache_k, cache_v, cache_ki, new_k, new_v, new_ki, L):
    Bs, past, _ = cache_k.shape
    new = new_k.shape[1]
    row = lambda b: (b, 0, 0)
    return pl.pallas_call(
        _stage_keys_kernel, grid=(Bs,), name="stage_sample_keys",
        in_specs=[
            pl.BlockSpec((1, past, _NKV), row), pl.BlockSpec((1, past, _NKV), row),
            pl.BlockSpec((1, past, IDX_DIM), row),
            pl.BlockSpec((1, new, _NKV), row), pl.BlockSpec((1, new, _NKV), row),
            pl.BlockSpec((1, new, IDX_DIM), row),
        ],
        out_specs=(
            pl.BlockSpec((1, A_KV_HEADS, L, A_HEAD_DIM), lambda b: (b, 0, 0, 0)),
            pl.BlockSpec((1, A_KV_HEADS, V_AUG_ROWS, L), lambda b: (b, 0, 0, 0)),
            pl.BlockSpec((1, L, IDX_DIM), row),
        ),
        out_shape=(
            jax.ShapeDtypeStruct((Bs, A_KV_HEADS, L, A_HEAD_DIM), BF16),
            jax.ShapeDtypeStruct((Bs, A_KV_HEADS, V_AUG_ROWS, L), BF16),
            jax.ShapeDtypeStruct((Bs, L, IDX_DIM), BF16),
        ),
        scratch_shapes=[pltpu.VMEM((L, _NKV), F32)],
        compiler_params=_params("arbitrary"),
    )(cache_k, cache_v, cache_ki, new_k, new_v, new_ki)


def _dsa_attn_kernel(lim_ref, x_ref, qT_ref, qiT_ref, wT_ref, kg_ref, vT_ref, kib_ref, wo_ref,
                     out_ref, key_scr, tie_scr, oT_scr, res_scr, acc_scr, bias_scr, sa_scr, sb_scr,
                     ia_scr, ib_scr,
                     *, topk, causal, n_chunks_total, rows_out):
    tq = qT_ref.shape[-1]
    lc = DSA_KEY_CHUNK
    if causal:
        n_chunks = jnp.minimum(((pl.program_id(1) + 1) * tq + lc - 1) // lc, n_chunks_total)
    else:
        n_chunks = jnp.int32(n_chunks_total)
    n_pairs = n_chunks // 2
    odd = (n_chunks & 1) == 1
    last = n_chunks - 1
    lim = lim_ref[0]

    def chunk_start(c):
        return pl.multiple_of(c * lc, lc)

    def key_index(c):
        return c * lc + lax.broadcasted_iota(I32, (lc, tq), 0)

    qi_all = jnp.concatenate(
        [qiT_ref[0, 0, h * IDX_DIM:(h + 1) * IDX_DIM, :] for h in range(IDX_HEADS)], axis=1)
    w_rows = wT_ref[0, 0] * ((IDX_DIM ** -0.5) * (IDX_HEADS ** -0.5))

    def idx_logits(c):
        return _dot(kib_ref[0, pl.ds(chunk_start(c), lc), :], qi_all)

    def store_keys(s_ref, c):
        off = chunk_start(c)
        score = jnp.maximum(s_ref[:, 0:tq], 0.0) * w_rows[0:1, :]
        for h in range(1, IDX_HEADS):
            score = score + jnp.maximum(s_ref[:, h * tq:(h + 1) * tq], 0.0) * w_rows[h:h + 1, :]
        key_scr[pl.ds(off, lc), :] = jnp.where(key_index(c) < lim, score, -jnp.inf)

    ia_scr[...] = idx_logits(0)

    def score_body(i, carry):
        c0 = 2 * i
        ib_scr[...] = idx_logits(c0 + 1)
        store_keys(ia_scr, c0)
        ia_scr[...] = idx_logits(jnp.minimum(c0 + 2, last))
        store_keys(ib_scr, c0 + 1)
        return carry

    lax.fori_loop(0, n_pairs, score_body, 0)

    @pl.when(odd)
    def _():
        store_keys(ia_scr, last)

    def count(pred):
        def chunk_flags(c):
            kk = key_scr[pl.ds(chunk_start(c), lc), :]
            m = jnp.where(pred(kk, c), 1, 0).astype(I32)
            return [m[r * SUBLANES:(r + 1) * SUBLANES, :] for r in range(lc // SUBLANES)]

        def tree_sum(flags):
            while len(flags) > 1:
                flags = [a + b for a, b in zip(flags[0::2], flags[1::2])]
            return flags[0]

        def body(i, acc):
            return acc + tree_sum(chunk_flags(2 * i) + chunk_flags(2 * i + 1))

        acc = lax.fori_loop(0, n_pairs, body, jnp.zeros((SUBLANES, tq), I32))
        acc = acc + lax.cond(odd, lambda: tree_sum(chunk_flags(last)), lambda: jnp.zeros((SUBLANES, tq), I32))
        return jnp.sum(acc, axis=0, keepdims=True)

    def code_to_float(code):
        bits = jnp.where(code < 0, (-code) | INT_MIN, code)
        return pltpu.bitcast(bits, F32)

    def bit_body(i, carry):
        code, cge = carry
        cand = code + jnp.left_shift(jnp.int32(1), 31 - i)
        cand_f = code_to_float(cand)
        c = count(lambda kk, _: kk >= cand_f)
        c = c - jnp.where(cand_f == -jnp.inf, n_inadmissible, 0)
        ok = c >= topk
        return jnp.where(ok, cand, code), jnp.where(ok, c, cge)

    code0 = jnp.full((1, tq), INT_MIN, I32)
    cge0 = jnp.zeros((1, tq), I32) + n_chunks * lc
    n_inadmissible = n_chunks * lc - lim
    code, cge = lax.fori_loop(0, 32, bit_body, (code0, cge0))
    thr = jnp.where(code > INT_MIN, code_to_float(code), -jnp.inf)

    n_idx_bits = max(1, (n_chunks_total * lc - 1).bit_length())
    tie_scr[...] = jnp.broadcast_to(jnp.where(thr == -jnp.inf, lim - 1, 2 ** 30), (SUBLANES, tq))
    surplus = jnp.logical_and(cge > topk, code > INT_MIN)

    @pl.when(jnp.max(jnp.where(surplus, 1, 0)) > 0)
    def _():
        cgt = count(lambda kk, _: kk > thr)
        want = topk - cgt
        bound = jnp.zeros((1, tq), I32)
        for bit in range(n_idx_bits - 1, -1, -1):
            cand = bound + (1 << bit)
            c = count(lambda kk, cc: jnp.logical_and(kk == thr, key_index(cc) < cand))
            bound = jnp.where(c < want, cand, bound)
        tie_scr[...] = jnp.broadcast_to(bound, (SUBLANES, tq))

    tie_bound = tie_scr[0:1, :]

    def bias_body(c, carry):
        off = chunk_start(c)
        kk = key_scr[pl.ds(off, lc), :]
        tie = jnp.where(key_index(c) <= tie_bound, 0.0, MASKED)
        b = jnp.where(kk > thr, 0.0, jnp.where(kk == thr, tie, MASKED))
        bias_scr[pl.ds(off, lc), :] = b.astype(BF16)
        return carry

    lax.fori_loop(0, n_chunks, bias_body, 0)

    gw = A_GROUP * tq
    eye = jnp.where(lax.broadcasted_iota(I32, (tq, gw), 0) == (lax.broadcasted_iota(I32, (tq, gw), 1) & (tq - 1)),
                    1.0, 0.0).astype(BF16)
    q_aug = [jnp.concatenate(
        [eye, jnp.concatenate(
            [qT_ref[0, 0, (g * A_GROUP + h) * A_HEAD_DIM:(g * A_GROUP + h + 1) * A_HEAD_DIM, :]
             for h in range(A_GROUP)], axis=1)], axis=0) for g in range(A_KV_HEADS)]
    acc_scr[...] = jnp.zeros(acc_scr.shape, F32)

    def logits(c, g):
        off = chunk_start(c)
        k_aug = jnp.concatenate([bias_scr[pl.ds(off, lc), :], kg_ref[0, g, pl.ds(off, lc), :]], axis=1)
        return _dot(k_aug, q_aug[g])

    def softmax_pv(s, c, g, m):
        mn = jnp.maximum(m, jnp.max(s, axis=0, keepdims=True))
        alpha = jnp.exp2(m - mn)
        p = jnp.exp2(s - mn).astype(BF16)
        vt = vT_ref[0, g, :, pl.ds(chunk_start(c), lc)]
        acc_scr[g] = alpha * acc_scr[g] + _dot(vt, p)
        return mn

    for g in range(A_KV_HEADS):
        sa_scr[g] = logits(0, g)

    def att_body(i, ms):
        c0 = 2 * i
        ms = list(ms)
        for g in range(A_KV_HEADS):
            sb_scr[g] = logits(c0 + 1, g)
            ms[g] = softmax_pv(sa_scr[g], c0, g, ms[g])
        c2 = jnp.minimum(c0 + 2, last)
        for g in range(A_KV_HEADS):
            sa_scr[g] = logits(c2, g)
            ms[g] = softmax_pv(sb_scr[g], c0 + 1, g, ms[g])
        return tuple(ms)

    m0 = tuple(jnp.full((1, gw), M_INIT, F32) for _ in range(A_KV_HEADS))
    ms = lax.fori_loop(0, n_pairs, att_body, m0)

    @pl.when(odd)
    def _():
        for g in range(A_KV_HEADS):
            softmax_pv(sa_scr[g], last, g, ms[g])

    for g in range(A_KV_HEADS):
        acc = acc_scr[g]
        o = acc[0:A_HEAD_DIM, :] / acc[A_HEAD_DIM:A_HEAD_DIM + 1, :]
        for h in range(A_GROUP):
            r0 = (g * A_GROUP + h) * A_HEAD_DIM
            oT_scr[r0:r0 + A_HEAD_DIM, :] = o[:, h * tq:(h + 1) * tq]

    o = jnp.transpose(oT_scr[...]).astype(BF16)
    res = x_ref[0] + _dot(o, wo_ref[...])
    if rows_out == tq:
        out_ref[0] = res
    else:
        res_scr[...] = res
        r0 = pl.multiple_of(pl.program_id(0) * rows_out, rows_out)
        out_ref[0] = res_scr[pl.ds(r0, rows_out), :]


def _dsa_attend(x, qT, qiT, wT, lim, kg, vT, kib, wo, *, topk, causal, shared_queries, rows_out):
    Bk, _, L, _ = kg.shape
    D = x.shape[-1]
    tq = qT.shape[-1]
    nq = qT.shape[1]
    assert L % DSA_KEY_CHUNK == 0
    if shared_queries:
        grid = (Bk, 1)
        qmap = lambda b, j: (0, 0, 0, 0)
        xmap = lambda b, j: (0, 0, 0)
        omap = lambda b, j: (0, b, 0)
        lmap = lambda b, j: (0, 0, 0)
    else:
        grid = (Bk, nq)
        qmap = lambda b, j: (b, j, 0, 0)
        xmap = lambda b, j: (b, j, 0)
        omap = xmap
        lmap = lambda b, j: (j, 0, 0)
    in_specs = [
        pl.BlockSpec((1, 1, tq), lmap),
        pl.BlockSpec((1, tq, D), xmap),
        pl.BlockSpec((1, 1, _NQ, tq), qmap),
        pl.BlockSpec((1, 1, _NQI, tq), qmap),
        pl.BlockSpec((1, 1, IDX_HEADS, tq), qmap),
        pl.BlockSpec((1, A_KV_HEADS, L, A_HEAD_DIM), lambda b, j: (b, 0, 0, 0)),
        pl.BlockSpec((1, A_KV_HEADS, V_AUG_ROWS, L), lambda b, j: (b, 0, 0, 0)),
        pl.BlockSpec((1, L, IDX_DIM), lambda b, j: (b, 0, 0)),
        pl.BlockSpec((None, _NQ, D), lambda b, j: (0, 0, 0)),
    ]
    kern = functools.partial(_dsa_attn_kernel, topk=topk, causal=causal,
                             n_chunks_total=L // DSA_KEY_CHUNK, rows_out=rows_out)
    return pl.pallas_call(
        kern, grid=grid, in_specs=in_specs, name="dsa_attend",
        out_specs=pl.BlockSpec((1, rows_out, D), omap),
        out_shape=jax.ShapeDtypeStruct(x.shape, F32),
        scratch_shapes=[pltpu.VMEM((L, tq), F32), pltpu.VMEM((SUBLANES, tq), I32),
                        pltpu.VMEM((_NQ, tq), F32), pltpu.VMEM((tq, D), F32),
                        pltpu.VMEM((A_KV_HEADS, V_AUG_ROWS, A_GROUP * tq), F32),
                        pltpu.VMEM((L, tq), BF16),
                        pltpu.VMEM((A_KV_HEADS, DSA_KEY_CHUNK, A_GROUP * tq), F32),
                        pltpu.VMEM((A_KV_HEADS, DSA_KEY_CHUNK, A_GROUP * tq), F32),
                        pltpu.VMEM((DSA_KEY_CHUNK, IDX_HEADS * tq), F32),
                        pltpu.VMEM((DSA_KEY_CHUNK, IDX_HEADS * tq), F32)],
        compiler_params=_params("arbitrary", "arbitrary"),
    )(lim, x, qT, qiT, wT, kg, vT, kib, wo)


CAST_STEPS = 8


def _cast_kernel(*refs):
    n = len(refs) // 2
    for src, dst in zip(refs[:n], refs[n:]):
        dst[...] = src[...].astype(BF16)


def _cast_bf16(*mats):
    specs = [pl.BlockSpec((m.shape[0], m.shape[1] // CAST_STEPS, m.shape[2]), lambda i: (0, i, 0)) for m in mats]
    return pl.pallas_call(
        _cast_kernel, grid=(CAST_STEPS,), name="cast_bf16",
        in_specs=specs, out_specs=tuple(specs),
        out_shape=tuple(jax.ShapeDtypeStruct(m.shape, BF16) for m in mats),
        compiler_params=_params("arbitrary"),
    )(*mats)


MLP_FF_CHUNK = 1024


def _mlp_kernel(x_ref, g_ref, wu_ref, wd_ref, gf_ref, out_ref, *, final_norm):
    x = x_ref[...]
    xn = _rms(x, g_ref[...]).astype(BF16)
    acc = x
    for f in range(0, wu_ref.shape[1], MLP_FF_CHUNK):
        h = jnp.maximum(_dot(xn, wu_ref[:, f:f + MLP_FF_CHUNK]), 0.0)
        acc = acc + _dot((h * h).astype(BF16), wd_ref[f:f + MLP_FF_CHUNK, :])
    if final_norm:
        acc = _rms(acc, gf_ref[...])
    out_ref[...] = acc


def _mlp(x, gain, w_up, w_down, layer, gain_final, *, tm, final_norm):
    M, D = x.shape
    FF = w_up.shape[2]
    const = lambda i: (0, 0)
    pick = lambda i: (layer, 0, 0)
    return pl.pallas_call(
        functools.partial(_mlp_kernel, final_norm=final_norm), name="mlp",
        grid=(M // tm,),
        in_specs=[
            pl.BlockSpec((tm, D), lambda i: (i, 0)),
            pl.BlockSpec((1, D), const),
            pl.BlockSpec((None, D, FF), pick, pipeline_mode=pl.Buffered(1)),
            pl.BlockSpec((None, FF, D), pick, pipeline_mode=pl.Buffered(1)),
            pl.BlockSpec((1, D), const),
        ],
        out_specs=pl.BlockSpec((tm, D), lambda i: (i, 0)),
        out_shape=jax.ShapeDtypeStruct((M, D), F32),
        compiler_params=_params("arbitrary"),
    )(x, gain.reshape(1, D), w_up, w_down, gain_final.reshape(1, D))


_GQK = GLA_HEADS * GLA_DK
_GV = GLA_HEADS * GLA_DV


def _gla_proj_kernel(x_ref, g_ref, w_ref, wa_ref, ba_ref, q_ref, k_ref, v_ref, gate_ref, la_ref):
    xn = _rms(x_ref[...], g_ref[...]).astype(BF16)
    y = _dot(xn, w_ref[...])
    D = gate_ref.shape[-1]
    q_ref[...] = y[:, 0:_GQK]
    k_ref[...] = y[:, _GQK:2 * _GQK]
    v_ref[...] = y[:, 2 * _GQK:2 * _GQK + _GV].astype(BF16)
    o = 2 * _GQK + _GV
    gate_ref[...] = y[:, o:o + D]
    a = y[:, o + D:o + D + GLA_RANK].astype(BF16)
    z = _dot(a, wa_ref[...]) + ba_ref[...]
    la_ref[...] = (jnp.minimum(z, 0.0) - jnp.log1p(jnp.exp(-jnp.abs(z)))) * (1.0 / GLA_TAU)


def _gla_project(x, gain, w_in, w_a2, b_a, *, tm):
    M, D = x.shape
    N = w_in.shape[2]
    const = lambda i: (0, 0)
    row = lambda i: (i, 0)
    return pl.pallas_call(
        _gla_proj_kernel, grid=(M // tm,), name="gla_project",
        in_specs=[
            pl.BlockSpec((tm, D), row),
            pl.BlockSpec((1, D), const),
            pl.BlockSpec((None, D, N), lambda i: (0, 0, 0)),
            pl.BlockSpec((GLA_RANK, _GQK), const),
            pl.BlockSpec((1, _GQK), const),
        ],
        out_specs=(
            pl.BlockSpec((tm, _GQK), row), pl.BlockSpec((tm, _GQK), row), pl.BlockSpec((tm, _GV), row),
            pl.BlockSpec((tm, D), row), pl.BlockSpec((tm, _GQK), row)),
        out_shape=(
            jax.ShapeDtypeStruct((M, _GQK), F32), jax.ShapeDtypeStruct((M, _GQK), F32),
            jax.ShapeDtypeStruct((M, _GV), BF16), jax.ShapeDtypeStruct((M, D), F32),
            jax.ShapeDtypeStruct((M, _GQK), F32)),
        compiler_params=_params("arbitrary"),
    )(x, gain.reshape(1, D), w_in, w_a2.astype(BF16), b_a.reshape(1, _GQK))


def _split3(a):
    a0 = a.astype(BF16)
    r = a - a0.astype(F32)
    a1 = r.astype(BF16)
    a2 = (r - a1.astype(F32)).astype(BF16)
    return a0, a1, a2


def _gla_kernel(x_ref, q_ref, k_ref, v_ref, gate_ref, la_ref, s0_ref, ng_ref, wo_ref,
                out_ref, sout_ref, s_scr, a_scr, o_scr):
    c = pl.program_id(1)
    C = q_ref.shape[1]

    @pl.when(c == 0)
    def _():
        s_scr[...] = s0_ref[0]

    la = la_ref[0]
    row = lax.broadcasted_iota(I32, (C, C), 0)
    col = lax.broadcasted_iota(I32, (C, C), 1)
    tri = jnp.where(row >= col, 1.0, 0.0).astype(BF16)
    l0, l1, l2 = _split3(la)
    b = _dot(tri, l0) + _dot(tri, l1) + _dot(tri, l2)
    q = q_ref[0] * (GLA_DK ** -0.5)
    k = k_ref[0]
    trow = lax.broadcasted_iota(I32, (C, 1), 0)

    def block_ref(values, size, pick):
        v3 = values.reshape(C // size, size, values.shape[-1])
        return jnp.broadcast_to(v3[:, pick:pick + 1, :], v3.shape).reshape(values.shape)

    def accumulate(qh, kh, mask, first):
        for h in range(GLA_HEADS):
            sl = slice(h * GLA_DK, (h + 1) * GLA_DK)
            blk = _dot_nt(qh[:, sl], kh[:, sl])
            if mask is not None:
                blk = jnp.where(mask, blk, 0.0)
            if first:
                a_scr[h] = blk
            else:
                a_scr[h] = a_scr[h] + blk

    d = min(GLA_DIAG, C)
    before = block_ref(b - la, d, 0)
    qh = (q * jnp.exp(b - before)).astype(BF16)
    kh = (k * jnp.exp(before - b)).astype(BF16)
    sh = d.bit_length() - 1
    accumulate(qh, kh, jnp.logical_and((row >> sh) == (col >> sh), row >= col), True)
    half = d
    while half < C:
        upper = (trow & (2 * half - 1)) >= half
        split = block_ref(b, 2 * half, half - 1)
        qh = (q * jnp.exp(jnp.where(upper, b - split, -jnp.inf))).astype(BF16)
        kh = (k * jnp.exp(jnp.where(upper, -jnp.inf, split - b))).astype(BF16)
        sh = (2 * half).bit_length() - 1
        accumulate(qh, kh, (row >> sh) == (col >> sh) if 2 * half < C else None, False)
        half *= 2

    b_end = b[C - 1:C, :]
    q_in = (q * jnp.exp(b)).astype(BF16)
    k_out = k * jnp.exp(b_end - b)
    ng = ng_ref[...]
    for h in range(GLA_HEADS):
        sl = slice(h * GLA_DK, (h + 1) * GLA_DK)
        vh = v_ref[0, :, h * GLA_DV:(h + 1) * GLA_DV]
        s_old = s_scr[h]
        o = _dot(q_in[:, sl], s_old.astype(BF16)) + _dot(a_scr[h].astype(BF16), vh)
        dec = jnp.transpose(jnp.broadcast_to(jnp.exp(b_end[:, sl]), (GLA_DK, GLA_DK)))
        dec = jnp.concatenate([dec] * (GLA_DV // GLA_DK), axis=1)
        s_scr[h] = dec * s_old + _dot(jnp.transpose(k_out[:, sl]).astype(BF16), vh)
        o = _rms(o, ng)
        gt = gate_ref[0, :, h * GLA_DV:(h + 1) * GLA_DV]
        o_scr[:, h * GLA_DV:(h + 1) * GLA_DV] = (o * (gt / (1.0 + jnp.exp(-gt)))).astype(BF16)

    out_ref[0] = x_ref[0] + _dot(o_scr[...], wo_ref[...])

    @pl.when(c == pl.num_programs(1) - 1)
    def _():
        sout_ref[0] = s_scr[...]


def _gla(x, q, k, v, gate, la, s0, norm_g, w_o, *, chunk):
    B, T, D = x.shape
    n = T // chunk
    tok = lambda b, c: (b, c, 0)
    const = lambda b, c: (0, 0)
    st = lambda b, c: (b, 0, 0, 0)
    return pl.pallas_call(
        _gla_kernel, grid=(B, n), name="gla",
        in_specs=[
            pl.BlockSpec((1, chunk, D), tok),
            pl.BlockSpec((1, chunk, _GQK), tok),
            pl.BlockSpec((1, chunk, _GQK), tok),
            pl.BlockSpec((1, chunk, _GV), tok),
            pl.BlockSpec((1, chunk, D), tok),
            pl.BlockSpec((1, chunk, _GQK), tok),
            pl.BlockSpec((1, GLA_HEADS, GLA_DK, GLA_DV), st),
            pl.BlockSpec((1, GLA_DV), const),
            pl.BlockSpec((None, _GV, D), lambda b, c: (0, 0, 0)),
        ],
        out_specs=(pl.BlockSpec((1, chunk, D), tok), pl.BlockSpec((1, GLA_HEADS, GLA_DK, GLA_DV), st)),
        out_shape=(jax.ShapeDtypeStruct((B, T, D), F32),
                   jax.ShapeDtypeStruct((B, GLA_HEADS, GLA_DK, GLA_DV), F32)),
        scratch_shapes=[pltpu.VMEM((GLA_HEADS, GLA_DK, GLA_DV), F32),
                        pltpu.VMEM((GLA_HEADS, chunk, chunk), F32),
                        pltpu.VMEM((chunk, _GV), BF16)],
        compiler_params=_params("arbitrary", "arbitrary"),
    )(x, q, k, v, gate, la, s0, norm_g.reshape(1, GLA_DV), w_o)


def _pick_tile(n, candidates):
    for t in candidates:
        if n % t == 0:
            return t
    return n


def _round_up(n, m):
    return (n + m - 1) // m * m


def kernel(x_prompt, x_sample, cache_k, cache_v, cache_kidx, state_gla, norm_mix, norm_ffn, norm_final,
           a_w_in, a_w_o, b_w_in, b_w_a2, b_b_a, b_norm, b_w_o, ffn_w_up, ffn_w_down):
    B, S, D = x_prompt.shape
    Bs, Ts, _ = x_sample.shape
    past = cache_k.shape[2]
    Ms = Bs * Ts
    tq = DSA_Q_TILE
    assert S % tq == 0 and Ms == tq and Ts % SUBLANES == 0

    wu_bf, wd_bf, bwin_bf, bwo_bf, wo_a = _cast_bf16(ffn_w_up, ffn_w_down, b_w_in, b_w_o, a_w_o)
    tm = _pick_tile(S, (512, 256, 128))
    qT, qiT, wT, vT, k_p, v_p, ki_p, kg, kib = _dsa_project(x_prompt, norm_mix[0], a_w_in[0], tm=tm, tq=tq)
    pos = jnp.arange(S, dtype=I32)
    lim_p = ((pos // STREAM_CHUNK + 1) * STREAM_CHUNK).reshape(S // tq, 1, tq)
    L_p = _round_up(S, DSA_KEY_CHUNK)
    if L_p != S:
        pad = L_p - S
        kg = jnp.pad(kg, ((0, 0), (0, 0), (0, pad), (0, 0)))
        vT = jnp.pad(vT, ((0, 0), (0, 0), (0, 0), (0, pad)))
        kib = jnp.pad(kib, ((0, 0), (0, pad), (0, 0)))
    hp = _dsa_attend(x_prompt, qT, qiT, wT, lim_p, kg, vT, kib, wo_a,
                     topk=min(TOPK_MAX, S // 4), causal=True, shared_queries=False, rows_out=tq)
    xs = x_sample.reshape(1, Ms, D)
    qT_s, qiT_s, wT_s, _, k_s, v_s, ki_s, _, _ = _dsa_project(xs, norm_mix[0], a_w_in[0], tm=Ms, tq=tq)
    k_s = k_s.reshape(Bs, Ts, _NKV)
    v_s = v_s.reshape(Bs, Ts, _NKV)
    ki_s = ki_s.reshape(Bs, Ts, IDX_DIM)
    L_real = past + Ts
    L_s = _round_up(L_real, DSA_KEY_CHUNK)
    kg_s, vT_s, kib_s = _stage_sample_keys(
        cache_k[0].reshape(Bs, past, _NKV), cache_v[0].reshape(Bs, past, _NKV), cache_kidx[0],
        k_s, v_s, ki_s, L_s)
    pos_s = past + jnp.arange(Ts, dtype=I32)
    lim_s = jnp.minimum((pos_s // STREAM_CHUNK + 1) * STREAM_CHUNK, L_real)
    lim_s = jnp.tile(lim_s, Bs).reshape(1, 1, Ms)
    hs = _dsa_attend(xs, qT_s, qiT_s, wT_s, lim_s, kg_s, vT_s, kib_s, wo_a,
                     topk=min(TOPK_MAX, L_real // 4), causal=False, shared_queries=True, rows_out=Ts)

    tmm = _pick_tile(B * S, (512, 256, 128))
    hp = _mlp(hp.reshape(B * S, D), norm_ffn[0], wu_bf, wd_bf, 0, norm_final, tm=tmm, final_norm=False)
    hs = _mlp(hs.reshape(Ms, D), norm_ffn[0], wu_bf, wd_bf, 0, norm_final, tm=Ms, final_norm=False)

    q, k, v, gate, la = _gla_project(hp, norm_mix[1], bwin_bf, b_w_a2[0], b_b_a[0], tm=tmm)
    chunk = _pick_tile(S, (256, 128, 64))
    r3 = lambda a: a.reshape(B, S, a.shape[-1])
    s0 = jnp.zeros((B, GLA_HEADS, GLA_DK, GLA_DV), F32)
    hp, s_p = _gla(hp.reshape(B, S, D), r3(q), r3(k), r3(v), r3(gate), r3(la), s0, b_norm[0], bwo_bf, chunk=chunk)
    q, k, v, gate, la = _gla_project(hs, norm_mix[1], bwin_bf, b_w_a2[0], b_b_a[0], tm=Ms)
    r3 = lambda a: a.reshape(Bs, Ts, a.shape[-1])
    hs, s_s = _gla(hs.reshape(Bs, Ts, D), r3(q), r3(k), r3(v), r3(gate), r3(la), state_gla[0], b_norm[0], bwo_bf,
                   chunk=Ts)

    y_p = _mlp(hp.reshape(B * S, D), norm_ffn[1], wu_bf, wd_bf, 1, norm_final, tm=tmm, final_norm=True)
    y_s = _mlp(hs.reshape(Ms, D), norm_ffn[1], wu_bf, wd_bf, 1, norm_final, tm=Ms, final_norm=True)

    return (y_p.reshape(B, S, D), y_s.reshape(Bs, Ts, D),
            k_p.reshape(1, B, S, A_KV_HEADS, A_HEAD_DIM), v_p.reshape(1, B, S, A_KV_HEADS, A_HEAD_DIM),
            ki_p.reshape(1, B, S, IDX_DIM), s_p[None],
            k_s.reshape(1, Bs, Ts, A_KV_HEADS, A_HEAD_DIM), v_s.reshape(1, Bs, Ts, A_KV_HEADS, A_HEAD_DIM),
            ki_s.reshape(1, Bs, Ts, IDX_DIM), s_s[None])
```

```python
import functools

import jax
import jax.numpy as jnp
from jax import lax
from jax.experimental import pallas as pl
from jax.experimental.pallas import tpu as pltpu

F32 = jnp.float32
BF16 = jnp.bfloat16
I32 = jnp.int32

EPS = 1e-6
STREAM_CHUNK = 64
TOPK_MAX = 256
A_HEADS = 16
A_KV_HEADS = 4
A_GROUP = A_HEADS // A_KV_HEADS
A_HEAD_DIM = 64
IDX_HEADS = 8
IDX_DIM = 64
GLA_HEADS = 4
GLA_DK = 128
GLA_DV = 256
GLA_RANK = 16
GLA_TAU = 16.0

LANES = 128
SUBLANES = 8
INT_MIN = -(2 ** 31)
NEG_BIG = -1e30

DSA_Q_TILE = LANES
DSA_KEY_CHUNK = 256
GLA_DIAG = 16
VMEM_LIMIT = 56 * 1024 * 1024


def _params(*sem):
    return pltpu.CompilerParams(dimension_semantics=sem, vmem_limit_bytes=VMEM_LIMIT)


def _rms(x, g):
    ms = jnp.mean(x * x, axis=-1, keepdims=True)
    return x * lax.rsqrt(ms + EPS) * g


def _dot(a, b):
    return jnp.dot(a, b, preferred_element_type=F32)


def _dot_nt(a, b):
    return lax.dot_general(a, b, (((1,), (1,)), ((), ())), preferred_element_type=F32)


_NQ = A_HEADS * A_HEAD_DIM
_NKV = A_KV_HEADS * A_HEAD_DIM
_NQI = IDX_HEADS * IDX_DIM
_T_ROWS = _NQ + _NQI + _NKV + 16
BF16_SUBLANES = 16
V_AUG_ROWS = A_HEAD_DIM + BF16_SUBLANES
LOG2E = 1.4426950408889634
MASKED = -1e30
M_INIT = -1e29


def _dsa_proj_kernel(x_ref, g_ref, wt_ref, wn_ref,
                     qT_ref, qiT_ref, wT_ref, vT_ref, k_ref, v_ref, ki_ref, kg_ref, kib_ref, *, tq):
    xn = _rms(x_ref[0], g_ref[...]).astype(BF16)
    tm = xn.shape[0]
    yT = _dot_nt(wt_ref[...], xn)
    y = _dot(xn, wn_ref[...])
    for j in range(tm // tq):
        sl = slice(j * tq, (j + 1) * tq)
        qT_ref[0, j] = yT[0:_NQ, sl].astype(BF16)
        qiT_ref[0, j] = yT[_NQ:_NQ + _NQI, sl].astype(BF16)
        wT_ref[0, j] = yT[_NQ + _NQI + _NKV:_NQ + _NQI + _NKV + IDX_HEADS, sl]
    for h in range(A_KV_HEADS):
        r0 = _NQ + _NQI + h * A_HEAD_DIM
        vT_ref[0, h, 0:A_HEAD_DIM, :] = yT[r0:r0 + A_HEAD_DIM, :].astype(BF16)
        vT_ref[0, h, A_HEAD_DIM:V_AUG_ROWS, :] = jnp.ones((V_AUG_ROWS - A_HEAD_DIM, tm), BF16)
    k = y[:, 0:_NKV]
    ki = y[:, 2 * _NKV:2 * _NKV + IDX_DIM]
    k_ref[0] = k
    v_ref[0] = y[:, _NKV:2 * _NKV]
    ki_ref[0] = ki
    for h in range(A_KV_HEADS):
        kg_ref[0, h] = k[:, h * A_HEAD_DIM:(h + 1) * A_HEAD_DIM].astype(BF16)
    kib_ref[0] = ki.astype(BF16)


def _dsa_project(x, gain, w_in, *, tm, tq):
    B, S, D = x.shape
    o = 0
    parts = []
    for n in (_NQ, _NKV, _NKV, _NQI, IDX_DIM, IDX_HEADS):
        parts.append(w_in[:, o:o + n])
        o += n
    w_q, w_k, w_v, w_qi, w_ki, w_wt = parts
    wt_all = jnp.concatenate(
        [w_q * (A_HEAD_DIM ** -0.5 * LOG2E), w_qi, w_v, w_wt, jnp.zeros((D, 16 - IDX_HEADS), F32)],
        axis=1).T.astype(BF16)
    wn_all = jnp.concatenate([w_k, w_v, w_ki], axis=1).astype(BF16)
    nq = S // tq
    nt = S // tm
    jq = tm // tq
    out_shape = (
        jax.ShapeDtypeStruct((B, nq, _NQ, tq), BF16),
        jax.ShapeDtypeStruct((B, nq, _NQI, tq), BF16),
        jax.ShapeDtypeStruct((B, nq, IDX_HEADS, tq), F32),
        jax.ShapeDtypeStruct((B, A_KV_HEADS, V_AUG_ROWS, S), BF16),
        jax.ShapeDtypeStruct((B, S, _NKV), F32),
        jax.ShapeDtypeStruct((B, S, _NKV), F32),
        jax.ShapeDtypeStruct((B, S, IDX_DIM), F32),
        jax.ShapeDtypeStruct((B, A_KV_HEADS, S, A_HEAD_DIM), BF16),
        jax.ShapeDtypeStruct((B, S, IDX_DIM), BF16),
    )
    out_specs = (
        pl.BlockSpec((1, jq, _NQ, tq), lambda b, i: (b, i, 0, 0)),
        pl.BlockSpec((1, jq, _NQI, tq), lambda b, i: (b, i, 0, 0)),
        pl.BlockSpec((1, jq, IDX_HEADS, tq), lambda b, i: (b, i, 0, 0)),
        pl.BlockSpec((1, A_KV_HEADS, V_AUG_ROWS, tm), lambda b, i: (b, 0, 0, i)),
        pl.BlockSpec((1, tm, _NKV), lambda b, i: (b, i, 0)),
        pl.BlockSpec((1, tm, _NKV), lambda b, i: (b, i, 0)),
        pl.BlockSpec((1, tm, IDX_DIM), lambda b, i: (b, i, 0)),
        pl.BlockSpec((1, A_KV_HEADS, tm, A_HEAD_DIM), lambda b, i: (b, 0, i, 0)),
        pl.BlockSpec((1, tm, IDX_DIM), lambda b, i: (b, i, 0)),
    )
    in_specs = [
        pl.BlockSpec((1, tm, D), lambda b, i: (b, i, 0)),
        pl.BlockSpec((1, D), lambda b, i: (0, 0)),
        pl.BlockSpec((_T_ROWS, D), lambda b, i: (0, 0)),
        pl.BlockSpec((D, 2 * _NKV + IDX_DIM), lambda b, i: (0, 0)),
    ]
    return pl.pallas_call(
        functools.partial(_dsa_proj_kernel, tq=tq), name="dsa_project",
        grid=(B, nt), in_specs=in_specs, out_specs=out_specs, out_shape=out_shape,
        compiler_params=_params("arbitrary", "arbitrary"),
    )(x, gain.reshape(1, D), wt_all, wn_all)


def _stage_keys_kernel(ck_ref, cv_ref, cki_ref, nk_ref, nv_ref, nki_ref, kg_ref, vT_ref, kib_ref, kv_scr):
    past = ck_ref.shape[1]
    new = nk_ref.shape[1]
    L = kv_scr.shape[0]

    def gather_rows(cache_ref, new_ref, dst_ref, cast):
        dst_ref[0:past, :] = cast(cache_ref[0])
        dst_ref[past:past + new, :] = cast(new_ref[0])
        if past + new < L:
            dst_ref[past + new:L, :] = jnp.zeros((L - past - new, dst_ref.shape[-1]), dst_ref.dtype)

    gather_rows(ck_ref, nk_ref, kv_scr, lambda a: a)
    k_all = kv_scr[...]
    for h in range(A_KV_HEADS):
        kg_ref[0, h] = k_all[:, h * A_HEAD_DIM:(h + 1) * A_HEAD_DIM].astype(BF16)
    gather_rows(cv_ref, nv_ref, kv_scr, lambda a: a)
    v_t = jnp.transpose(kv_scr[...])
    for h in range(A_KV_HEADS):
        vT_ref[0, h, 0:A_HEAD_DIM, :] = v_t[h * A_HEAD_DIM:(h + 1) * A_HEAD_DIM, :].astype(BF16)
        vT_ref[0, h, A_HEAD_DIM:V_AUG_ROWS, :] = jnp.ones((V_AUG_ROWS - A_HEAD_DIM, L), BF16)
    gather_rows(cki_ref, nki_ref, kib_ref.at[0], lambda a: a.astype(BF16))


def _stage_sample_keys(cache_k, cache_v, cache_ki, new_k, new_v, new_ki, L):
    Bs, past, _ = cache_k.shape
    new = new_k.shape[1]
    row = lambda b: (b, 0, 0)
    return pl.pallas_call(
        _stage_keys_kernel, grid=(Bs,), name="stage_sample_keys",
        in_specs=[
            pl.BlockSpec((1, past, _NKV), row), pl.BlockSpec((1, past, _NKV), row),
            pl.BlockSpec((1, past, IDX_DIM), row),
            pl.BlockSpec((1, new, _NKV), row), pl.BlockSpec((1, new, _NKV), row),
            pl.BlockSpec((1, new, IDX_DIM), row),
        ],
        out_specs=(
            pl.BlockSpec((1, A_KV_HEADS, L, A_HEAD_DIM), lambda b: (b, 0, 0, 0)),
            pl.BlockSpec((1, A_KV_HEADS, V_AUG_ROWS, L), lambda b: (b, 0, 0, 0)),
            pl.BlockSpec((1, L, IDX_DIM), row),
        ),
        out_shape=(
            jax.ShapeDtypeStruct((Bs, A_KV_HEADS, L, A_HEAD_DIM), BF16),
            jax.ShapeDtypeStruct((Bs, A_KV_HEADS, V_AUG_ROWS, L), BF16),
            jax.ShapeDtypeStruct((Bs, L, IDX_DIM), BF16),
        ),
        scratch_shapes=[pltpu.VMEM((L, _NKV), F32)],
        compiler_params=_params("arbitrary"),
    )(cache_k, cache_v, cache_ki, new_k, new_v, new_ki)


def _dsa_attn_kernel(lim_ref, x_ref, qT_ref, qiT_ref, wT_ref, kg_ref, vT_ref, kib_ref, wo_ref,
                     out_ref, key_scr, tie_scr, oT_scr, res_scr, acc_scr, bias_scr, sa_scr, sb_scr,
                     ia_scr, ib_scr,
                     *, topk, causal, n_chunks_total, rows_out):
    tq = qT_ref.shape[-1]
    lc = DSA_KEY_CHUNK
    if causal:
        n_chunks = jnp.minimum(((pl.program_id(1) + 1) * tq + lc - 1) // lc, n_chunks_total)
    else:
        n_chunks = jnp.int32(n_chunks_total)
    n_pairs = n_chunks // 2
    odd = (n_chunks & 1) == 1
    last = n_chunks - 1
    lim = lim_ref[0]

    def chunk_start(c):
        return pl.multiple_of(c * lc, lc)

    def key_index(c):
        return c * lc + lax.broadcasted_iota(I32, (lc, tq), 0)

    qi_all = jnp.concatenate(
        [qiT_ref[0, 0, h * IDX_DIM:(h + 1) * IDX_DIM, :] for h in range(IDX_HEADS)], axis=1)
    w_rows = wT_ref[0, 0] * ((IDX_DIM ** -0.5) * (IDX_HEADS ** -0.5))

    def idx_logits(c):
        return _dot(kib_ref[0, pl.ds(chunk_start(c), lc), :], qi_all)

    def store_keys(s_ref, c):
        off = chunk_start(c)
        score = jnp.maximum(s_ref[:, 0:tq], 0.0) * w_rows[0:1, :]
        for h in range(1, IDX_HEADS):
            score = score + jnp.maximum(s_ref[:, h * tq:(h + 1) * tq], 0.0) * w_rows[h:h + 1, :]
        key_scr[pl.ds(off, lc), :] = jnp.where(key_index(c) < lim, score, -jnp.inf)

    ia_scr[...] = idx_logits(0)

    def score_body(i, carry):
        c0 = 2 * i
        ib_scr[...] = idx_logits(c0 + 1)
        store_keys(ia_scr, c0)
        ia_scr[...] = idx_logits(jnp.minimum(c0 + 2, last))
        store_keys(ib_scr, c0 + 1)
        return carry

    lax.fori_loop(0, n_pairs, score_body, 0)

    @pl.when(odd)
    def _():
        store_keys(ia_scr, last)

    def count(pred):
        def chunk_flags(c):
            kk = key_scr[pl.ds(chunk_start(c), lc), :]
            m = jnp.where(pred(kk, c), 1, 0).astype(I32)
            return [m[r * SUBLANES:(r + 1) * SUBLANES, :] for r in range(lc // SUBLANES)]

        def tree_sum(flags):
            while len(flags) > 1:
                flags = [a + b for a, b in zip(flags[0::2], flags[1::2])]
            return flags[0]

        def span(first, n):
            flags = []
            for j in range(n):
                flags += chunk_flags(first + j)
            return tree_sum(flags)

        zero = jnp.zeros((SUBLANES, tq), I32)
        acc = lax.fori_loop(0, n_pairs, lambda i, a: a + span(2 * i, 2), zero)
        acc = acc + lax.cond(odd, lambda: span(last, 1), lambda: zero)
        return jnp.sum(acc, axis=0, keepdims=True)

    def code_to_float(code):
        bits = jnp.where(code < 0, (-code) | INT_MIN, code)
        return pltpu.bitcast(bits, F32)

    def bit_body(i, carry):
        code, cge = carry
        cand = code + jnp.left_shift(jnp.int32(1), 31 - i)
        cand_f = code_to_float(cand)
        c = count(lambda kk, _: kk >= cand_f)
        c = c - jnp.where(cand_f == -jnp.inf, n_inadmissible, 0)
        ok = c >= topk
        return jnp.where(ok, cand, code), jnp.where(ok, c, cge)

    code0 = jnp.full((1, tq), INT_MIN, I32)
    cge0 = jnp.zeros((1, tq), I32) + n_chunks * lc
    n_inadmissible = n_chunks * lc - lim
    code, cge = lax.fori_loop(0, 32, bit_body, (code0, cge0))
    thr = jnp.where(code > INT_MIN, code_to_float(code), -jnp.inf)

    n_idx_bits = max(1, (n_chunks_total * lc - 1).bit_length())
    tie_scr[...] = jnp.broadcast_to(jnp.where(thr == -jnp.inf, lim - 1, 2 ** 30), (SUBLANES, tq))
    surplus = jnp.logical_and(cge > topk, code > INT_MIN)

    @pl.when(jnp.max(jnp.where(surplus, 1, 0)) > 0)
    def _():
        cgt = count(lambda kk, _: kk > thr)
        want = topk - cgt
        bound = jnp.zeros((1, tq), I32)
        for bit in range(n_idx_bits - 1, -1, -1):
            cand = bound + (1 << bit)
            c = count(lambda kk, cc: jnp.logical_and(kk == thr, key_index(cc) < cand))
            bound = jnp.where(c < want, cand, bound)
        tie_scr[...] = jnp.broadcast_to(bound, (SUBLANES, tq))

    tie_bound = tie_scr[0:1, :]

    def bias_body(c, carry):
        off = chunk_start(c)
        kk = key_scr[pl.ds(off, lc), :]
        tie = jnp.where(key_index(c) <= tie_bound, 0.0, MASKED)
        b = jnp.where(kk > thr, 0.0, jnp.where(kk == thr, tie, MASKED))
        bias_scr[pl.ds(off, lc), :] = b.astype(BF16)
        return carry

    lax.fori_loop(0, n_chunks, bias_body, 0)

    gw = A_GROUP * tq
    eye = jnp.where(lax.broadcasted_iota(I32, (tq, gw), 0) == (lax.broadcasted_iota(I32, (tq, gw), 1) & (tq - 1)),
                    1.0, 0.0).astype(BF16)
    q_aug = [jnp.concatenate(
        [eye, jnp.concatenate(
            [qT_ref[0, 0, (g * A_GROUP + h) * A_HEAD_DIM:(g * A_GROUP + h + 1) * A_HEAD_DIM, :]
             for h in range(A_GROUP)], axis=1)], axis=0) for g in range(A_KV_HEADS)]
    acc_scr[...] = jnp.zeros(acc_scr.shape, F32)

    def logits(c, g):
        off = chunk_start(c)
        k_aug = jnp.concatenate([bias_scr[pl.ds(off, lc), :], kg_ref[0, g, pl.ds(off, lc), :]], axis=1)
        return _dot(k_aug, q_aug[g])

    def softmax_pv(s, c, g, m):
        mn = jnp.maximum(m, jnp.max(s, axis=0, keepdims=True))
        alpha = jnp.exp2(m - mn)
        p = jnp.exp2(s - mn).astype(BF16)
        vt = vT_ref[0, g, :, pl.ds(chunk_start(c), lc)]
        acc_scr[g] = alpha * acc_scr[g] + _dot(vt, p)
        return mn

    for g in range(A_KV_HEADS):
        sa_scr[g] = logits(0, g)

    def att_body(i, ms):
        c0 = 2 * i
        ms = list(ms)
        for g in range(A_KV_HEADS):
            sb_scr[g] = logits(c0 + 1, g)
            ms[g] = softmax_pv(sa_scr[g], c0, g, ms[g])
        c2 = jnp.minimum(c0 + 2, last)
        for g in range(A_KV_HEADS):
            sa_scr[g] = logits(c2, g)
            ms[g] = softmax_pv(sb_scr[g], c0 + 1, g, ms[g])
        return tuple(ms)

    m0 = tuple(jnp.full((1, gw), M_INIT, F32) for _ in range(A_KV_HEADS))
    ms = lax.fori_loop(0, n_pairs, att_body, m0)

    @pl.when(odd)
    def _():
        for g in range(A_KV_HEADS):
            softmax_pv(sa_scr[g], last, g, ms[g])

    for g in range(A_KV_HEADS):
        acc = acc_scr[g]
        o = acc[0:A_HEAD_DIM, :] / acc[A_HEAD_DIM:A_HEAD_DIM + 1, :]
        for h in range(A_GROUP):
            r0 = (g * A_GROUP + h) * A_HEAD_DIM
            oT_scr[r0:r0 + A_HEAD_DIM, :] = o[:, h * tq:(h + 1) * tq]

    o = jnp.transpose(oT_scr[...]).astype(BF16)
    res = x_ref[0] + _dot(o, wo_ref[...])
    if rows_out == tq:
        out_ref[0] = res
    else:
        res_scr[...] = res
        r0 = pl.multiple_of(pl.program_id(0) * rows_out, rows_out)
        out_ref[0] = res_scr[pl.ds(r0, rows_out), :]


def _dsa_attend(x, qT, qiT, wT, lim, kg, vT, kib, wo, *, topk, causal, shared_queries, rows_out):
    Bk, _, L, _ = kg.shape
    D = x.shape[-1]
    tq = qT.shape[-1]
    nq = qT.shape[1]
    assert L % DSA_KEY_CHUNK == 0
    if shared_queries:
        grid = (Bk, 1)
        qmap = lambda b, j: (0, 0, 0, 0)
        xmap = lambda b, j: (0, 0, 0)
        omap = lambda b, j: (0, b, 0)
        lmap = lambda b, j: (0, 0, 0)
    else:
        grid = (Bk, nq)
        qmap = lambda b, j: (b, j, 0, 0)
        xmap = lambda b, j: (b, j, 0)
        omap = xmap
        lmap = lambda b, j: (j, 0, 0)
    in_specs = [
        pl.BlockSpec((1, 1, tq), lmap),
        pl.BlockSpec((1, tq, D), xmap),
        pl.BlockSpec((1, 1, _NQ, tq), qmap),
        pl.BlockSpec((1, 1, _NQI, tq), qmap),
        pl.BlockSpec((1, 1, IDX_HEADS, tq), qmap),
        pl.BlockSpec((1, A_KV_HEADS, L, A_HEAD_DIM), lambda b, j: (b, 0, 0, 0)),
        pl.BlockSpec((1, A_KV_HEADS, V_AUG_ROWS, L), lambda b, j: (b, 0, 0, 0)),
        pl.BlockSpec((1, L, IDX_DIM), lambda b, j: (b, 0, 0)),
        pl.BlockSpec((None, _NQ, D), lambda b, j: (0, 0, 0)),
    ]
    kern = functools.partial(_dsa_attn_kernel, topk=topk, causal=causal,
                             n_chunks_total=L // DSA_KEY_CHUNK, rows_out=rows_out)
    return pl.pallas_call(
        kern, grid=grid, in_specs=in_specs, name="dsa_attend",
        out_specs=pl.BlockSpec((1, rows_out, D), omap),
        out_shape=jax.ShapeDtypeStruct(x.shape, F32),
        scratch_shapes=[pltpu.VMEM((L, tq), F32), pltpu.VMEM((SUBLANES, tq), I32),
                        pltpu.VMEM((_NQ, tq), F32), pltpu.VMEM((tq, D), F32),
                        pltpu.VMEM((A_KV_HEADS, V_AUG_ROWS, A_GROUP * tq), F32),
                        pltpu.VMEM((L, tq), BF16),
                        pltpu.VMEM((A_KV_HEADS, DSA_KEY_CHUNK, A_GROUP * tq), F32),
                        pltpu.VMEM((A_KV_HEADS, DSA_KEY_CHUNK, A_GROUP * tq), F32),
                        pltpu.VMEM((DSA_KEY_CHUNK, IDX_HEADS * tq), F32),
                        pltpu.VMEM((DSA_KEY_CHUNK, IDX_HEADS * tq), F32)],
        compiler_params=_params("arbitrary", "arbitrary"),
    )(lim, x, qT, qiT, wT, kg, vT, kib, wo)


CAST_STEPS = 8


def _cast_kernel(*refs):
    n = len(refs) // 2
    for src, dst in zip(refs[:n], refs[n:]):
        dst[...] = src[...].astype(BF16)


def _cast_bf16(*mats):
    specs = [pl.BlockSpec((m.shape[0], m.shape[1] // CAST_STEPS, m.shape[2]), lambda i: (0, i, 0)) for m in mats]
    return pl.pallas_call(
        _cast_kernel, grid=(CAST_STEPS,), name="cast_bf16",
        in_specs=specs, out_specs=tuple(specs),
        out_shape=tuple(jax.ShapeDtypeStruct(m.shape, BF16) for m in mats),
        compiler_params=_params("arbitrary"),
    )(*mats)


MLP_FF_CHUNK = 1024


def _mlp_kernel(x_ref, g_ref, wu_ref, wd_ref, gf_ref, out_ref, *, final_norm):
    x = x_ref[...]
    xn = _rms(x, g_ref[...]).astype(BF16)
    acc = x
    for f in range(0, wu_ref.shape[1], MLP_FF_CHUNK):
        h = jnp.maximum(_dot(xn, wu_ref[:, f:f + MLP_FF_CHUNK]), 0.0)
        acc = acc + _dot((h * h).astype(BF16), wd_ref[f:f + MLP_FF_CHUNK, :])
    if final_norm:
        acc = _rms(acc, gf_ref[...])
    out_ref[...] = acc


def _mlp(x, gain, w_up, w_down, layer, gain_final, *, tm, final_norm):
    M, D = x.shape
    FF = w_up.shape[2]
    const = lambda i: (0, 0)
    pick = lambda i: (layer, 0, 0)
    return pl.pallas_call(
        functools.partial(_mlp_kernel, final_norm=final_norm), name="mlp",
        grid=(M // tm,),
        in_specs=[
            pl.BlockSpec((tm, D), lambda i: (i, 0)),
            pl.BlockSpec((1, D), const),
            pl.BlockSpec((None, D, FF), pick, pipeline_mode=pl.Buffered(1)),
            pl.BlockSpec((None, FF, D), pick, pipeline_mode=pl.Buffered(1)),
            pl.BlockSpec((1, D), const),
        ],
        out_specs=pl.BlockSpec((tm, D), lambda i: (i, 0)),
        out_shape=jax.ShapeDtypeStruct((M, D), F32),
        compiler_params=_params("arbitrary"),
    )(x, gain.reshape(1, D), w_up, w_down, gain_final.reshape(1, D))


_GQK = GLA_HEADS * GLA_DK
_GV = GLA_HEADS * GLA_DV


def _gla_proj_kernel(x_ref, g_ref, w_ref, wa_ref, ba_ref, q_ref, k_ref, v_ref, gate_ref, la_ref):
    xn = _rms(x_ref[...], g_ref[...]).astype(BF16)
    y = _dot(xn, w_ref[...])
    D = gate_ref.shape[-1]
    q_ref[...] = y[:, 0:_GQK]
    k_ref[...] = y[:, _GQK:2 * _GQK]
    v_ref[...] = y[:, 2 * _GQK:2 * _GQK + _GV].astype(BF16)
    o = 2 * _GQK + _GV
    gate_ref[...] = y[:, o:o + D]
    a = y[:, o + D:o + D + GLA_RANK].astype(BF16)
    z = _dot(a, wa_ref[...]) + ba_ref[...]
    la_ref[...] = (jnp.minimum(z, 0.0) - jnp.log1p(jnp.exp(-jnp.abs(z)))) * (1.0 / GLA_TAU)


def _gla_project(x, gain, w_in, w_a2, b_a, *, tm):
    M, D = x.shape
    N = w_in.shape[2]
    const = lambda i: (0, 0)
    row = lambda i: (i, 0)
    return pl.pallas_call(
        _gla_proj_kernel, grid=(M // tm,), name="gla_project",
        in_specs=[
            pl.BlockSpec((tm, D), row),
            pl.BlockSpec((1, D), const),
            pl.BlockSpec((None, D, N), lambda i: (0, 0, 0)),
            pl.BlockSpec((GLA_RANK, _GQK), const),
            pl.BlockSpec((1, _GQK), const),
        ],
        out_specs=(
            pl.BlockSpec((tm, _GQK), row), pl.BlockSpec((tm, _GQK), row), pl.BlockSpec((tm, _GV), row),
            pl.BlockSpec((tm, D), row), pl.BlockSpec((tm, _GQK), row)),
        out_shape=(
            jax.ShapeDtypeStruct((M, _GQK), F32), jax.ShapeDtypeStruct((M, _GQK), F32),
            jax.ShapeDtypeStruct((M, _GV), BF16), jax.ShapeDtypeStruct((M, D), F32),
            jax.ShapeDtypeStruct((M, _GQK), F32)),
        compiler_params=_params("arbitrary"),
    )(x, gain.reshape(1, D), w_in, w_a2.astype(BF16), b_a.reshape(1, _GQK))


def _split3(a):
    a0 = a.astype(BF16)
    r = a - a0.astype(F32)
    a1 = r.astype(BF16)
    a2 = (r - a1.astype(F32)).astype(BF16)
    return a0, a1, a2


def _gla_kernel(x_ref, q_ref, k_ref, v_ref, gate_ref, la_ref, s0_ref, ng_ref, wo_ref,
                out_ref, sout_ref, s_scr, a_scr, o_scr):
    c = pl.program_id(1)
    C = q_ref.shape[1]

    @pl.when(c == 0)
    def _():
        s_scr[...] = s0_ref[0]

    la = la_ref[0]
    row = lax.broadcasted_iota(I32, (C, C), 0)
    col = lax.broadcasted_iota(I32, (C, C), 1)
    tri = jnp.where(row >= col, 1.0, 0.0).astype(BF16)
    l0, l1, l2 = _split3(la)
    b = _dot(tri, l0) + _dot(tri, l1) + _dot(tri, l2)
    q = q_ref[0] * (GLA_DK ** -0.5)
    k = k_ref[0]
    trow = lax.broadcasted_iota(I32, (C, 1), 0)

    def block_ref(values, size, pick):
        v3 = values.reshape(C // size, size, values.shape[-1])
        return jnp.broadcast_to(v3[:, pick:pick + 1, :], v3.shape).reshape(values.shape)

    def accumulate(qh, kh, mask, first):
        for h in range(GLA_HEADS):
            sl = slice(h * GLA_DK, (h + 1) * GLA_DK)
            blk = _dot_nt(qh[:, sl], kh[:, sl])
            if mask is not None:
                blk = jnp.where(mask, blk, 0.0)
            if first:
                a_scr[h] = blk
            else:
                a_scr[h] = a_scr[h] + blk

    d = min(GLA_DIAG, C)
    before = block_ref(b - la, d, 0)
    qh = (q * jnp.exp(b - before)).astype(BF16)
    kh = (k * jnp.exp(before - b)).astype(BF16)
    sh = d.bit_length() - 1
    accumulate(qh, kh, jnp.logical_and((row >> sh) == (col >> sh), row >= col), True)
    half = d
    while half < C:
        upper = (trow & (2 * half - 1)) >= half
        split = block_ref(b, 2 * half, half - 1)
        qh = (q * jnp.exp(jnp.where(upper, b - split, -jnp.inf))).astype(BF16)
        kh = (k * jnp.exp(jnp.where(upper, -jnp.inf, split - b))).astype(BF16)
        sh = (2 * half).bit_length() - 1
        accumulate(qh, kh, (row >> sh) == (col >> sh) if 2 * half < C else None, False)
        half *= 2

    b_end = b[C - 1:C, :]
    q_in = (q * jnp.exp(b)).astype(BF16)
    k_out = k * jnp.exp(b_end - b)
    ng = ng_ref[...]
    for h in range(GLA_HEADS):
        sl = slice(h * GLA_DK, (h + 1) * GLA_DK)
        vh = v_ref[0, :, h * GLA_DV:(h + 1) * GLA_DV]
        s_old = s_scr[h]
        o = _dot(q_in[:, sl], s_old.astype(BF16)) + _dot(a_scr[h].astype(BF16), vh)
        dec = jnp.transpose(jnp.broadcast_to(jnp.exp(b_end[:, sl]), (GLA_DK, GLA_DK)))
        dec = jnp.concatenate([dec] * (GLA_DV // GLA_DK), axis=1)
        s_scr[h] = dec * s_old + _dot(jnp.transpose(k_out[:, sl]).astype(BF16), vh)
        o = _rms(o, ng)
        gt = gate_ref[0, :, h * GLA_DV:(h + 1) * GLA_DV]
        o_scr[:, h * GLA_DV:(h + 1) * GLA_DV] = (o * (gt / (1.0 + jnp.exp(-gt)))).astype(BF16)

    out_ref[0] = x_ref[0] + _dot(o_scr[...], wo_ref[...])

    @pl.when(c == pl.num_programs(1) - 1)
    def _():
        sout_ref[0] = s_scr[...]


def _gla(x, q, k, v, gate, la, s0, norm_g, w_o, *, chunk):
    B, T, D = x.shape
    n = T // chunk
    tok = lambda b, c: (b, c, 0)
    const = lambda b, c: (0, 0)
    st = lambda b, c: (b, 0, 0, 0)
    return pl.pallas_call(
        _gla_kernel, grid=(B, n), name="gla",
        in_specs=[
            pl.BlockSpec((1, chunk, D), tok),
            pl.BlockSpec((1, chunk, _GQK), tok),
            pl.BlockSpec((1, chunk, _GQK), tok),
            pl.BlockSpec((1, chunk, _GV), tok),
            pl.BlockSpec((1, chunk, D), tok),
            pl.BlockSpec((1, chunk, _GQK), tok),
            pl.BlockSpec((1, GLA_HEADS, GLA_DK, GLA_DV), st),
            pl.BlockSpec((1, GLA_DV), const),
            pl.BlockSpec((None, _GV, D), lambda b, c: (0, 0, 0)),
        ],
        out_specs=(pl.BlockSpec((1, chunk, D), tok), pl.BlockSpec((1, GLA_HEADS, GLA_DK, GLA_DV), st)),
        out_shape=(jax.ShapeDtypeStruct((B, T, D), F32),
                   jax.ShapeDtypeStruct((B, GLA_HEADS, GLA_DK, GLA_DV), F32)),
        scratch_shapes=[pltpu.VMEM((GLA_HEADS, GLA_DK, GLA_DV), F32),
                        pltpu.VMEM((GLA_HEADS, chunk, chunk), F32),
                        pltpu.VMEM((chunk, _GV), BF16)],
        compiler_params=_params("arbitrary", "arbitrary"),
    )(x, q, k, v, gate, la, s0, norm_g.reshape(1, GLA_DV), w_o)


def _pick_tile(n, candidates):
    for t in candidates:
        if n % t == 0:
            return t
    return n


def _round_up(n, m):
    return (n + m - 1) // m * m


def kernel(x_prompt, x_sample, cache_k, cache_v, cache_kidx, state_gla, norm_mix, norm_ffn, norm_final,
           a_w_in, a_w_o, b_w_in, b_w_a2, b_b_a, b_norm, b_w_o, ffn_w_up, ffn_w_down):
    B, S, D = x_prompt.shape
    Bs, Ts, _ = x_sample.shape
    past = cache_k.shape[2]
    Ms = Bs * Ts
    tq = DSA_Q_TILE
    assert S % tq == 0 and Ms == tq and Ts % SUBLANES == 0

    wu_bf, wd_bf, bwin_bf, bwo_bf, wo_a = _cast_bf16(ffn_w_up, ffn_w_down, b_w_in, b_w_o, a_w_o)
    tm = _pick_tile(S, (1024, 512, 256, 128))
    qT, qiT, wT, vT, k_p, v_p, ki_p, kg, kib = _dsa_project(x_prompt, norm_mix[0], a_w_in[0], tm=tm, tq=tq)
    pos = jnp.arange(S, dtype=I32)
    lim_p = ((pos // STREAM_CHUNK + 1) * STREAM_CHUNK).reshape(S // tq, 1, tq)
    L_p = _round_up(S, DSA_KEY_CHUNK)
    if L_p != S:
        pad = L_p - S
        kg = jnp.pad(kg, ((0, 0), (0, 0), (0, pad), (0, 0)))
        vT = jnp.pad(vT, ((0, 0), (0, 0), (0, 0), (0, pad)))
        kib = jnp.pad(kib, ((0, 0), (0, pad), (0, 0)))
    hp = _dsa_attend(x_prompt, qT, qiT, wT, lim_p, kg, vT, kib, wo_a,
                     topk=min(TOPK_MAX, S // 4), causal=True, shared_queries=False, rows_out=tq)
    xs = x_sample.reshape(1, Ms, D)
    qT_s, qiT_s, wT_s, _, k_s, v_s, ki_s, _, _ = _dsa_project(xs, norm_mix[0], a_w_in[0], tm=Ms, tq=tq)
    k_s = k_s.reshape(Bs, Ts, _NKV)
    v_s = v_s.reshape(Bs, Ts, _NKV)
    ki_s = ki_s.reshape(Bs, Ts, IDX_DIM)
    L_real = past + Ts
    L_s = _round_up(L_real, DSA_KEY_CHUNK)
    kg_s, vT_s, kib_s = _stage_sample_keys(
        cache_k[0].reshape(Bs, past, _NKV), cache_v[0].reshape(Bs, past, _NKV), cache_kidx[0],
        k_s, v_s, ki_s, L_s)
    pos_s = past + jnp.arange(Ts, dtype=I32)
    lim_s = jnp.minimum((pos_s // STREAM_CHUNK + 1) * STREAM_CHUNK, L_real)
    lim_s = jnp.tile(lim_s, Bs).reshape(1, 1, Ms)
    hs = _dsa_attend(xs, qT_s, qiT_s, wT_s, lim_s, kg_s, vT_s, kib_s, wo_a,
                     topk=min(TOPK_MAX, L_real // 4), causal=False, shared_queries=True, rows_out=Ts)

    tmm = _pick_tile(B * S, (1024, 512, 256, 128))
    hp = _mlp(hp.reshape(B * S, D), norm_ffn[0], wu_bf, wd_bf, 0, norm_final, tm=tmm, final_norm=False)
    hs = _mlp(hs.reshape(Ms, D), norm_ffn[0], wu_bf, wd_bf, 0, norm_final, tm=Ms, final_norm=False)

    q, k, v, gate, la = _gla_project(hp, norm_mix[1], bwin_bf, b_w_a2[0], b_b_a[0], tm=tmm)
    chunk = _pick_tile(S, (256, 128, 64))
    r3 = lambda a: a.reshape(B, S, a.shape[-1])
    s0 = jnp.zeros((B, GLA_HEADS, GLA_DK, GLA_DV), F32)
    hp, s_p = _gla(hp.reshape(B, S, D), r3(q), r3(k), r3(v), r3(gate), r3(la), s0, b_norm[0], bwo_bf, chunk=chunk)
    q, k, v, gate, la = _gla_project(hs, norm_mix[1], bwin_bf, b_w_a2[0], b_b_a[0], tm=Ms)
    r3 = lambda a: a.reshape(Bs, Ts, a.shape[-1])
    hs, s_s = _gla(hs.reshape(Bs, Ts, D), r3(q), r3(k), r3(v), r3(gate), r3(la), state_gla[0], b_norm[0], bwo_bf,
                   chunk=Ts)

    y_p = _mlp(hp.reshape(B * S, D), norm_ffn[1], wu_bf, wd_bf, 1, norm_final, tm=tmm, final_norm=True)
    y_s = _mlp(hs.reshape(Ms, D), norm_ffn[1], wu_bf, wd_bf, 1, norm_final, tm=Ms, final_norm=True)

    return (y_p.reshape(B, S, D), y_s.reshape(Bs, Ts, D),
            k_p.reshape(1, B, S, A_KV_HEADS, A_HEAD_DIM), v_p.reshape(1, B, S, A_KV_HEADS, A_HEAD_DIM),
            ki_p.reshape(1, B, S, IDX_DIM), s_p[None],
            k_s.reshape(1, Bs, Ts, A_KV_HEADS, A_HEAD_DIM), v_s.reshape(1, Bs, Ts, A_KV_HEADS, A_HEAD_DIM),
            ki_s.reshape(1, Bs, Ts, IDX_DIM), s_s[None])
```

```python
import functools

import jax
import jax.numpy as jnp
from jax import lax
from jax.experimental import pallas as pl
from jax.experimental.pallas import tpu as pltpu

F32 = jnp.float32
BF16 = jnp.bfloat16
I32 = jnp.int32

EPS = 1e-6
STREAM_CHUNK = 64
TOPK_MAX = 256
A_HEADS = 16
A_KV_HEADS = 4
A_GROUP = A_HEADS // A_KV_HEADS
A_HEAD_DIM = 64
IDX_HEADS = 8
IDX_DIM = 64
GLA_HEADS = 4
GLA_DK = 128
GLA_DV = 256
GLA_RANK = 16
GLA_TAU = 16.0

LANES = 128
SUBLANES = 8
INT_MIN = -(2 ** 31)
NEG_BIG = -1e30

DSA_Q_TILE = LANES
DSA_KEY_CHUNK = 256
GLA_DIAG = 16
VMEM_LIMIT = 56 * 1024 * 1024


def _params(*sem):
    return pltpu.CompilerParams(dimension_semantics=sem, vmem_limit_bytes=VMEM_LIMIT)


def _rms(x, g):
    ms = jnp.mean(x * x, axis=-1, keepdims=True)
    return x * lax.rsqrt(ms + EPS) * g


def _dot(a, b):
    return jnp.dot(a, b, preferred_element_type=F32)


def _dot_nt(a, b):
    return lax.dot_general(a, b, (((1,), (1,)), ((), ())), preferred_element_type=F32)


_NQ = A_HEADS * A_HEAD_DIM
_NKV = A_KV_HEADS * A_HEAD_DIM
_NQI = IDX_HEADS * IDX_DIM
_T_ROWS = _NQ + _NQI + _NKV + 16
BF16_SUBLANES = 16
V_AUG_ROWS = A_HEAD_DIM + BF16_SUBLANES
LOG2E = 1.4426950408889634
MASKED = -1e30
M_INIT = -1e29


def _dsa_proj_kernel(x_ref, g_ref, wt_ref, wn_ref,
                     qT_ref, qiT_ref, wT_ref, vT_ref, k_ref, v_ref, ki_ref, kg_ref, kib_ref, *, tq):
    xn = _rms(x_ref[0], g_ref[...]).astype(BF16)
    tm = xn.shape[0]
    yT = _dot_nt(wt_ref[...], xn)
    y = _dot(xn, wn_ref[...])
    for j in range(tm // tq):
        sl = slice(j * tq, (j + 1) * tq)
        qT_ref[0, j] = yT[0:_NQ, sl].astype(BF16)
        qiT_ref[0, j] = yT[_NQ:_NQ + _NQI, sl].astype(BF16)
        wT_ref[0, j] = yT[_NQ + _NQI + _NKV:_NQ + _NQI + _NKV + IDX_HEADS, sl]
    for h in range(A_KV_HEADS):
        r0 = _NQ + _NQI + h * A_HEAD_DIM
        vT_ref[0, h, 0:A_HEAD_DIM, :] = yT[r0:r0 + A_HEAD_DIM, :].astype(BF16)
        vT_ref[0, h, A_HEAD_DIM:V_AUG_ROWS, :] = jnp.ones((V_AUG_ROWS - A_HEAD_DIM, tm), BF16)
    k = y[:, 0:_NKV]
    ki = y[:, 2 * _NKV:2 * _NKV + IDX_DIM]
    k_ref[0] = k
    v_ref[0] = y[:, _NKV:2 * _NKV]
    ki_ref[0] = ki
    for h in range(A_KV_HEADS):
        kg_ref[0, h] = k[:, h * A_HEAD_DIM:(h + 1) * A_HEAD_DIM].astype(BF16)
    kib_ref[0] = ki.astype(BF16)


def _dsa_project(x, gain, w_in, *, tm, tq):
    B, S, D = x.shape
    o = 0
    parts = []
    for n in (_NQ, _NKV, _NKV, _NQI, IDX_DIM, IDX_HEADS):
        parts.append(w_in[:, o:o + n])
        o += n
    w_q, w_k, w_v, w_qi, w_ki, w_wt = parts
    wt_all = jnp.concatenate(
        [w_q * (A_HEAD_DIM ** -0.5 * LOG2E), w_qi, w_v, w_wt, jnp.zeros((D, 16 - IDX_HEADS), F32)],
        axis=1).T.astype(BF16)
    wn_all = jnp.concatenate([w_k, w_v, w_ki], axis=1).astype(BF16)
    nq = S // tq
    nt = S // tm
    jq = tm // tq
    out_shape = (
        jax.ShapeDtypeStruct((B, nq, _NQ, tq), BF16),
        jax.ShapeDtypeStruct((B, nq, _NQI, tq), BF16),
        jax.ShapeDtypeStruct((B, nq, IDX_HEADS, tq), F32),
        jax.ShapeDtypeStruct((B, A_KV_HEADS, V_AUG_ROWS, S), BF16),
        jax.ShapeDtypeStruct((B, S, _NKV), F32),
        jax.ShapeDtypeStruct((B, S, _NKV), F32),
        jax.ShapeDtypeStruct((B, S, IDX_DIM), F32),
        jax.ShapeDtypeStruct((B, A_KV_HEADS, S, A_HEAD_DIM), BF16),
        jax.ShapeDtypeStruct((B, S, IDX_DIM), BF16),
    )
    out_specs = (
        pl.BlockSpec((1, jq, _NQ, tq), lambda b, i: (b, i, 0, 0)),
        pl.BlockSpec((1, jq, _NQI, tq), lambda b, i: (b, i, 0, 0)),
        pl.BlockSpec((1, jq, IDX_HEADS, tq), lambda b, i: (b, i, 0, 0)),
        pl.BlockSpec((1, A_KV_HEADS, V_AUG_ROWS, tm), lambda b, i: (b, 0, 0, i)),
        pl.BlockSpec((1, tm, _NKV), lambda b, i: (b, i, 0)),
        pl.BlockSpec((1, tm, _NKV), lambda b, i: (b, i, 0)),
        pl.BlockSpec((1, tm, IDX_DIM), lambda b, i: (b, i, 0)),
        pl.BlockSpec((1, A_KV_HEADS, tm, A_HEAD_DIM), lambda b, i: (b, 0, i, 0)),
        pl.BlockSpec((1, tm, IDX_DIM), lambda b, i: (b, i, 0)),
    )
    in_specs = [
        pl.BlockSpec((1, tm, D), lambda b, i: (b, i, 0)),
        pl.BlockSpec((1, D), lambda b, i: (0, 0)),
        pl.BlockSpec((_T_ROWS, D), lambda b, i: (0, 0)),
        pl.BlockSpec((D, 2 * _NKV + IDX_DIM), lambda b, i: (0, 0)),
    ]
    return pl.pallas_call(
        functools.partial(_dsa_proj_kernel, tq=tq), name="dsa_project",
        grid=(B, nt), in_specs=in_specs, out_specs=out_specs, out_shape=out_shape,
        compiler_params=_params("arbitrary", "arbitrary"),
    )(x, gain.reshape(1, D), wt_all, wn_all)


def _stage_keys_kernel(ck_ref, cv_ref, cki_ref, nk_ref, nv_ref, nki_ref, kg_ref, vT_ref, kib_ref, kv_scr):
    past = ck_ref.shape[1]
    new = nk_ref.shape[1]
    L = kv_scr.shape[0]

    def gather_rows(cache_ref, new_ref, dst_ref, cast):
        dst_ref[0:past, :] = cast(cache_ref[0])
        dst_ref[past:past + new, :] = cast(new_ref[0])
        if past + new < L:
            dst_ref[past + new:L, :] = jnp.zeros((L - past - new, dst_ref.shape[-1]), dst_ref.dtype)

    gather_rows(ck_ref, nk_ref, kv_scr, lambda a: a)
    k_all = kv_scr[...]
    for h in range(A_KV_HEADS):
        kg_ref[0, h] = k_all[:, h * A_HEAD_DIM:(h + 1) * A_HEAD_DIM].astype(BF16)
    gather_rows(cv_ref, nv_ref, kv_scr, lambda a: a)
    v_t = jnp.transpose(kv_scr[...])
    for h in range(A_KV_HEADS):
        vT_ref[0, h, 0:A_HEAD_DIM, :] = v_t[h * A_HEAD_DIM:(h + 1) * A_HEAD_DIM, :].astype(BF16)
        vT_ref[0, h, A_HEAD_DIM:V_AUG_ROWS, :] = jnp.ones((V_AUG_ROWS - A_HEAD_DIM, L), BF16)
    gather_rows(cki_ref, nki_ref, kib_ref.at[0], lambda a: a.astype(BF16))


def _stage_sample_keys(cache_k, cache_v, cache_ki, new_k, new_v, new_ki, L):
    Bs, past, _ = cache_k.shape
    new = new_k.shape[1]
    row = lambda b: (b, 0, 0)
    return pl.pallas_call(
        _stage_keys_kernel, grid=(Bs,), name="stage_sample_keys",
        in_specs=[
            pl.BlockSpec((1, past, _NKV), row), pl.BlockSpec((1, past, _NKV), row),
            pl.BlockSpec((1, past, IDX_DIM), row),
            pl.BlockSpec((1, new, _NKV), row), pl.BlockSpec((1, new, _NKV), row),
            pl.BlockSpec((1, new, IDX_DIM), row),
        ],
        out_specs=(
            pl.BlockSpec((1, A_KV_HEADS, L, A_HEAD_DIM), lambda b: (b, 0, 0, 0)),
            pl.BlockSpec((1, A_KV_HEADS, V_AUG_ROWS, L), lambda b: (b, 0, 0, 0)),
            pl.BlockSpec((1, L, IDX_DIM), row),
        ),
        out_shape=(
            jax.ShapeDtypeStruct((Bs, A_KV_HEADS, L, A_HEAD_DIM), BF16),
            jax.ShapeDtypeStruct((Bs, A_KV_HEADS, V_AUG_ROWS, L), BF16),
            jax.ShapeDtypeStruct((Bs, L, IDX_DIM), BF16),
        ),
        scratch_shapes=[pltpu.VMEM((L, _NKV), F32)],
        compiler_params=_params("arbitrary"),
    )(cache_k, cache_v, cache_ki, new_k, new_v, new_ki)


def _dsa_attn_kernel(lim_ref, x_ref, qT_ref, qiT_ref, wT_ref, kg_ref, vT_ref, kib_ref, wo_ref,
                     out_ref, key_scr, tie_scr, oT_scr, res_scr, acc_scr, bias_scr, sa_scr, sb_scr,
                     ia_scr, ib_scr,
                     *, topk, causal, n_chunks_total, rows_out):
    tq = qT_ref.shape[-1]
    lc = DSA_KEY_CHUNK
    if causal:
        n_chunks = jnp.minimum(((pl.program_id(1) + 1) * tq + lc - 1) // lc, n_chunks_total)
    else:
        n_chunks = jnp.int32(n_chunks_total)
    n_pairs = n_chunks // 2
    odd = (n_chunks & 1) == 1
    last = n_chunks - 1
    lim = lim_ref[0]

    def chunk_start(c):
        return pl.multiple_of(c * lc, lc)

    def key_index(c):
        return c * lc + lax.broadcasted_iota(I32, (lc, tq), 0)

    qi_all = jnp.concatenate(
        [qiT_ref[0, 0, h * IDX_DIM:(h + 1) * IDX_DIM, :] for h in range(IDX_HEADS)], axis=1)
    w_rows = wT_ref[0, 0] * ((IDX_DIM ** -0.5) * (IDX_HEADS ** -0.5))

    def idx_logits(c):
        return _dot(kib_ref[0, pl.ds(chunk_start(c), lc), :], qi_all)

    def store_keys(s_ref, c):
        off = chunk_start(c)
        score = jnp.maximum(s_ref[:, 0:tq], 0.0) * w_rows[0:1, :]
        for h in range(1, IDX_HEADS):
            score = score + jnp.maximum(s_ref[:, h * tq:(h + 1) * tq], 0.0) * w_rows[h:h + 1, :]
        key_scr[pl.ds(off, lc), :] = jnp.where(key_index(c) < lim, score, -jnp.inf)

    ia_scr[...] = idx_logits(0)

    def score_body(i, carry):
        c0 = 2 * i
        ib_scr[...] = idx_logits(c0 + 1)
        store_keys(ia_scr, c0)
        ia_scr[...] = idx_logits(jnp.minimum(c0 + 2, last))
        store_keys(ib_scr, c0 + 1)
        return carry

    lax.fori_loop(0, n_pairs, score_body, 0)

    @pl.when(odd)
    def _():
        store_keys(ia_scr, last)

    def count(pred):
        def chunk_flags(c):
            kk = key_scr[pl.ds(chunk_start(c), lc), :]
            m = jnp.where(pred(kk, c), 1, 0).astype(I32)
            return [m[r * SUBLANES:(r + 1) * SUBLANES, :] for r in range(lc // SUBLANES)]

        def tree_sum(flags):
            while len(flags) > 1:
                flags = [a + b for a, b in zip(flags[0::2], flags[1::2])]
            return flags[0]

        def span(first, n):
            flags = []
            for j in range(n):
                flags += chunk_flags(first + j)
            return tree_sum(flags)

        zero = jnp.zeros((SUBLANES, tq), I32)
        acc = lax.fori_loop(0, n_pairs, lambda i, a: a + span(2 * i, 2), zero)
        acc = acc + lax.cond(odd, lambda: span(last, 1), lambda: zero)
        return jnp.sum(acc, axis=0, keepdims=True)

    def code_to_float(code):
        bits = jnp.where(code < 0, (-code) | INT_MIN, code)
        return pltpu.bitcast(bits, F32)

    def bit_body(i, carry):
        code, cge = carry
        cand = code + jnp.left_shift(jnp.int32(1), 31 - i)
        cand_f = code_to_float(cand)
        c = count(lambda kk, _: kk >= cand_f)
        c = c - jnp.where(cand_f == -jnp.inf, n_inadmissible, 0)
        ok = c >= topk
        return jnp.where(ok, cand, code), jnp.where(ok, c, cge)

    code0 = jnp.full((1, tq), INT_MIN, I32)
    cge0 = jnp.zeros((1, tq), I32) + n_chunks * lc
    n_inadmissible = n_chunks * lc - lim
    code, cge = lax.fori_loop(0, 32, bit_body, (code0, cge0))
    thr = jnp.where(code > INT_MIN, code_to_float(code), -jnp.inf)

    n_idx_bits = max(1, (n_chunks_total * lc - 1).bit_length())
    tie_scr[...] = jnp.broadcast_to(jnp.where(thr == -jnp.inf, lim - 1, 2 ** 30), (SUBLANES, tq))
    surplus = jnp.logical_and(cge > topk, code > INT_MIN)

    @pl.when(jnp.max(jnp.where(surplus, 1, 0)) > 0)
    def _():
        cgt = count(lambda kk, _: kk > thr)
        want = topk - cgt
        bound = jnp.zeros((1, tq), I32)
        for bit in range(n_idx_bits - 1, -1, -1):
            cand = bound + (1 << bit)
            c = count(lambda kk, cc: jnp.logical_and(kk == thr, key_index(cc) < cand))
            bound = jnp.where(c < want, cand, bound)
        tie_scr[...] = jnp.broadcast_to(bound, (SUBLANES, tq))

    tie_bound = tie_scr[0:1, :]

    def bias_body(c, carry):
        off = chunk_start(c)
        kk = key_scr[pl.ds(off, lc), :]
        tie = jnp.where(key_index(c) <= tie_bound, 0.0, MASKED)
        b = jnp.where(kk > thr, 0.0, jnp.where(kk == thr, tie, MASKED))
        bias_scr[pl.ds(off, lc), :] = b.astype(BF16)
        return carry

    def bias_body_all_ties(c, carry):
        off = chunk_start(c)
        b = jnp.where(key_scr[pl.ds(off, lc), :] >= thr, 0.0, MASKED)
        bias_scr[pl.ds(off, lc), :] = b.astype(BF16)
        return carry

    bounded = jnp.max(jnp.where(jnp.logical_or(surplus, thr == -jnp.inf), 1, 0)) > 0

    @pl.when(bounded)
    def _():
        lax.fori_loop(0, n_chunks, bias_body, 0)

    @pl.when(jnp.logical_not(bounded))
    def _():
        lax.fori_loop(0, n_chunks, bias_body_all_ties, 0)

    gw = A_GROUP * tq
    eye = jnp.where(lax.broadcasted_iota(I32, (tq, gw), 0) == (lax.broadcasted_iota(I32, (tq, gw), 1) & (tq - 1)),
                    1.0, 0.0).astype(BF16)
    q_aug = [jnp.concatenate(
        [eye, jnp.concatenate(
            [qT_ref[0, 0, (g * A_GROUP + h) * A_HEAD_DIM:(g * A_GROUP + h + 1) * A_HEAD_DIM, :]
             for h in range(A_GROUP)], axis=1)], axis=0) for g in range(A_KV_HEADS)]
    acc_scr[...] = jnp.zeros(acc_scr.shape, F32)

    def logits(c, g):
        off = chunk_start(c)
        k_aug = jnp.concatenate([bias_scr[pl.ds(off, lc), :], kg_ref[0, g, pl.ds(off, lc), :]], axis=1)
        return _dot(k_aug, q_aug[g])

    def softmax_pv(s, c, g, m):
        mn = jnp.maximum(m, jnp.max(s, axis=0, keepdims=True))
        alpha = jnp.exp2(m - mn)
        p = jnp.exp2(s - mn).astype(BF16)
        vt = vT_ref[0, g, :, pl.ds(chunk_start(c), lc)]
        acc_scr[g] = alpha * acc_scr[g] + _dot(vt, p)
        return mn

    for g in range(A_KV_HEADS):
        sa_scr[g] = logits(0, g)

    def att_body(i, ms):
        c0 = 2 * i
        ms = list(ms)
        for g in range(A_KV_HEADS):
            sb_scr[g] = logits(c0 + 1, g)
            ms[g] = softmax_pv(sa_scr[g], c0, g, ms[g])
        c2 = jnp.minimum(c0 + 2, last)
        for g in range(A_KV_HEADS):
            sa_scr[g] = logits(c2, g)
            ms[g] = softmax_pv(sb_scr[g], c0 + 1, g, ms[g])
        return tuple(ms)

    m0 = tuple(jnp.full((1, gw), M_INIT, F32) for _ in range(A_KV_HEADS))
    ms = lax.fori_loop(0, n_pairs, att_body, m0)

    @pl.when(odd)
    def _():
        for g in range(A_KV_HEADS):
            softmax_pv(sa_scr[g], last, g, ms[g])

    for g in range(A_KV_HEADS):
        acc = acc_scr[g]
        o = acc[0:A_HEAD_DIM, :] / acc[A_HEAD_DIM:A_HEAD_DIM + 1, :]
        for h in range(A_GROUP):
            r0 = (g * A_GROUP + h) * A_HEAD_DIM
            oT_scr[r0:r0 + A_HEAD_DIM, :] = o[:, h * tq:(h + 1) * tq]

    o = jnp.transpose(oT_scr[...]).astype(BF16)
    res = x_ref[0] + _dot(o, wo_ref[...])
    if rows_out == tq:
        out_ref[0] = res
    else:
        res_scr[...] = res
        r0 = pl.multiple_of(pl.program_id(0) * rows_out, rows_out)
        out_ref[0] = res_scr[pl.ds(r0, rows_out), :]


def _dsa_attend(x, qT, qiT, wT, lim, kg, vT, kib, wo, *, topk, causal, shared_queries, rows_out):
    Bk, _, L, _ = kg.shape
    D = x.shape[-1]
    tq = qT.shape[-1]
    nq = qT.shape[1]
    assert L % DSA_KEY_CHUNK == 0
    if shared_queries:
        grid = (Bk, 1)
        qmap = lambda b, j: (0, 0, 0, 0)
        xmap = lambda b, j: (0, 0, 0)
        omap = lambda b, j: (0, b, 0)
        lmap = lambda b, j: (0, 0, 0)
    else:
        grid = (Bk, nq)
        qmap = lambda b, j: (b, j, 0, 0)
        xmap = lambda b, j: (b, j, 0)
        omap = xmap
        lmap = lambda b, j: (j, 0, 0)
    in_specs = [
        pl.BlockSpec((1, 1, tq), lmap),
        pl.BlockSpec((1, tq, D), xmap),
        pl.BlockSpec((1, 1, _NQ, tq), qmap),
        pl.BlockSpec((1, 1, _NQI, tq), qmap),
        pl.BlockSpec((1, 1, IDX_HEADS, tq), qmap),
        pl.BlockSpec((1, A_KV_HEADS, L, A_HEAD_DIM), lambda b, j: (b, 0, 0, 0)),
        pl.BlockSpec((1, A_KV_HEADS, V_AUG_ROWS, L), lambda b, j: (b, 0, 0, 0)),
        pl.BlockSpec((1, L, IDX_DIM), lambda b, j: (b, 0, 0)),
        pl.BlockSpec((None, _NQ, D), lambda b, j: (0, 0, 0)),
    ]
    kern = functools.partial(_dsa_attn_kernel, topk=topk, causal=causal,
                             n_chunks_total=L // DSA_KEY_CHUNK, rows_out=rows_out)
    return pl.pallas_call(
        kern, grid=grid, in_specs=in_specs, name="dsa_attend",
        out_specs=pl.BlockSpec((1, rows_out, D), omap),
        out_shape=jax.ShapeDtypeStruct(x.shape, F32),
        scratch_shapes=[pltpu.VMEM((L, tq), F32), pltpu.VMEM((SUBLANES, tq), I32),
                        pltpu.VMEM((_NQ, tq), F32), pltpu.VMEM((tq, D), F32),
                        pltpu.VMEM((A_KV_HEADS, V_AUG_ROWS, A_GROUP * tq), F32),
                        pltpu.VMEM((L, tq), BF16),
                        pltpu.VMEM((A_KV_HEADS, DSA_KEY_CHUNK, A_GROUP * tq), F32),
                        pltpu.VMEM((A_KV_HEADS, DSA_KEY_CHUNK, A_GROUP * tq), F32),
                        pltpu.VMEM((DSA_KEY_CHUNK, IDX_HEADS * tq), F32),
                        pltpu.VMEM((DSA_KEY_CHUNK, IDX_HEADS * tq), F32)],
        compiler_params=_params("arbitrary", "arbitrary"),
    )(lim, x, qT, qiT, wT, kg, vT, kib, wo)


CAST_STEPS = 8


def _cast_kernel(*refs):
    n = len(refs) // 2
    for src, dst in zip(refs[:n], refs[n:]):
        dst[...] = src[...].astype(BF16)


def _cast_bf16(*mats):
    specs = [pl.BlockSpec((m.shape[0], m.shape[1] // CAST_STEPS, m.shape[2]), lambda i: (0, i, 0)) for m in mats]
    return pl.pallas_call(
        _cast_kernel, grid=(CAST_STEPS,), name="cast_bf16",
        in_specs=specs, out_specs=tuple(specs),
        out_shape=tuple(jax.ShapeDtypeStruct(m.shape, BF16) for m in mats),
        compiler_params=_params("arbitrary"),
    )(*mats)


MLP_FF_CHUNK = 1024


def _mlp_kernel(x_ref, g_ref, wu_ref, wd_ref, gf_ref, out_ref, *, final_norm):
    x = x_ref[...]
    xn = _rms(x, g_ref[...]).astype(BF16)
    acc = x
    for f in range(0, wu_ref.shape[1], MLP_FF_CHUNK):
        h = jnp.maximum(_dot(xn, wu_ref[:, f:f + MLP_FF_CHUNK]), 0.0)
        acc = acc + _dot((h * h).astype(BF16), wd_ref[f:f + MLP_FF_CHUNK, :])
    if final_norm:
        acc = _rms(acc, gf_ref[...])
    out_ref[...] = acc


def _mlp(x, gain, w_up, w_down, layer, gain_final, *, tm, final_norm):
    M, D = x.shape
    FF = w_up.shape[2]
    const = lambda i: (0, 0)
    pick = lambda i: (layer, 0, 0)
    return pl.pallas_call(
        functools.partial(_mlp_kernel, final_norm=final_norm), name="mlp",
        grid=(M // tm,),
        in_specs=[
            pl.BlockSpec((tm, D), lambda i: (i, 0)),
            pl.BlockSpec((1, D), const),
            pl.BlockSpec((None, D, FF), pick, pipeline_mode=pl.Buffered(1)),
            pl.BlockSpec((None, FF, D), pick, pipeline_mode=pl.Buffered(1)),
            pl.BlockSpec((1, D), const),
        ],
        out_specs=pl.BlockSpec((tm, D), lambda i: (i, 0)),
        out_shape=jax.ShapeDtypeStruct((M, D), F32),
        compiler_params=_params("arbitrary"),
    )(x, gain.reshape(1, D), w_up, w_down, gain_final.reshape(1, D))


_GQK = GLA_HEADS * GLA_DK
_GV = GLA_HEADS * GLA_DV


def _gla_proj_kernel(x_ref, g_ref, w_ref, wa_ref, ba_ref, q_ref, k_ref, v_ref, gate_ref, la_ref):
    xn = _rms(x_ref[...], g_ref[...]).astype(BF16)
    y = _dot(xn, w_ref[...])
    D = gate_ref.shape[-1]
    q_ref[...] = y[:, 0:_GQK]
    k_ref[...] = y[:, _GQK:2 * _GQK]
    v_ref[...] = y[:, 2 * _GQK:2 * _GQK + _GV].astype(BF16)
    o = 2 * _GQK + _GV
    gate_ref[...] = y[:, o:o + D]
    a = y[:, o + D:o + D + GLA_RANK].astype(BF16)
    z = _dot(a, wa_ref[...]) + ba_ref[...]
    la_ref[...] = (jnp.minimum(z, 0.0) - jnp.log1p(jnp.exp(-jnp.abs(z)))) * (1.0 / GLA_TAU)


def _gla_project(x, gain, w_in, w_a2, b_a, *, tm):
    M, D = x.shape
    N = w_in.shape[2]
    const = lambda i: (0, 0)
    row = lambda i: (i, 0)
    return pl.pallas_call(
        _gla_proj_kernel, grid=(M // tm,), name="gla_project",
        in_specs=[
            pl.BlockSpec((tm, D), row),
            pl.BlockSpec((1, D), const),
            pl.BlockSpec((None, D, N), lambda i: (0, 0, 0)),
            pl.BlockSpec((GLA_RANK, _GQK), const),
            pl.BlockSpec((1, _GQK), const),
        ],
        out_specs=(
            pl.BlockSpec((tm, _GQK), row), pl.BlockSpec((tm, _GQK), row), pl.BlockSpec((tm, _GV), row),
            pl.BlockSpec((tm, D), row), pl.BlockSpec((tm, _GQK), row)),
        out_shape=(
            jax.ShapeDtypeStruct((M, _GQK), F32), jax.ShapeDtypeStruct((M, _GQK), F32),
            jax.ShapeDtypeStruct((M, _GV), BF16), jax.ShapeDtypeStruct((M, D), F32),
            jax.ShapeDtypeStruct((M, _GQK), F32)),
        compiler_params=_params("arbitrary"),
    )(x, gain.reshape(1, D), w_in, w_a2.astype(BF16), b_a.reshape(1, _GQK))


def _split3(a):
    a0 = a.astype(BF16)
    r = a - a0.astype(F32)
    a1 = r.astype(BF16)
    a2 = (r - a1.astype(F32)).astype(BF16)
    return a0, a1, a2


def _gla_kernel(x_ref, q_ref, k_ref, v_ref, gate_ref, la_ref, s0_ref, ng_ref, wo_ref,
                out_ref, sout_ref, s_scr, a_scr, o_scr):
    c = pl.program_id(1)
    C = q_ref.shape[1]

    @pl.when(c == 0)
    def _():
        s_scr[...] = s0_ref[0]

    la = la_ref[0]
    row = lax.broadcasted_iota(I32, (C, C), 0)
    col = lax.broadcasted_iota(I32, (C, C), 1)
    tri = jnp.where(row >= col, 1.0, 0.0).astype(BF16)
    l0, l1, l2 = _split3(la)
    b = _dot(tri, l0) + _dot(tri, l1) + _dot(tri, l2)
    q = q_ref[0] * (GLA_DK ** -0.5)
    k = k_ref[0]
    trow = lax.broadcasted_iota(I32, (C, 1), 0)

    def block_ref(values, size, pick):
        v3 = values.reshape(C // size, size, values.shape[-1])
        return jnp.broadcast_to(v3[:, pick:pick + 1, :], v3.shape).reshape(values.shape)

    def accumulate(qh, kh, mask, first):
        for h in range(GLA_HEADS):
            sl = slice(h * GLA_DK, (h + 1) * GLA_DK)
            blk = _dot_nt(qh[:, sl], kh[:, sl])
            if mask is not None:
                blk = jnp.where(mask, blk, 0.0)
            if first:
                a_scr[h] = blk
            else:
                a_scr[h] = a_scr[h] + blk

    d = min(GLA_DIAG, C)
    before = block_ref(b - la, d, 0)
    qh = (q * jnp.exp(b - before)).astype(BF16)
    kh = (k * jnp.exp(before - b)).astype(BF16)
    sh = d.bit_length() - 1
    accumulate(qh, kh, jnp.logical_and((row >> sh) == (col >> sh), row >= col), True)
    half = d
    while half < C:
        upper = (trow & (2 * half - 1)) >= half
        split = block_ref(b, 2 * half, half - 1)
        qh = (q * jnp.exp(jnp.where(upper, b - split, -jnp.inf))).astype(BF16)
        kh = (k * jnp.exp(jnp.where(upper, -jnp.inf, split - b))).astype(BF16)
        sh = (2 * half).bit_length() - 1
        accumulate(qh, kh, (row >> sh) == (col >> sh) if 2 * half < C else None, False)
        half *= 2

    b_end = b[C - 1:C, :]
    q_in = (q * jnp.exp(b)).astype(BF16)
    k_out = k * jnp.exp(b_end - b)
    ng = ng_ref[...]
    for h in range(GLA_HEADS):
        sl = slice(h * GLA_DK, (h + 1) * GLA_DK)
        vh = v_ref[0, :, h * GLA_DV:(h + 1) * GLA_DV]
        s_old = s_scr[h]
        o = _dot(q_in[:, sl], s_old.astype(BF16)) + _dot(a_scr[h].astype(BF16), vh)
        dec = jnp.transpose(jnp.broadcast_to(jnp.exp(b_end[:, sl]), (GLA_DK, GLA_DK)))
        dec = jnp.concatenate([dec] * (GLA_DV // GLA_DK), axis=1)
        s_scr[h] = dec * s_old + _dot(jnp.transpose(k_out[:, sl]).astype(BF16), vh)
        o = _rms(o, ng)
        gt = gate_ref[0, :, h * GLA_DV:(h + 1) * GLA_DV]
        o_scr[:, h * GLA_DV:(h + 1) * GLA_DV] = (o * (gt / (1.0 + jnp.exp(-gt)))).astype(BF16)

    out_ref[0] = x_ref[0] + _dot(o_scr[...], wo_ref[...])

    @pl.when(c == pl.num_programs(1) - 1)
    def _():
        sout_ref[0] = s_scr[...]


def _gla(x, q, k, v, gate, la, s0, norm_g, w_o, *, chunk):
    B, T, D = x.shape
    n = T // chunk
    tok = lambda b, c: (b, c, 0)
    const = lambda b, c: (0, 0)
    st = lambda b, c: (b, 0, 0, 0)
    return pl.pallas_call(
        _gla_kernel, grid=(B, n), name="gla",
        in_specs=[
            pl.BlockSpec((1, chunk, D), tok),
            pl.BlockSpec((1, chunk, _GQK), tok),
            pl.BlockSpec((1, chunk, _GQK), tok),
            pl.BlockSpec((1, chunk, _GV), tok),
            pl.BlockSpec((1, chunk, D), tok),
            pl.BlockSpec((1, chunk, _GQK), tok),
            pl.BlockSpec((1, GLA_HEADS, GLA_DK, GLA_DV), st),
            pl.BlockSpec((1, GLA_DV), const),
            pl.BlockSpec((None, _GV, D), lambda b, c: (0, 0, 0)),
        ],
        out_specs=(pl.BlockSpec((1, chunk, D), tok), pl.BlockSpec((1, GLA_HEADS, GLA_DK, GLA_DV), st)),
        out_shape=(jax.ShapeDtypeStruct((B, T, D), F32),
                   jax.ShapeDtypeStruct((B, GLA_HEADS, GLA_DK, GLA_DV), F32)),
        scratch_shapes=[pltpu.VMEM((GLA_HEADS, GLA_DK, GLA_DV), F32),
                        pltpu.VMEM((GLA_HEADS, chunk, chunk), F32),
                        pltpu.VMEM((chunk, _GV), BF16)],
        compiler_params=_params("arbitrary", "arbitrary"),
    )(x, q, k, v, gate, la, s0, norm_g.reshape(1, GLA_DV), w_o)


def _pick_tile(n, candidates):
    for t in candidates:
        if n % t == 0:
            return t
    return n


def _round_up(n, m):
    return (n + m - 1) // m * m


def kernel(x_prompt, x_sample, cache_k, cache_v, cache_kidx, state_gla, norm_mix, norm_ffn, norm_final,
           a_w_in, a_w_o, b_w_in, b_w_a2, b_b_a, b_norm, b_w_o, ffn_w_up, ffn_w_down):
    B, S, D = x_prompt.shape
    Bs, Ts, _ = x_sample.shape
    past = cache_k.shape[2]
    Ms = Bs * Ts
    tq = DSA_Q_TILE
    assert S % tq == 0 and Ms == tq and Ts % SUBLANES == 0

    wu_bf, wd_bf, bwin_bf, bwo_bf, wo_a = _cast_bf16(ffn_w_up, ffn_w_down, b_w_in, b_w_o, a_w_o)
    tm = _pick_tile(S, (1024, 512, 256, 128))
    qT, qiT, wT, vT, k_p, v_p, ki_p, kg, kib = _dsa_project(x_prompt, norm_mix[0], a_w_in[0], tm=tm, tq=tq)
    pos = jnp.arange(S, dtype=I32)
    lim_p = ((pos // STREAM_CHUNK + 1) * STREAM_CHUNK).reshape(S // tq, 1, tq)
    L_p = _round_up(S, DSA_KEY_CHUNK)
    if L_p != S:
        pad = L_p - S
        kg = jnp.pad(kg, ((0, 0), (0, 0), (0, pad), (0, 0)))
        vT = jnp.pad(vT, ((0, 0), (0, 0), (0, 0), (0, pad)))
        kib = jnp.pad(kib, ((0, 0), (0, pad), (0, 0)))
    hp = _dsa_attend(x_prompt, qT, qiT, wT, lim_p, kg, vT, kib, wo_a,
                     topk=min(TOPK_MAX, S // 4), causal=True, shared_queries=False, rows_out=tq)
    xs = x_sample.reshape(1, Ms, D)
    qT_s, qiT_s, wT_s, _, k_s, v_s, ki_s, _, _ = _dsa_project(xs, norm_mix[0], a_w_in[0], tm=Ms, tq=tq)
    k_s = k_s.reshape(Bs, Ts, _NKV)
    v_s = v_s.reshape(Bs, Ts, _NKV)
    ki_s = ki_s.reshape(Bs, Ts, IDX_DIM)
    L_real = past + Ts
    L_s = _round_up(L_real, DSA_KEY_CHUNK)
    kg_s, vT_s, kib_s = _stage_sample_keys(
        cache_k[0].reshape(Bs, past, _NKV), cache_v[0].reshape(Bs, past, _NKV), cache_kidx[0],
        k_s, v_s, ki_s, L_s)
    pos_s = past + jnp.arange(Ts, dtype=I32)
    lim_s = jnp.minimum((pos_s // STREAM_CHUNK + 1) * STREAM_CHUNK, L_real)
    lim_s = jnp.tile(lim_s, Bs).reshape(1, 1, Ms)
    hs = _dsa_attend(xs, qT_s, qiT_s, wT_s, lim_s, kg_s, vT_s, kib_s, wo_a,
                     topk=min(TOPK_MAX, L_real // 4), causal=False, shared_queries=True, rows_out=Ts)

    tmm = _pick_tile(B * S, (1024, 512, 256, 128))
    hp = _mlp(hp.reshape(B * S, D), norm_ffn[0], wu_bf, wd_bf, 0, norm_final, tm=tmm, final_norm=False)
    hs = _mlp(hs.reshape(Ms, D), norm_ffn[0], wu_bf, wd_bf, 0, norm_final, tm=Ms, final_norm=False)

    q, k, v, gate, la = _gla_project(hp, norm_mix[1], bwin_bf, b_w_a2[0], b_b_a[0], tm=tmm)
    chunk = _pick_tile(S, (256, 128, 64))
    r3 = lambda a: a.reshape(B, S, a.shape[-1])
    s0 = jnp.zeros((B, GLA_HEADS, GLA_DK, GLA_DV), F32)
    hp, s_p = _gla(hp.reshape(B, S, D), r3(q), r3(k), r3(v), r3(gate), r3(la), s0, b_norm[0], bwo_bf, chunk=chunk)
    q, k, v, gate, la = _gla_project(hs, norm_mix[1], bwin_bf, b_w_a2[0], b_b_a[0], tm=Ms)
    r3 = lambda a: a.reshape(Bs, Ts, a.shape[-1])
    hs, s_s = _gla(hs.reshape(Bs, Ts, D), r3(q), r3(k), r3(v), r3(gate), r3(la), state_gla[0], b_norm[0], bwo_bf,
                   chunk=Ts)

    y_p = _mlp(hp.reshape(B * S, D), norm_ffn[1], wu_bf, wd_bf, 1, norm_final, tm=tmm, final_norm=True)
    y_s = _mlp(hs.reshape(Ms, D), norm_ffn[1], wu_bf, wd_bf, 1, norm_final, tm=Ms, final_norm=True)

    return (y_p.reshape(B, S, D), y_s.reshape(Bs, Ts, D),
            k_p.reshape(1, B, S, A_KV_HEADS, A_HEAD_DIM), v_p.reshape(1, B, S, A_KV_HEADS, A_HEAD_DIM),
            ki_p.reshape(1, B, S, IDX_DIM), s_p[None],
            k_s.reshape(1, Bs, Ts, A_KV_HEADS, A_HEAD_DIM), v_s.reshape(1, Bs, Ts, A_KV_HEADS, A_HEAD_DIM),
            ki_s.reshape(1, Bs, Ts, IDX_DIM), s_s[None])
```

```python
import functools

import jax
import jax.numpy as jnp
from jax import lax
from jax.experimental import pallas as pl
from jax.experimental.pallas import tpu as pltpu

F32 = jnp.float32
BF16 = jnp.bfloat16
I32 = jnp.int32

EPS = 1e-6
STREAM_CHUNK = 64
TOPK_MAX = 256
A_HEADS = 16
A_KV_HEADS = 4
A_GROUP = A_HEADS // A_KV_HEADS
A_HEAD_DIM = 64
IDX_HEADS = 8
IDX_DIM = 64
GLA_HEADS = 4
GLA_DK = 128
GLA_DV = 256
GLA_RANK = 16
GLA_TAU = 16.0

LANES = 128
SUBLANES = 8
INT_MIN = -(2 ** 31)

DSA_Q_TILE = LANES
DSA_KEY_CHUNK = 256
GLA_DIAG = 16
VMEM_LIMIT = 56 * 1024 * 1024


def _params(*sem):
    return pltpu.CompilerParams(dimension_semantics=sem, vmem_limit_bytes=VMEM_LIMIT)


def _rms(x, g):
    ms = jnp.mean(x * x, axis=-1, keepdims=True)
    return x * lax.rsqrt(ms + EPS) * g


def _dot(a, b):
    return jnp.dot(a, b, preferred_element_type=F32)


def _dot_nt(a, b):
    return lax.dot_general(a, b, (((1,), (1,)), ((), ())), preferred_element_type=F32)


_NQ = A_HEADS * A_HEAD_DIM
_NKV = A_KV_HEADS * A_HEAD_DIM
_NQI = IDX_HEADS * IDX_DIM
BF16_SUBLANES = 16
_T_ROWS = _NQ + _NQI + _NKV + BF16_SUBLANES
V_AUG_ROWS = A_HEAD_DIM + BF16_SUBLANES
LOG2E = 1.4426950408889634
MASKED = -1e30
M_INIT = -1e29


def _dsa_proj_kernel(x_ref, g_ref, wt_ref, wn_ref,
                     qT_ref, qiT_ref, wT_ref, vT_ref, k_ref, v_ref, ki_ref, kg_ref, kib_ref, *, tq):
    xn = _rms(x_ref[0], g_ref[...]).astype(BF16)
    tm = xn.shape[0]
    yT = _dot_nt(wt_ref[...], xn)
    y = _dot(xn, wn_ref[...])
    for j in range(tm // tq):
        sl = slice(j * tq, (j + 1) * tq)
        qT_ref[0, j] = yT[0:_NQ, sl].astype(BF16)
        qiT_ref[0, j] = yT[_NQ:_NQ + _NQI, sl].astype(BF16)
        wT_ref[0, j] = yT[_NQ + _NQI + _NKV:_NQ + _NQI + _NKV + IDX_HEADS, sl]
    for h in range(A_KV_HEADS):
        r0 = _NQ + _NQI + h * A_HEAD_DIM
        vT_ref[0, h, 0:A_HEAD_DIM, :] = yT[r0:r0 + A_HEAD_DIM, :].astype(BF16)
        vT_ref[0, h, A_HEAD_DIM:V_AUG_ROWS, :] = jnp.ones((V_AUG_ROWS - A_HEAD_DIM, tm), BF16)
    k = y[:, 0:_NKV]
    ki = y[:, 2 * _NKV:2 * _NKV + IDX_DIM]
    k_ref[0] = k
    v_ref[0] = y[:, _NKV:2 * _NKV]
    ki_ref[0] = ki
    for h in range(A_KV_HEADS):
        kg_ref[0, h] = k[:, h * A_HEAD_DIM:(h + 1) * A_HEAD_DIM].astype(BF16)
    kib_ref[0] = ki.astype(BF16)


def _dsa_project(x, gain, w_in, *, tm, tq):
    B, S, D = x.shape
    o = 0
    parts = []
    for n in (_NQ, _NKV, _NKV, _NQI, IDX_DIM, IDX_HEADS):
        parts.append(w_in[:, o:o + n])
        o += n
    w_q, w_k, w_v, w_qi, w_ki, w_wt = parts
    wt_all = jnp.concatenate(
        [w_q * (A_HEAD_DIM ** -0.5 * LOG2E), w_qi, w_v, w_wt, jnp.zeros((D, BF16_SUBLANES - IDX_HEADS), F32)],
        axis=1).T.astype(BF16)
    wn_all = jnp.concatenate([w_k, w_v, w_ki], axis=1).astype(BF16)
    nq = S // tq
    nt = S // tm
    jq = tm // tq
    out_shape = (
        jax.ShapeDtypeStruct((B, nq, _NQ, tq), BF16),
        jax.ShapeDtypeStruct((B, nq, _NQI, tq), BF16),
        jax.ShapeDtypeStruct((B, nq, IDX_HEADS, tq), F32),
        jax.ShapeDtypeStruct((B, A_KV_HEADS, V_AUG_ROWS, S), BF16),
        jax.ShapeDtypeStruct((B, S, _NKV), F32),
        jax.ShapeDtypeStruct((B, S, _NKV), F32),
        jax.ShapeDtypeStruct((B, S, IDX_DIM), F32),
        jax.ShapeDtypeStruct((B, A_KV_HEADS, S, A_HEAD_DIM), BF16),
        jax.ShapeDtypeStruct((B, S, IDX_DIM), BF16),
    )
    out_specs = (
        pl.BlockSpec((1, jq, _NQ, tq), lambda b, i: (b, i, 0, 0)),
        pl.BlockSpec((1, jq, _NQI, tq), lambda b, i: (b, i, 0, 0)),
        pl.BlockSpec((1, jq, IDX_HEADS, tq), lambda b, i: (b, i, 0, 0)),
        pl.BlockSpec((1, A_KV_HEADS, V_AUG_ROWS, tm), lambda b, i: (b, 0, 0, i)),
        pl.BlockSpec((1, tm, _NKV), lambda b, i: (b, i, 0)),
        pl.BlockSpec((1, tm, _NKV), lambda b, i: (b, i, 0)),
        pl.BlockSpec((1, tm, IDX_DIM), lambda b, i: (b, i, 0)),
        pl.BlockSpec((1, A_KV_HEADS, tm, A_HEAD_DIM), lambda b, i: (b, 0, i, 0)),
        pl.BlockSpec((1, tm, IDX_DIM), lambda b, i: (b, i, 0)),
    )
    in_specs = [
        pl.BlockSpec((1, tm, D), lambda b, i: (b, i, 0)),
        pl.BlockSpec((1, D), lambda b, i: (0, 0)),
        pl.BlockSpec((_T_ROWS, D), lambda b, i: (0, 0)),
        pl.BlockSpec((D, 2 * _NKV + IDX_DIM), lambda b, i: (0, 0)),
    ]
    return pl.pallas_call(
        functools.partial(_dsa_proj_kernel, tq=tq), name="dsa_project",
        grid=(B, nt), in_specs=in_specs, out_specs=out_specs, out_shape=out_shape,
        compiler_params=_params("arbitrary", "arbitrary"),
    )(x, gain.reshape(1, D), wt_all, wn_all)


def _stage_keys_kernel(ck_ref, cv_ref, cki_ref, nk_ref, nv_ref, nki_ref, kg_ref, vT_ref, kib_ref, kv_scr):
    past = ck_ref.shape[1]
    new = nk_ref.shape[1]
    L = kv_scr.shape[0]

    def gather_rows(cache_ref, new_ref, dst_ref, cast):
        dst_ref[0:past, :] = cast(cache_ref[0])
        dst_ref[past:past + new, :] = cast(new_ref[0])
        if past + new < L:
            dst_ref[past + new:L, :] = jnp.zeros((L - past - new, dst_ref.shape[-1]), dst_ref.dtype)

    gather_rows(ck_ref, nk_ref, kv_scr, lambda a: a)
    k_all = kv_scr[...]
    for h in range(A_KV_HEADS):
        kg_ref[0, h] = k_all[:, h * A_HEAD_DIM:(h + 1) * A_HEAD_DIM].astype(BF16)
    gather_rows(cv_ref, nv_ref, kv_scr, lambda a: a)
    v_t = jnp.transpose(kv_scr[...])
    for h in range(A_KV_HEADS):
        vT_ref[0, h, 0:A_HEAD_DIM, :] = v_t[h * A_HEAD_DIM:(h + 1) * A_HEAD_DIM, :].astype(BF16)
        vT_ref[0, h, A_HEAD_DIM:V_AUG_ROWS, :] = jnp.ones((V_AUG_ROWS - A_HEAD_DIM, L), BF16)
    gather_rows(cki_ref, nki_ref, kib_ref.at[0], lambda a: a.astype(BF16))


def _stage_sample_keys(cache_k, cache_v, cache_ki, new_k, new_v, new_ki, L):
    Bs, past, _ = cache_k.shape
    new = new_k.shape[1]
    row = lambda b: (b, 0, 0)
    return pl.pallas_call(
        _stage_keys_kernel, grid=(Bs,), name="stage_sample_keys",
        in_specs=[
            pl.BlockSpec((1, past, _NKV), row), pl.BlockSpec((1, past, _NKV), row),
            pl.BlockSpec((1, past, IDX_DIM), row),
            pl.BlockSpec((1, new, _NKV), row), pl.BlockSpec((1, new, _NKV), row),
            pl.BlockSpec((1, new, IDX_DIM), row),
        ],
        out_specs=(
            pl.BlockSpec((1, A_KV_HEADS, L, A_HEAD_DIM), lambda b: (b, 0, 0, 0)),
            pl.BlockSpec((1, A_KV_HEADS, V_AUG_ROWS, L), lambda b: (b, 0, 0, 0)),
            pl.BlockSpec((1, L, IDX_DIM), row),
        ),
        out_shape=(
            jax.ShapeDtypeStruct((Bs, A_KV_HEADS, L, A_HEAD_DIM), BF16),
            jax.ShapeDtypeStruct((Bs, A_KV_HEADS, V_AUG_ROWS, L), BF16),
            jax.ShapeDtypeStruct((Bs, L, IDX_DIM), BF16),
        ),
        scratch_shapes=[pltpu.VMEM((L, _NKV), F32)],
        compiler_params=_params("arbitrary"),
    )(cache_k, cache_v, cache_ki, new_k, new_v, new_ki)


def _dsa_attn_kernel(lim_ref, x_ref, qT_ref, qiT_ref, wT_ref, kg_ref, vT_ref, kib_ref, wo_ref,
                     out_ref, key_scr, tie_scr, oT_scr, res_scr, acc_scr, bias_scr, sa_scr, sb_scr,
                     ia_scr, ib_scr,
                     *, topk, causal, n_chunks_total, rows_out):
    tq = qT_ref.shape[-1]
    lc = DSA_KEY_CHUNK
    if causal:
        n_chunks = jnp.minimum(((pl.program_id(1) + 1) * tq + lc - 1) // lc, n_chunks_total)
    else:
        n_chunks = jnp.int32(n_chunks_total)
    n_pairs = n_chunks // 2
    odd = (n_chunks & 1) == 1
    last = n_chunks - 1
    lim = lim_ref[0]

    def chunk_start(c):
        return pl.multiple_of(c * lc, lc)

    def key_index(c):
        return c * lc + lax.broadcasted_iota(I32, (lc, tq), 0)

    qi_all = jnp.concatenate(
        [qiT_ref[0, 0, h * IDX_DIM:(h + 1) * IDX_DIM, :] for h in range(IDX_HEADS)], axis=1)
    w_rows = wT_ref[0, 0] * ((IDX_DIM ** -0.5) * (IDX_HEADS ** -0.5))

    def idx_logits(c):
        return _dot(kib_ref[0, pl.ds(chunk_start(c), lc), :], qi_all)

    def store_keys(s_ref, c):
        off = chunk_start(c)
        score = jnp.maximum(s_ref[:, 0:tq], 0.0) * w_rows[0:1, :]
        for h in range(1, IDX_HEADS):
            score = score + jnp.maximum(s_ref[:, h * tq:(h + 1) * tq], 0.0) * w_rows[h:h + 1, :]
        key_scr[pl.ds(off, lc), :] = jnp.where(key_index(c) < lim, score, -jnp.inf)

    ia_scr[...] = idx_logits(0)

    def score_body(i, carry):
        c0 = 2 * i
        ib_scr[...] = idx_logits(c0 + 1)
        store_keys(ia_scr, c0)
        ia_scr[...] = idx_logits(jnp.minimum(c0 + 2, last))
        store_keys(ib_scr, c0 + 1)
        return carry

    lax.fori_loop(0, n_pairs, score_body, 0)

    @pl.when(odd)
    def _():
        store_keys(ia_scr, last)

    def count(pred):
        def chunk_flags(c):
            kk = key_scr[pl.ds(chunk_start(c), lc), :]
            m = jnp.where(pred(kk, c), 1, 0).astype(I32)
            return [m[r * SUBLANES:(r + 1) * SUBLANES, :] for r in range(lc // SUBLANES)]

        def tree_sum(flags):
            while len(flags) > 1:
                flags = [a + b for a, b in zip(flags[0::2], flags[1::2])]
            return flags[0]

        def span(first, n):
            flags = []
            for j in range(n):
                flags += chunk_flags(first + j)
            return tree_sum(flags)

        zero = jnp.zeros((SUBLANES, tq), I32)
        acc = lax.fori_loop(0, n_pairs, lambda i, a: a + span(2 * i, 2), zero)
        acc = acc + lax.cond(odd, lambda: span(last, 1), lambda: zero)
        return jnp.sum(acc, axis=0, keepdims=True)

    def code_to_float(code):
        bits = jnp.where(code < 0, (-code) | INT_MIN, code)
        return pltpu.bitcast(bits, F32)

    def bit_body(i, carry):
        code, cge = carry
        cand = code + jnp.left_shift(jnp.int32(1), 31 - i)
        cand_f = code_to_float(cand)
        c = count(lambda kk, _: kk >= cand_f)
        c = c - jnp.where(cand_f == -jnp.inf, n_inadmissible, 0)
        ok = c >= topk
        return jnp.where(ok, cand, code), jnp.where(ok, c, cge)

    code0 = jnp.full((1, tq), INT_MIN, I32)
    cge0 = jnp.zeros((1, tq), I32) + n_chunks * lc
    n_inadmissible = n_chunks * lc - lim
    code, cge = lax.fori_loop(0, 32, bit_body, (code0, cge0))
    thr = jnp.where(code > INT_MIN, code_to_float(code), -jnp.inf)

    n_idx_bits = max(1, (n_chunks_total * lc - 1).bit_length())
    tie_scr[...] = jnp.broadcast_to(jnp.where(thr == -jnp.inf, lim - 1, 2 ** 30), (SUBLANES, tq))
    surplus = jnp.logical_and(cge > topk, code > INT_MIN)

    @pl.when(jnp.max(jnp.where(surplus, 1, 0)) > 0)
    def _():
        cgt = count(lambda kk, _: kk > thr)
        want = topk - cgt
        bound = jnp.zeros((1, tq), I32)
        for bit in range(n_idx_bits - 1, -1, -1):
            cand = bound + (1 << bit)
            c = count(lambda kk, cc: jnp.logical_and(kk == thr, key_index(cc) < cand))
            bound = jnp.where(c < want, cand, bound)
        tie_scr[...] = jnp.broadcast_to(bound, (SUBLANES, tq))

    tie_bound = tie_scr[0:1, :]

    def bias_body(c, carry):
        off = chunk_start(c)
        kk = key_scr[pl.ds(off, lc), :]
        tie = jnp.where(key_index(c) <= tie_bound, 0.0, MASKED)
        b = jnp.where(kk > thr, 0.0, jnp.where(kk == thr, tie, MASKED))
        bias_scr[pl.ds(off, lc), :] = b.astype(BF16)
        return carry

    lax.fori_loop(0, n_chunks, bias_body, 0)

    gw = A_GROUP * tq
    eye = jnp.where(lax.broadcasted_iota(I32, (tq, gw), 0) == (lax.broadcasted_iota(I32, (tq, gw), 1) & (tq - 1)),
                    1.0, 0.0).astype(BF16)
    q_aug = [jnp.concatenate(
        [eye, jnp.concatenate(
            [qT_ref[0, 0, (g * A_GROUP + h) * A_HEAD_DIM:(g * A_GROUP + h + 1) * A_HEAD_DIM, :]
             for h in range(A_GROUP)], axis=1)], axis=0) for g in range(A_KV_HEADS)]
    acc_scr[...] = jnp.zeros(acc_scr.shape, F32)

    def logits(c, g):
        off = chunk_start(c)
        k_aug = jnp.concatenate([bias_scr[pl.ds(off, lc), :], kg_ref[0, g, pl.ds(off, lc), :]], axis=1)
        return _dot(k_aug, q_aug[g])

    def softmax_pv(s, c, g, m):
        mn = jnp.maximum(m, jnp.max(s, axis=0, keepdims=True))
        alpha = jnp.exp2(m - mn)
        p = jnp.exp2(s - mn).astype(BF16)
        vt = vT_ref[0, g, :, pl.ds(chunk_start(c), lc)]
        acc_scr[g] = alpha * acc_scr[g] + _dot(vt, p)
        return mn

    for g in range(A_KV_HEADS):
        sa_scr[g] = logits(0, g)

    def att_body(i, ms):
        c0 = 2 * i
        ms = list(ms)
        for g in range(A_KV_HEADS):
            sb_scr[g] = logits(c0 + 1, g)
            ms[g] = softmax_pv(sa_scr[g], c0, g, ms[g])
        c2 = jnp.minimum(c0 + 2, last)
        for g in range(A_KV_HEADS):
            sa_scr[g] = logits(c2, g)
            ms[g] = softmax_pv(sb_scr[g], c0 + 1, g, ms[g])
        return tuple(ms)

    m0 = tuple(jnp.full((1, gw), M_INIT, F32) for _ in range(A_KV_HEADS))
    ms = lax.fori_loop(0, n_pairs, att_body, m0)

    @pl.when(odd)
    def _():
        for g in range(A_KV_HEADS):
            softmax_pv(sa_scr[g], last, g, ms[g])

    for g in range(A_KV_HEADS):
        acc = acc_scr[g]
        o = acc[0:A_HEAD_DIM, :] / acc[A_HEAD_DIM:A_HEAD_DIM + 1, :]
        for h in range(A_GROUP):
            r0 = (g * A_GROUP + h) * A_HEAD_DIM
            oT_scr[r0:r0 + A_HEAD_DIM, :] = o[:, h * tq:(h + 1) * tq]

    o = jnp.transpose(oT_scr[...]).astype(BF16)
    res = x_ref[0] + _dot(o, wo_ref[...])
    if rows_out == tq:
        out_ref[0] = res
    else:
        res_scr[...] = res
        r0 = pl.multiple_of(pl.program_id(0) * rows_out, rows_out)
        out_ref[0] = res_scr[pl.ds(r0, rows_out), :]


def _dsa_attend(x, qT, qiT, wT, lim, kg, vT, kib, wo, *, topk, causal, shared_queries, rows_out):
    Bk, _, L, _ = kg.shape
    D = x.shape[-1]
    tq = qT.shape[-1]
    nq = qT.shape[1]
    assert L % DSA_KEY_CHUNK == 0
    if shared_queries:
        grid = (Bk, 1)
        qmap = lambda b, j: (0, 0, 0, 0)
        xmap = lambda b, j: (0, 0, 0)
        omap = lambda b, j: (0, b, 0)
        lmap = lambda b, j: (0, 0, 0)
    else:
        grid = (Bk, nq)
        qmap = lambda b, j: (b, j, 0, 0)
        xmap = lambda b, j: (b, j, 0)
        omap = xmap
        lmap = lambda b, j: (j, 0, 0)
    in_specs = [
        pl.BlockSpec((1, 1, tq), lmap),
        pl.BlockSpec((1, tq, D), xmap),
        pl.BlockSpec((1, 1, _NQ, tq), qmap),
        pl.BlockSpec((1, 1, _NQI, tq), qmap),
        pl.BlockSpec((1, 1, IDX_HEADS, tq), qmap),
        pl.BlockSpec((1, A_KV_HEADS, L, A_HEAD_DIM), lambda b, j: (b, 0, 0, 0)),
        pl.BlockSpec((1, A_KV_HEADS, V_AUG_ROWS, L), lambda b, j: (b, 0, 0, 0)),
        pl.BlockSpec((1, L, IDX_DIM), lambda b, j: (b, 0, 0)),
        pl.BlockSpec((None, _NQ, D), lambda b, j: (0, 0, 0)),
    ]
    kern = functools.partial(_dsa_attn_kernel, topk=topk, causal=causal,
                             n_chunks_total=L // DSA_KEY_CHUNK, rows_out=rows_out)
    return pl.pallas_call(
        kern, grid=grid, in_specs=in_specs, name="dsa_attend",
        out_specs=pl.BlockSpec((1, rows_out, D), omap),
        out_shape=jax.ShapeDtypeStruct(x.shape, F32),
        scratch_shapes=[pltpu.VMEM((L, tq), F32), pltpu.VMEM((SUBLANES, tq), I32),
                        pltpu.VMEM((_NQ, tq), F32), pltpu.VMEM((tq, D), F32),
                        pltpu.VMEM((A_KV_HEADS, V_AUG_ROWS, A_GROUP * tq), F32),
                        pltpu.VMEM((L, tq), BF16),
                        pltpu.VMEM((A_KV_HEADS, DSA_KEY_CHUNK, A_GROUP * tq), F32),
                        pltpu.VMEM((A_KV_HEADS, DSA_KEY_CHUNK, A_GROUP * tq), F32),
                        pltpu.VMEM((DSA_KEY_CHUNK, IDX_HEADS * tq), F32),
                        pltpu.VMEM((DSA_KEY_CHUNK, IDX_HEADS * tq), F32)],
        compiler_params=_params("arbitrary", "arbitrary"),
    )(lim, x, qT, qiT, wT, kg, vT, kib, wo)


CAST_STEPS = 8


def _cast_kernel(*refs):
    n = len(refs) // 2
    for src, dst in zip(refs[:n], refs[n:]):
        dst[...] = src[...].astype(BF16)


def _cast_bf16(*mats):
    specs = [pl.BlockSpec((m.shape[0], m.shape[1] // CAST_STEPS, m.shape[2]), lambda i: (0, i, 0)) for m in mats]
    return pl.pallas_call(
        _cast_kernel, grid=(CAST_STEPS,), name="cast_bf16",
        in_specs=specs, out_specs=tuple(specs),
        out_shape=tuple(jax.ShapeDtypeStruct(m.shape, BF16) for m in mats),
        compiler_params=_params("arbitrary"),
    )(*mats)


MLP_FF_CHUNK = 1024


def _mlp_kernel(x_ref, g_ref, wu_ref, wd_ref, gf_ref, out_ref, *, final_norm):
    x = x_ref[...]
    xn = _rms(x, g_ref[...]).astype(BF16)
    acc = x
    for f in range(0, wu_ref.shape[1], MLP_FF_CHUNK):
        h = jnp.maximum(_dot(xn, wu_ref[:, f:f + MLP_FF_CHUNK]), 0.0)
        acc = acc + _dot((h * h).astype(BF16), wd_ref[f:f + MLP_FF_CHUNK, :])
    if final_norm:
        acc = _rms(acc, gf_ref[...])
    out_ref[...] = acc


def _mlp(x, gain, w_up, w_down, layer, gain_final, *, tm, final_norm):
    M, D = x.shape
    FF = w_up.shape[2]
    const = lambda i: (0, 0)
    pick = lambda i: (layer, 0, 0)
    return pl.pallas_call(
        functools.partial(_mlp_kernel, final_norm=final_norm), name="mlp",
        grid=(M // tm,),
        in_specs=[
            pl.BlockSpec((tm, D), lambda i: (i, 0)),
            pl.BlockSpec((1, D), const),
            pl.BlockSpec((None, D, FF), pick, pipeline_mode=pl.Buffered(1)),
            pl.BlockSpec((None, FF, D), pick, pipeline_mode=pl.Buffered(1)),
            pl.BlockSpec((1, D), const),
        ],
        out_specs=pl.BlockSpec((tm, D), lambda i: (i, 0)),
        out_shape=jax.ShapeDtypeStruct((M, D), F32),
        compiler_params=_params("arbitrary"),
    )(x, gain.reshape(1, D), w_up, w_down, gain_final.reshape(1, D))


_GQK = GLA_HEADS * GLA_DK
_GV = GLA_HEADS * GLA_DV


def _gla_proj_kernel(x_ref, g_ref, w_ref, wa_ref, ba_ref, q_ref, k_ref, v_ref, gate_ref, la_ref):
    xn = _rms(x_ref[...], g_ref[...]).astype(BF16)
    y = _dot(xn, w_ref[...])
    D = gate_ref.shape[-1]
    q_ref[...] = y[:, 0:_GQK]
    k_ref[...] = y[:, _GQK:2 * _GQK]
    v_ref[...] = y[:, 2 * _GQK:2 * _GQK + _GV].astype(BF16)
    o = 2 * _GQK + _GV
    gate_ref[...] = y[:, o:o + D]
    a = y[:, o + D:o + D + GLA_RANK].astype(BF16)
    z = _dot(a, wa_ref[...]) + ba_ref[...]
    la_ref[...] = (jnp.minimum(z, 0.0) - jnp.log1p(jnp.exp(-jnp.abs(z)))) * (1.0 / GLA_TAU)


def _gla_project(x, gain, w_in, w_a2, b_a, *, tm):
    M, D = x.shape
    N = w_in.shape[2]
    const = lambda i: (0, 0)
    row = lambda i: (i, 0)
    return pl.pallas_call(
        _gla_proj_kernel, grid=(M // tm,), name="gla_project",
        in_specs=[
            pl.BlockSpec((tm, D), row),
            pl.BlockSpec((1, D), const),
            pl.BlockSpec((None, D, N), lambda i: (0, 0, 0)),
            pl.BlockSpec((GLA_RANK, _GQK), const),
            pl.BlockSpec((1, _GQK), const),
        ],
        out_specs=(
            pl.BlockSpec((tm, _GQK), row), pl.BlockSpec((tm, _GQK), row), pl.BlockSpec((tm, _GV), row),
            pl.BlockSpec((tm, D), row), pl.BlockSpec((tm, _GQK), row)),
        out_shape=(
            jax.ShapeDtypeStruct((M, _GQK), F32), jax.ShapeDtypeStruct((M, _GQK), F32),
            jax.ShapeDtypeStruct((M, _GV), BF16), jax.ShapeDtypeStruct((M, D), F32),
            jax.ShapeDtypeStruct((M, _GQK), F32)),
        compiler_params=_params("arbitrary"),
    )(x, gain.reshape(1, D), w_in, w_a2.astype(BF16), b_a.reshape(1, _GQK))


def _split3(a):
    a0 = a.astype(BF16)
    r = a - a0.astype(F32)
    a1 = r.astype(BF16)
    a2 = (r - a1.astype(F32)).astype(BF16)
    return a0, a1, a2


def _gla_kernel(x_ref, q_ref, k_ref, v_ref, gate_ref, la_ref, s0_ref, ng_ref, wo_ref,
                out_ref, sout_ref, s_scr, a_scr, o_scr):
    c = pl.program_id(1)
    C = q_ref.shape[1]

    @pl.when(c == 0)
    def _():
        s_scr[...] = s0_ref[0]

    la = la_ref[0]
    row = lax.broadcasted_iota(I32, (C, C), 0)
    col = lax.broadcasted_iota(I32, (C, C), 1)
    tri = jnp.where(row >= col, 1.0, 0.0).astype(BF16)
    l0, l1, l2 = _split3(la)
    b = _dot(tri, l0) + _dot(tri, l1) + _dot(tri, l2)
    q = q_ref[0] * (GLA_DK ** -0.5)
    k = k_ref[0]
    trow = lax.broadcasted_iota(I32, (C, 1), 0)

    def block_ref(values, size, pick):
        v3 = values.reshape(C // size, size, values.shape[-1])
        return jnp.broadcast_to(v3[:, pick:pick + 1, :], v3.shape).reshape(values.shape)

    def accumulate(qh, kh, mask, first):
        for h in range(GLA_HEADS):
            sl = slice(h * GLA_DK, (h + 1) * GLA_DK)
            blk = _dot_nt(qh[:, sl], kh[:, sl])
            if mask is not None:
                blk = jnp.where(mask, blk, 0.0)
            if first:
                a_scr[h] = blk
            else:
                a_scr[h] = a_scr[h] + blk

    d = min(GLA_DIAG, C)
    before = block_ref(b - la, d, 0)
    qh = (q * jnp.exp(b - before)).astype(BF16)
    kh = (k * jnp.exp(before - b)).astype(BF16)
    sh = d.bit_length() - 1
    accumulate(qh, kh, jnp.logical_and((row >> sh) == (col >> sh), row >= col), True)
    half = d
    while half < C:
        upper = (trow & (2 * half - 1)) >= half
        split = block_ref(b, 2 * half, half - 1)
        qh = (q * jnp.exp(jnp.where(upper, b - split, -jnp.inf))).astype(BF16)
        kh = (k * jnp.exp(jnp.where(upper, -jnp.inf, split - b))).astype(BF16)
        sh = (2 * half).bit_length() - 1
        accumulate(qh, kh, (row >> sh) == (col >> sh) if 2 * half < C else None, False)
        half *= 2

    b_end = b[C - 1:C, :]
    q_in = (q * jnp.exp(b)).astype(BF16)
    k_out = k * jnp.exp(b_end - b)
    ng = ng_ref[...]
    for h in range(GLA_HEADS):
        sl = slice(h * GLA_DK, (h + 1) * GLA_DK)
        vh = v_ref[0, :, h * GLA_DV:(h + 1) * GLA_DV]
        s_old = s_scr[h]
        o = _dot(q_in[:, sl], s_old.astype(BF16)) + _dot(a_scr[h].astype(BF16), vh)
        dec = jnp.transpose(jnp.broadcast_to(jnp.exp(b_end[:, sl]), (GLA_DK, GLA_DK)))
        dec = jnp.concatenate([dec] * (GLA_DV // GLA_DK), axis=1)
        s_scr[h] = dec * s_old + _dot(jnp.transpose(k_out[:, sl]).astype(BF16), vh)
        o = _rms(o, ng)
        gt = gate_ref[0, :, h * GLA_DV:(h + 1) * GLA_DV]
        o_scr[:, h * GLA_DV:(h + 1) * GLA_DV] = (o * (gt / (1.0 + jnp.exp(-gt)))).astype(BF16)

    out_ref[0] = x_ref[0] + _dot(o_scr[...], wo_ref[...])

    @pl.when(c == pl.num_programs(1) - 1)
    def _():
        sout_ref[0] = s_scr[...]


def _gla(x, q, k, v, gate, la, s0, norm_g, w_o, *, chunk):
    B, T, D = x.shape
    n = T // chunk
    tok = lambda b, c: (b, c, 0)
    const = lambda b, c: (0, 0)
    st = lambda b, c: (b, 0, 0, 0)
    return pl.pallas_call(
        _gla_kernel, grid=(B, n), name="gla",
        in_specs=[
            pl.BlockSpec((1, chunk, D), tok),
            pl.BlockSpec((1, chunk, _GQK), tok),
            pl.BlockSpec((1, chunk, _GQK), tok),
            pl.BlockSpec((1, chunk, _GV), tok),
            pl.BlockSpec((1, chunk, D), tok),
            pl.BlockSpec((1, chunk, _GQK), tok),
            pl.BlockSpec((1, GLA_HEADS, GLA_DK, GLA_DV), st),
            pl.BlockSpec((1, GLA_DV), const),
            pl.BlockSpec((None, _GV, D), lambda b, c: (0, 0, 0)),
        ],
        out_specs=(pl.BlockSpec((1, chunk, D), tok), pl.BlockSpec((1, GLA_HEADS, GLA_DK, GLA_DV), st)),
        out_shape=(jax.ShapeDtypeStruct((B, T, D), F32),
                   jax.ShapeDtypeStruct((B, GLA_HEADS, GLA_DK, GLA_DV), F32)),
        scratch_shapes=[pltpu.VMEM((GLA_HEADS, GLA_DK, GLA_DV), F32),
                        pltpu.VMEM((GLA_HEADS, chunk, chunk), F32),
                        pltpu.VMEM((chunk, _GV), BF16)],
        compiler_params=_params("arbitrary", "arbitrary"),
    )(x, q, k, v, gate, la, s0, norm_g.reshape(1, GLA_DV), w_o)


def _pick_tile(n, candidates):
    for t in candidates:
        if n % t == 0:
            return t
    return n


def _round_up(n, m):
    return (n + m - 1) // m * m


def kernel(x_prompt, x_sample, cache_k, cache_v, cache_kidx, state_gla, norm_mix, norm_ffn, norm_final,
           a_w_in, a_w_o, b_w_in, b_w_a2, b_b_a, b_norm, b_w_o, ffn_w_up, ffn_w_down):
    B, S, D = x_prompt.shape
    Bs, Ts, _ = x_sample.shape
    past = cache_k.shape[2]
    Ms = Bs * Ts
    tq = DSA_Q_TILE
    assert S % tq == 0 and Ms == tq and Ts % SUBLANES == 0

    wu_bf, wd_bf, bwin_bf, bwo_bf, wo_a = _cast_bf16(ffn_w_up, ffn_w_down, b_w_in, b_w_o, a_w_o)
    tm = _pick_tile(S, (1024, 512, 256, 128))
    qT, qiT, wT, vT, k_p, v_p, ki_p, kg, kib = _dsa_project(x_prompt, norm_mix[0], a_w_in[0], tm=tm, tq=tq)
    pos = jnp.arange(S, dtype=I32)
    lim_p = ((pos // STREAM_CHUNK + 1) * STREAM_CHUNK).reshape(S // tq, 1, tq)
    L_p = _round_up(S, DSA_KEY_CHUNK)
    if L_p != S:
        pad = L_p - S
        kg = jnp.pad(kg, ((0, 0), (0, 0), (0, pad), (0, 0)))
        vT = jnp.pad(vT, ((0, 0), (0, 0), (0, 0), (0, pad)))
        kib = jnp.pad(kib, ((0, 0), (0, pad), (0, 0)))
    hp = _dsa_attend(x_prompt, qT, qiT, wT, lim_p, kg, vT, kib, wo_a,
                     topk=min(TOPK_MAX, S // 4), causal=True, shared_queries=False, rows_out=tq)
    xs = x_sample.reshape(1, Ms, D)
    qT_s, qiT_s, wT_s, _, k_s, v_s, ki_s, _, _ = _dsa_project(xs, norm_mix[0], a_w_in[0], tm=Ms, tq=tq)
    k_s = k_s.reshape(Bs, Ts, _NKV)
    v_s = v_s.reshape(Bs, Ts, _NKV)
    ki_s = ki_s.reshape(Bs, Ts, IDX_DIM)
    L_real = past + Ts
    L_s = _round_up(L_real, DSA_KEY_CHUNK)
    kg_s, vT_s, kib_s = _stage_sample_keys(
        cache_k[0].reshape(Bs, past, _NKV), cache_v[0].reshape(Bs, past, _NKV), cache_kidx[0],
        k_s, v_s, ki_s, L_s)
    pos_s = past + jnp.arange(Ts, dtype=I32)
    lim_s = jnp.minimum((pos_s // STREAM_CHUNK + 1) * STREAM_CHUNK, L_real)
    lim_s = jnp.tile(lim_s, Bs).reshape(1, 1, Ms)
    hs = _dsa_attend(xs, qT_s, qiT_s, wT_s, lim_s, kg_s, vT_s, kib_s, wo_a,
                     topk=min(TOPK_MAX, L_real // 4), causal=False, shared_queries=True, rows_out=Ts)

    tmm = _pick_tile(B * S, (1024, 512, 256, 128))
    hp = _mlp(hp.reshape(B * S, D), norm_ffn[0], wu_bf, wd_bf, 0, norm_final, tm=tmm, final_norm=False)
    hs = _mlp(hs.reshape(Ms, D), norm_ffn[0], wu_bf, wd_bf, 0, norm_final, tm=Ms, final_norm=False)

    q, k, v, gate, la = _gla_project(hp, norm_mix[1], bwin_bf, b_w_a2[0], b_b_a[0], tm=tmm)
    chunk = _pick_tile(S, (256, 128, 64))
    r3 = lambda a: a.reshape(B, S, a.shape[-1])
    s0 = jnp.zeros((B, GLA_HEADS, GLA_DK, GLA_DV), F32)
    hp, s_p = _gla(hp.reshape(B, S, D), r3(q), r3(k), r3(v), r3(gate), r3(la), s0, b_norm[0], bwo_bf, chunk=chunk)
    q, k, v, gate, la = _gla_project(hs, norm_mix[1], bwin_bf, b_w_a2[0], b_b_a[0], tm=Ms)
    r3 = lambda a: a.reshape(Bs, Ts, a.shape[-1])
    hs, s_s = _gla(hs.reshape(Bs, Ts, D), r3(q), r3(k), r3(v), r3(gate), r3(la), state_gla[0], b_norm[0], bwo_bf,
                   chunk=Ts)

    y_p = _mlp(hp.reshape(B * S, D), norm_ffn[1], wu_bf, wd_bf, 1, norm_final, tm=tmm, final_norm=True)
    y_s = _mlp(hs.reshape(Ms, D), norm_ffn[1], wu_bf, wd_bf, 1, norm_final, tm=Ms, final_norm=True)

    return (y_p.reshape(B, S, D), y_s.reshape(Bs, Ts, D),
            k_p.reshape(1, B, S, A_KV_HEADS, A_HEAD_DIM), v_p.reshape(1, B, S, A_KV_HEADS, A_HEAD_DIM),
            ki_p.reshape(1, B, S, IDX_DIM), s_p[None],
            k_s.reshape(1, Bs, Ts, A_KV_HEADS, A_HEAD_DIM), v_s.reshape(1, Bs, Ts, A_KV_HEADS, A_HEAD_DIM),
            ki_s.reshape(1, Bs, Ts, IDX_DIM), s_s[None])
```

```python
import functools

import jax
import jax.numpy as jnp
from jax import lax
from jax.experimental import pallas as pl
from jax.experimental.pallas import tpu as pltpu

F32 = jnp.float32
BF16 = jnp.bfloat16
I32 = jnp.int32

EPS = 1e-6
STREAM_CHUNK = 64
TOPK_MAX = 256
A_HEADS = 16
A_KV_HEADS = 4
A_GROUP = A_HEADS // A_KV_HEADS
A_HEAD_DIM = 64
IDX_HEADS = 8
IDX_DIM = 64
GLA_HEADS = 4
GLA_DK = 128
GLA_DV = 256
GLA_RANK = 16
GLA_TAU = 16.0

LANES = 128
SUBLANES = 8
INT_MIN = -(2 ** 31)

DSA_Q_TILE = LANES
DSA_KEY_CHUNK = 256
SEARCH_CHECKPOINTS = (22, 25, 28)
GLA_DIAG = 16
VMEM_LIMIT = 56 * 1024 * 1024


def _params(*sem):
    return pltpu.CompilerParams(dimension_semantics=sem, vmem_limit_bytes=VMEM_LIMIT)


def _rms(x, g):
    ms = jnp.mean(x * x, axis=-1, keepdims=True)
    return x * lax.rsqrt(ms + EPS) * g


def _dot(a, b):
    return jnp.dot(a, b, preferred_element_type=F32)


def _dot_nt(a, b):
    return lax.dot_general(a, b, (((1,), (1,)), ((), ())), preferred_element_type=F32)


_NQ = A_HEADS * A_HEAD_DIM
_NKV = A_KV_HEADS * A_HEAD_DIM
_NQI = IDX_HEADS * IDX_DIM
BF16_SUBLANES = 16
_T_ROWS = _NQ + _NQI + _NKV + BF16_SUBLANES
V_AUG_ROWS = A_HEAD_DIM + BF16_SUBLANES
LOG2E = 1.4426950408889634
MASKED = -1e30
M_INIT = -1e29


def _dsa_proj_kernel(x_ref, g_ref, wt_ref, wn_ref,
                     qT_ref, qiT_ref, wT_ref, vT_ref, k_ref, v_ref, ki_ref, kg_ref, kib_ref, *, tq):
    xn = _rms(x_ref[0], g_ref[...]).astype(BF16)
    tm = xn.shape[0]
    yT = _dot_nt(wt_ref[...], xn)
    y = _dot(xn, wn_ref[...])
    for j in range(tm // tq):
        sl = slice(j * tq, (j + 1) * tq)
        qT_ref[0, j] = yT[0:_NQ, sl].astype(BF16)
        qiT_ref[0, j] = yT[_NQ:_NQ + _NQI, sl].astype(BF16)
        wT_ref[0, j] = yT[_NQ + _NQI + _NKV:_NQ + _NQI + _NKV + IDX_HEADS, sl]
    for h in range(A_KV_HEADS):
        r0 = _NQ + _NQI + h * A_HEAD_DIM
        vT_ref[0, h, 0:A_HEAD_DIM, :] = yT[r0:r0 + A_HEAD_DIM, :].astype(BF16)
        vT_ref[0, h, A_HEAD_DIM:V_AUG_ROWS, :] = jnp.ones((V_AUG_ROWS - A_HEAD_DIM, tm), BF16)
    k = y[:, 0:_NKV]
    ki = y[:, 2 * _NKV:2 * _NKV + IDX_DIM]
    k_ref[0] = k
    v_ref[0] = y[:, _NKV:2 * _NKV]
    ki_ref[0] = ki
    for h in range(A_KV_HEADS):
        kg_ref[0, h] = k[:, h * A_HEAD_DIM:(h + 1) * A_HEAD_DIM].astype(BF16)
    kib_ref[0] = ki.astype(BF16)


def _dsa_project(x, gain, w_in, *, tm, tq):
    B, S, D = x.shape
    o = 0
    parts = []
    for n in (_NQ, _NKV, _NKV, _NQI, IDX_DIM, IDX_HEADS):
        parts.append(w_in[:, o:o + n])
        o += n
    w_q, w_k, w_v, w_qi, w_ki, w_wt = parts
    wt_all = jnp.concatenate(
        [w_q * (A_HEAD_DIM ** -0.5 * LOG2E), w_qi, w_v, w_wt, jnp.zeros((D, BF16_SUBLANES - IDX_HEADS), F32)],
        axis=1).T.astype(BF16)
    wn_all = jnp.concatenate([w_k, w_v, w_ki], axis=1).astype(BF16)
    nq = S // tq
    nt = S // tm
    jq = tm // tq
    out_shape = (
        jax.ShapeDtypeStruct((B, nq, _NQ, tq), BF16),
        jax.ShapeDtypeStruct((B, nq, _NQI, tq), BF16),
        jax.ShapeDtypeStruct((B, nq, IDX_HEADS, tq), F32),
        jax.ShapeDtypeStruct((B, A_KV_HEADS, V_AUG_ROWS, S), BF16),
        jax.ShapeDtypeStruct((B, S, _NKV), F32),
        jax.ShapeDtypeStruct((B, S, _NKV), F32),
        jax.ShapeDtypeStruct((B, S, IDX_DIM), F32),
        jax.ShapeDtypeStruct((B, A_KV_HEADS, S, A_HEAD_DIM), BF16),
        jax.ShapeDtypeStruct((B, S, IDX_DIM), BF16),
    )
    out_specs = (
        pl.BlockSpec((1, jq, _NQ, tq), lambda b, i: (b, i, 0, 0)),
        pl.BlockSpec((1, jq, _NQI, tq), lambda b, i: (b, i, 0, 0)),
        pl.BlockSpec((1, jq, IDX_HEADS, tq), lambda b, i: (b, i, 0, 0)),
        pl.BlockSpec((1, A_KV_HEADS, V_AUG_ROWS, tm), lambda b, i: (b, 0, 0, i)),
        pl.BlockSpec((1, tm, _NKV), lambda b, i: (b, i, 0)),
        pl.BlockSpec((1, tm, _NKV), lambda b, i: (b, i, 0)),
        pl.BlockSpec((1, tm, IDX_DIM), lambda b, i: (b, i, 0)),
        pl.BlockSpec((1, A_KV_HEADS, tm, A_HEAD_DIM), lambda b, i: (b, 0, i, 0)),
        pl.BlockSpec((1, tm, IDX_DIM), lambda b, i: (b, i, 0)),
    )
    in_specs = [
        pl.BlockSpec((1, tm, D), lambda b, i: (b, i, 0)),
        pl.BlockSpec((1, D), lambda b, i: (0, 0)),
        pl.BlockSpec((_T_ROWS, D), lambda b, i: (0, 0)),
        pl.BlockSpec((D, 2 * _NKV + IDX_DIM), lambda b, i: (0, 0)),
    ]
    return pl.pallas_call(
        functools.partial(_dsa_proj_kernel, tq=tq), name="dsa_project",
        grid=(B, nt), in_specs=in_specs, out_specs=out_specs, out_shape=out_shape,
        compiler_params=_params("arbitrary", "arbitrary"),
    )(x, gain.reshape(1, D), wt_all, wn_all)


def _stage_keys_kernel(ck_ref, cv_ref, cki_ref, nk_ref, nv_ref, nki_ref, kg_ref, vT_ref, kib_ref, kv_scr):
    past = ck_ref.shape[1]
    new = nk_ref.shape[1]
    L = kv_scr.shape[0]

    def gather_rows(cache_ref, new_ref, dst_ref, cast):
        dst_ref[0:past, :] = cast(cache_ref[0])
        dst_ref[past:past + new, :] = cast(new_ref[0])
        if past + new < L:
            dst_ref[past + new:L, :] = jnp.zeros((L - past - new, dst_ref.shape[-1]), dst_ref.dtype)

    gather_rows(ck_ref, nk_ref, kv_scr, lambda a: a)
    k_all = kv_scr[...]
    for h in range(A_KV_HEADS):
        kg_ref[0, h] = k_all[:, h * A_HEAD_DIM:(h + 1) * A_HEAD_DIM].astype(BF16)
    gather_rows(cv_ref, nv_ref, kv_scr, lambda a: a)
    v_t = jnp.transpose(kv_scr[...])
    for h in range(A_KV_HEADS):
        vT_ref[0, h, 0:A_HEAD_DIM, :] = v_t[h * A_HEAD_DIM:(h + 1) * A_HEAD_DIM, :].astype(BF16)
        vT_ref[0, h, A_HEAD_DIM:V_AUG_ROWS, :] = jnp.ones((V_AUG_ROWS - A_HEAD_DIM, L), BF16)
    gather_rows(cki_ref, nki_ref, kib_ref.at[0], lambda a: a.astype(BF16))


def _stage_sample_keys(cache_k, cache_v, cache_ki, new_k, new_v, new_ki, L):
    Bs, past, _ = cache_k.shape
    new = new_k.shape[1]
    row = lambda b: (b, 0, 0)
    return pl.pallas_call(
        _stage_keys_kernel, grid=(Bs,), name="stage_sample_keys",
        in_specs=[
            pl.BlockSpec((1, past, _NKV), row), pl.BlockSpec((1, past, _NKV), row),
            pl.BlockSpec((1, past, IDX_DIM), row),
            pl.BlockSpec((1, new, _NKV), row), pl.BlockSpec((1, new, _NKV), row),
            pl.BlockSpec((1, new, IDX_DIM), row),
        ],
        out_specs=(
            pl.BlockSpec((1, A_KV_HEADS, L, A_HEAD_DIM), lambda b: (b, 0, 0, 0)),
            pl.BlockSpec((1, A_KV_HEADS, V_AUG_ROWS, L), lambda b: (b, 0, 0, 0)),
            pl.BlockSpec((1, L, IDX_DIM), row),
        ),
        out_shape=(
            jax.ShapeDtypeStruct((Bs, A_KV_HEADS, L, A_HEAD_DIM), BF16),
            jax.ShapeDtypeStruct((Bs, A_KV_HEADS, V_AUG_ROWS, L), BF16),
            jax.ShapeDtypeStruct((Bs, L, IDX_DIM), BF16),
        ),
        scratch_shapes=[pltpu.VMEM((L, _NKV), F32)],
        compiler_params=_params("arbitrary"),
    )(cache_k, cache_v, cache_ki, new_k, new_v, new_ki)


def _dsa_attn_kernel(lim_ref, x_ref, qT_ref, qiT_ref, wT_ref, kg_ref, vT_ref, kib_ref, wo_ref,
                     out_ref, key_scr, tie_scr, oT_scr, res_scr, acc_scr, bias_scr, sa_scr, sb_scr,
                     ia_scr, ib_scr,
                     *, topk, causal, n_chunks_total, rows_out):
    tq = qT_ref.shape[-1]
    lc = DSA_KEY_CHUNK
    if causal:
        n_chunks = jnp.minimum(((pl.program_id(1) + 1) * tq + lc - 1) // lc, n_chunks_total)
    else:
        n_chunks = jnp.int32(n_chunks_total)
    n_pairs = n_chunks // 2
    odd = (n_chunks & 1) == 1
    last = n_chunks - 1
    lim = lim_ref[0]

    def chunk_start(c):
        return pl.multiple_of(c * lc, lc)

    def key_index(c):
        return c * lc + lax.broadcasted_iota(I32, (lc, tq), 0)

    qi_all = jnp.concatenate(
        [qiT_ref[0, 0, h * IDX_DIM:(h + 1) * IDX_DIM, :] for h in range(IDX_HEADS)], axis=1)
    w_rows = wT_ref[0, 0] * ((IDX_DIM ** -0.5) * (IDX_HEADS ** -0.5))

    def idx_logits(c):
        return _dot(kib_ref[0, pl.ds(chunk_start(c), lc), :], qi_all)

    def store_keys(s_ref, c):
        off = chunk_start(c)
        score = jnp.maximum(s_ref[:, 0:tq], 0.0) * w_rows[0:1, :]
        for h in range(1, IDX_HEADS):
            score = score + jnp.maximum(s_ref[:, h * tq:(h + 1) * tq], 0.0) * w_rows[h:h + 1, :]
        key_scr[pl.ds(off, lc), :] = jnp.where(key_index(c) < lim, score, -jnp.inf)

    ia_scr[...] = idx_logits(0)

    def score_body(i, carry):
        c0 = 2 * i
        ib_scr[...] = idx_logits(c0 + 1)
        store_keys(ia_scr, c0)
        ia_scr[...] = idx_logits(jnp.minimum(c0 + 2, last))
        store_keys(ib_scr, c0 + 1)
        return carry

    lax.fori_loop(0, n_pairs, score_body, 0)

    @pl.when(odd)
    def _():
        store_keys(ia_scr, last)

    def count(pred):
        def chunk_flags(c):
            kk = key_scr[pl.ds(chunk_start(c), lc), :]
            m = jnp.where(pred(kk, c), 1, 0).astype(I32)
            return [m[r * SUBLANES:(r + 1) * SUBLANES, :] for r in range(lc // SUBLANES)]

        def tree_sum(flags):
            while len(flags) > 1:
                flags = [a + b for a, b in zip(flags[0::2], flags[1::2])]
            return flags[0]

        def span(first, n):
            flags = []
            for j in range(n):
                flags += chunk_flags(first + j)
            return tree_sum(flags)

        zero = jnp.zeros((SUBLANES, tq), I32)
        acc = lax.fori_loop(0, n_pairs, lambda i, a: a + span(2 * i, 2), zero)
        acc = acc + lax.cond(odd, lambda: span(last, 1), lambda: zero)
        return jnp.sum(acc, axis=0, keepdims=True)

    def code_to_float(code):
        bits = jnp.where(code < 0, (-code) | INT_MIN, code)
        return pltpu.bitcast(bits, F32)

    def bit_body(i, carry):
        code, cge = carry
        cand = code + jnp.left_shift(jnp.int32(1), 31 - i)
        cand_f = code_to_float(cand)
        c = count(lambda kk, _: kk >= cand_f)
        c = c - jnp.where(cand_f == -jnp.inf, n_inadmissible, 0)
        ok = c >= topk
        return jnp.where(ok, cand, code), jnp.where(ok, c, cge)

    code0 = jnp.full((1, tq), INT_MIN, I32)
    cge0 = jnp.zeros((1, tq), I32) + n_chunks * lc
    n_inadmissible = n_chunks * lc - lim

    def settled(carry):
        ok = jnp.logical_or(carry[1] == topk, lim < topk)
        return jnp.min(jnp.where(ok, 1, 0)) > 0

    stops = SEARCH_CHECKPOINTS + (32,)
    carry = lax.fori_loop(0, stops[0], bit_body, (code0, cge0))
    for lo, hi in zip(stops[:-1], stops[1:]):
        carry = lax.cond(settled(carry), lambda c: c,
                         lambda c, lo=lo, hi=hi: lax.fori_loop(lo, hi, bit_body, c), carry)
    code, cge = carry
    thr = jnp.where(code > INT_MIN, code_to_float(code), -jnp.inf)

    n_idx_bits = max(1, (n_chunks_total * lc - 1).bit_length())
    tie_scr[...] = jnp.broadcast_to(jnp.where(thr == -jnp.inf, lim - 1, 2 ** 30), (SUBLANES, tq))
    surplus = jnp.logical_and(cge > topk, code > INT_MIN)

    @pl.when(jnp.max(jnp.where(surplus, 1, 0)) > 0)
    def _():
        cgt = count(lambda kk, _: kk > thr)
        want = topk - cgt
        bound = jnp.zeros((1, tq), I32)
        for bit in range(n_idx_bits - 1, -1, -1):
            cand = bound + (1 << bit)
            c = count(lambda kk, cc: jnp.logical_and(kk == thr, key_index(cc) < cand))
            bound = jnp.where(c < want, cand, bound)
        tie_scr[...] = jnp.broadcast_to(bound, (SUBLANES, tq))

    tie_bound = tie_scr[0:1, :]

    def bias_body(c, carry):
        off = chunk_start(c)
        kk = key_scr[pl.ds(off, lc), :]
        tie = jnp.where(key_index(c) <= tie_bound, 0.0, MASKED)
        b = jnp.where(kk > thr, 0.0, jnp.where(kk == thr, tie, MASKED))
        bias_scr[pl.ds(off, lc), :] = b.astype(BF16)
        return carry

    lax.fori_loop(0, n_chunks, bias_body, 0)

    gw = A_GROUP * tq
    eye = jnp.where(lax.broadcasted_iota(I32, (tq, gw), 0) == (lax.broadcasted_iota(I32, (tq, gw), 1) & (tq - 1)),
                    1.0, 0.0).astype(BF16)
    q_aug = [jnp.concatenate(
        [eye, jnp.concatenate(
            [qT_ref[0, 0, (g * A_GROUP + h) * A_HEAD_DIM:(g * A_GROUP + h + 1) * A_HEAD_DIM, :]
             for h in range(A_GROUP)], axis=1)], axis=0) for g in range(A_KV_HEADS)]
    acc_scr[...] = jnp.zeros(acc_scr.shape, F32)

    def logits(c, g):
        off = chunk_start(c)
        k_aug = jnp.concatenate([bias_scr[pl.ds(off, lc), :], kg_ref[0, g, pl.ds(off, lc), :]], axis=1)
        return _dot(k_aug, q_aug[g])

    def softmax_pv(s, c, g, m):
        mn = jnp.maximum(m, jnp.max(s, axis=0, keepdims=True))
        alpha = jnp.exp2(m - mn)
        p = jnp.exp2(s - mn).astype(BF16)
        vt = vT_ref[0, g, :, pl.ds(chunk_start(c), lc)]
        acc_scr[g] = alpha * acc_scr[g] + _dot(vt, p)
        return mn

    for g in range(A_KV_HEADS):
        sa_scr[g] = logits(0, g)

    def att_body(i, ms):
        c0 = 2 * i
        ms = list(ms)
        for g in range(A_KV_HEADS):
            sb_scr[g] = logits(c0 + 1, g)
            ms[g] = softmax_pv(sa_scr[g], c0, g, ms[g])
        c2 = jnp.minimum(c0 + 2, last)
        for g in range(A_KV_HEADS):
            sa_scr[g] = logits(c2, g)
            ms[g] = softmax_pv(sb_scr[g], c0 + 1, g, ms[g])
        return tuple(ms)

    m0 = tuple(jnp.full((1, gw), M_INIT, F32) for _ in range(A_KV_HEADS))
    ms = lax.fori_loop(0, n_pairs, att_body, m0)

    @pl.when(odd)
    def _():
        for g in range(A_KV_HEADS):
            softmax_pv(sa_scr[g], last, g, ms[g])

    for g in range(A_KV_HEADS):
        acc = acc_scr[g]
        o = acc[0:A_HEAD_DIM, :] / acc[A_HEAD_DIM:A_HEAD_DIM + 1, :]
        for h in range(A_GROUP):
            r0 = (g * A_GROUP + h) * A_HEAD_DIM
            oT_scr[r0:r0 + A_HEAD_DIM, :] = o[:, h * tq:(h + 1) * tq]

    o = jnp.transpose(oT_scr[...]).astype(BF16)
    res = x_ref[0] + _dot(o, wo_ref[...])
    if rows_out == tq:
        out_ref[0] = res
    else:
        res_scr[...] = res
        r0 = pl.multiple_of(pl.program_id(0) * rows_out, rows_out)
        out_ref[0] = res_scr[pl.ds(r0, rows_out), :]


def _dsa_attend(x, qT, qiT, wT, lim, kg, vT, kib, wo, *, topk, causal, shared_queries, rows_out):
    Bk, _, L, _ = kg.shape
    D = x.shape[-1]
    tq = qT.shape[-1]
    nq = qT.shape[1]
    assert L % DSA_KEY_CHUNK == 0
    if shared_queries:
        grid = (Bk, 1)
        qmap = lambda b, j: (0, 0, 0, 0)
        xmap = lambda b, j: (0, 0, 0)
        omap = lambda b, j: (0, b, 0)
        lmap = lambda b, j: (0, 0, 0)
    else:
        grid = (Bk, nq)
        qmap = lambda b, j: (b, j, 0, 0)
        xmap = lambda b, j: (b, j, 0)
        omap = xmap
        lmap = lambda b, j: (j, 0, 0)
    in_specs = [
        pl.BlockSpec((1, 1, tq), lmap),
        pl.BlockSpec((1, tq, D), xmap),
        pl.BlockSpec((1, 1, _NQ, tq), qmap),
        pl.BlockSpec((1, 1, _NQI, tq), qmap),
        pl.BlockSpec((1, 1, IDX_HEADS, tq), qmap),
        pl.BlockSpec((1, A_KV_HEADS, L, A_HEAD_DIM), lambda b, j: (b, 0, 0, 0)),
        pl.BlockSpec((1, A_KV_HEADS, V_AUG_ROWS, L), lambda b, j: (b, 0, 0, 0)),
        pl.BlockSpec((1, L, IDX_DIM), lambda b, j: (b, 0, 0)),
        pl.BlockSpec((None, _NQ, D), lambda b, j: (0, 0, 0)),
    ]
    kern = functools.partial(_dsa_attn_kernel, topk=topk, causal=causal,
                             n_chunks_total=L // DSA_KEY_CHUNK, rows_out=rows_out)
    return pl.pallas_call(
        kern, grid=grid, in_specs=in_specs, name="dsa_attend",
        out_specs=pl.BlockSpec((1, rows_out, D), omap),
        out_shape=jax.ShapeDtypeStruct(x.shape, F32),
        scratch_shapes=[pltpu.VMEM((L, tq), F32), pltpu.VMEM((SUBLANES, tq), I32),
                        pltpu.VMEM((_NQ, tq), F32), pltpu.VMEM((tq, D), F32),
                        pltpu.VMEM((A_KV_HEADS, V_AUG_ROWS, A_GROUP * tq), F32),
                        pltpu.VMEM((L, tq), BF16),
                        pltpu.VMEM((A_KV_HEADS, DSA_KEY_CHUNK, A_GROUP * tq), F32),
                        pltpu.VMEM((A_KV_HEADS, DSA_KEY_CHUNK, A_GROUP * tq), F32),
                        pltpu.VMEM((DSA_KEY_CHUNK, IDX_HEADS * tq), F32),
                        pltpu.VMEM((DSA_KEY_CHUNK, IDX_HEADS * tq), F32)],
        compiler_params=_params("arbitrary", "arbitrary"),
    )(lim, x, qT, qiT, wT, kg, vT, kib, wo)


CAST_STEPS = 8


def _cast_kernel(*refs):
    n = len(refs) // 2
    for src, dst in zip(refs[:n], refs[n:]):
        dst[...] = src[...].astype(BF16)


def _cast_bf16(*mats):
    specs = [pl.BlockSpec((m.shape[0], m.shape[1] // CAST_STEPS, m.shape[2]), lambda i: (0, i, 0)) for m in mats]
    return pl.pallas_call(
        _cast_kernel, grid=(CAST_STEPS,), name="cast_bf16",
        in_specs=specs, out_specs=tuple(specs),
        out_shape=tuple(jax.ShapeDtypeStruct(m.shape, BF16) for m in mats),
        compiler_params=_params("arbitrary"),
    )(*mats)


MLP_FF_CHUNK = 1024


def _mlp_kernel(x_ref, g_ref, wu_ref, wd_ref, gf_ref, out_ref, *, final_norm):
    x = x_ref[...]
    xn = _rms(x, g_ref[...]).astype(BF16)
    acc = x
    for f in range(0, wu_ref.shape[1], MLP_FF_CHUNK):
        h = jnp.maximum(_dot(xn, wu_ref[:, f:f + MLP_FF_CHUNK]), 0.0)
        acc = acc + _dot((h * h).astype(BF16), wd_ref[f:f + MLP_FF_CHUNK, :])
    if final_norm:
        acc = _rms(acc, gf_ref[...])
    out_ref[...] = acc


def _mlp(x, gain, w_up, w_down, layer, gain_final, *, tm, final_norm):
    M, D = x.shape
    FF = w_up.shape[2]
    const = lambda i: (0, 0)
    pick = lambda i: (layer, 0, 0)
    return pl.pallas_call(
        functools.partial(_mlp_kernel, final_norm=final_norm), name="mlp",
        grid=(M // tm,),
        in_specs=[
            pl.BlockSpec((tm, D), lambda i: (i, 0)),
            pl.BlockSpec((1, D), const),
            pl.BlockSpec((None, D, FF), pick, pipeline_mode=pl.Buffered(1)),
            pl.BlockSpec((None, FF, D), pick, pipeline_mode=pl.Buffered(1)),
            pl.BlockSpec((1, D), const),
        ],
        out_specs=pl.BlockSpec((tm, D), lambda i: (i, 0)),
        out_shape=jax.ShapeDtypeStruct((M, D), F32),
        compiler_params=_params("arbitrary"),
    )(x, gain.reshape(1, D), w_up, w_down, gain_final.reshape(1, D))


_GQK = GLA_HEADS * GLA_DK
_GV = GLA_HEADS * GLA_DV


def _gla_proj_kernel(x_ref, g_ref, w_ref, wa_ref, ba_ref, q_ref, k_ref, v_ref, gate_ref, la_ref):
    xn = _rms(x_ref[...], g_ref[...]).astype(BF16)
    y = _dot(xn, w_ref[...])
    D = gate_ref.shape[-1]
    q_ref[...] = y[:, 0:_GQK]
    k_ref[...] = y[:, _GQK:2 * _GQK]
    v_ref[...] = y[:, 2 * _GQK:2 * _GQK + _GV].astype(BF16)
    o = 2 * _GQK + _GV
    gate_ref[...] = y[:, o:o + D]
    a = y[:, o + D:o + D + GLA_RANK].astype(BF16)
    z = _dot(a, wa_ref[...]) + ba_ref[...]
    la_ref[...] = (jnp.minimum(z, 0.0) - jnp.log1p(jnp.exp(-jnp.abs(z)))) * (1.0 / GLA_TAU)


def _gla_project(x, gain, w_in, w_a2, b_a, *, tm):
    M, D = x.shape
    N = w_in.shape[2]
    const = lambda i: (0, 0)
    row = lambda i: (i, 0)
    return pl.pallas_call(
        _gla_proj_kernel, grid=(M // tm,), name="gla_project",
        in_specs=[
            pl.BlockSpec((tm, D), row),
            pl.BlockSpec((1, D), const),
            pl.BlockSpec((None, D, N), lambda i: (0, 0, 0)),
            pl.BlockSpec((GLA_RANK, _GQK), const),
            pl.BlockSpec((1, _GQK), const),
        ],
        out_specs=(
            pl.BlockSpec((tm, _GQK), row), pl.BlockSpec((tm, _GQK), row), pl.BlockSpec((tm, _GV), row),
            pl.BlockSpec((tm, D), row), pl.BlockSpec((tm, _GQK), row)),
        out_shape=(
            jax.ShapeDtypeStruct((M, _GQK), F32), jax.ShapeDtypeStruct((M, _GQK), F32),
            jax.ShapeDtypeStruct((M, _GV), BF16), jax.ShapeDtypeStruct((M, D), F32),
            jax.ShapeDtypeStruct((M, _GQK), F32)),
        compiler_params=_params("arbitrary"),
    )(x, gain.reshape(1, D), w_in, w_a2.astype(BF16), b_a.reshape(1, _GQK))


def _split3(a):
    a0 = a.astype(BF16)
    r = a - a0.astype(F32)
    a1 = r.astype(BF16)
    a2 = (r - a1.astype(F32)).astype(BF16)
    return a0, a1, a2


def _gla_kernel(x_ref, q_ref, k_ref, v_ref, gate_ref, la_ref, s0_ref, ng_ref, wo_ref,
                out_ref, sout_ref, s_scr, a_scr, o_scr):
    c = pl.program_id(1)
    C = q_ref.shape[1]

    @pl.when(c == 0)
    def _():
        s_scr[...] = s0_ref[0]

    la = la_ref[0]
    row = lax.broadcasted_iota(I32, (C, C), 0)
    col = lax.broadcasted_iota(I32, (C, C), 1)
    tri = jnp.where(row >= col, 1.0, 0.0).astype(BF16)
    l0, l1, l2 = _split3(la)
    b = _dot(tri, l0) + _dot(tri, l1) + _dot(tri, l2)
    q = q_ref[0] * (GLA_DK ** -0.5)
    k = k_ref[0]
    trow = lax.broadcasted_iota(I32, (C, 1), 0)

    def block_ref(values, size, pick):
        v3 = values.reshape(C // size, size, values.shape[-1])
        return jnp.broadcast_to(v3[:, pick:pick + 1, :], v3.shape).reshape(values.shape)

    def accumulate(qh, kh, mask, first):
        for h in range(GLA_HEADS):
            sl = slice(h * GLA_DK, (h + 1) * GLA_DK)
            blk = _dot_nt(qh[:, sl], kh[:, sl])
            if mask is not None:
                blk = jnp.where(mask, blk, 0.0)
            if first:
                a_scr[h] = blk
            else:
                a_scr[h] = a_scr[h] + blk

    d = min(GLA_DIAG, C)
    before = block_ref(b - la, d, 0)
    qh = (q * jnp.exp(b - before)).astype(BF16)
    kh = (k * jnp.exp(before - b)).astype(BF16)
    sh = d.bit_length() - 1
    accumulate(qh, kh, jnp.logical_and((row >> sh) == (col >> sh), row >= col), True)
    half = d
    while half < C:
        upper = (trow & (2 * half - 1)) >= half
        split = block_ref(b, 2 * half, half - 1)
        qh = (q * jnp.exp(jnp.where(upper, b - split, -jnp.inf))).astype(BF16)
        kh = (k * jnp.exp(jnp.where(upper, -jnp.inf, split - b))).astype(BF16)
        sh = (2 * half).bit_length() - 1
        accumulate(qh, kh, (row >> sh) == (col >> sh) if 2 * half < C else None, False)
        half *= 2

    b_end = b[C - 1:C, :]
    q_in = (q * jnp.exp(b)).astype(BF16)
    k_out = k * jnp.exp(b_end - b)
    ng = ng_ref[...]
    for h in range(GLA_HEADS):
        sl = slice(h * GLA_DK, (h + 1) * GLA_DK)
        vh = v_ref[0, :, h * GLA_DV:(h + 1) * GLA_DV]
        s_old = s_scr[h]
        o = _dot(q_in[:, sl], s_old.astype(BF16)) + _dot(a_scr[h].astype(BF16), vh)
        dec = jnp.transpose(jnp.broadcast_to(jnp.exp(b_end[:, sl]), (GLA_DK, GLA_DK)))
        dec = jnp.concatenate([dec] * (GLA_DV // GLA_DK), axis=1)
        s_scr[h] = dec * s_old + _dot(jnp.transpose(k_out[:, sl]).astype(BF16), vh)
        o = _rms(o, ng)
        gt = gate_ref[0, :, h * GLA_DV:(h + 1) * GLA_DV]
        o_scr[:, h * GLA_DV:(h + 1) * GLA_DV] = (o * (gt / (1.0 + jnp.exp(-gt)))).astype(BF16)

    out_ref[0] = x_ref[0] + _dot(o_scr[...], wo_ref[...])

    @pl.when(c == pl.num_programs(1) - 1)
    def _():
        sout_ref[0] = s_scr[...]


def _gla(x, q, k, v, gate, la, s0, norm_g, w_o, *, chunk):
    B, T, D = x.shape
    n = T // chunk
    tok = lambda b, c: (b, c, 0)
    const = lambda b, c: (0, 0)
    st = lambda b, c: (b, 0, 0, 0)
    return pl.pallas_call(
        _gla_kernel, grid=(B, n), name="gla",
        in_specs=[
            pl.BlockSpec((1, chunk, D), tok),
            pl.BlockSpec((1, chunk, _GQK), tok),
            pl.BlockSpec((1, chunk, _GQK), tok),
            pl.BlockSpec((1, chunk, _GV), tok),
            pl.BlockSpec((1, chunk, D), tok),
            pl.BlockSpec((1, chunk, _GQK), tok),
            pl.BlockSpec((1, GLA_HEADS, GLA_DK, GLA_DV), st),
            pl.BlockSpec((1, GLA_DV), const),
            pl.BlockSpec((None, _GV, D), lambda b, c: (0, 0, 0)),
        ],
        out_specs=(pl.BlockSpec((1, chunk, D), tok), pl.BlockSpec((1, GLA_HEADS, GLA_DK, GLA_DV), st)),
        out_shape=(jax.ShapeDtypeStruct((B, T, D), F32),
                   jax.ShapeDtypeStruct((B, GLA_HEADS, GLA_DK, GLA_DV), F32)),
        scratch_shapes=[pltpu.VMEM((GLA_HEADS, GLA_DK, GLA_DV), F32),
                        pltpu.VMEM((GLA_HEADS, chunk, chunk), F32),
                        pltpu.VMEM((chunk, _GV), BF16)],
        compiler_params=_params("arbitrary", "arbitrary"),
    )(x, q, k, v, gate, la, s0, norm_g.reshape(1, GLA_DV), w_o)


def _pick_tile(n, candidates):
    for t in candidates:
        if n % t == 0:
            return t
    return n


def _round_up(n, m):
    return (n + m - 1) // m * m


def kernel(x_prompt, x_sample, cache_k, cache_v, cache_kidx, state_gla, norm_mix, norm_ffn, norm_final,
           a_w_in, a_w_o, b_w_in, b_w_a2, b_b_a, b_norm, b_w_o, ffn_w_up, ffn_w_down):
    B, S, D = x_prompt.shape
    Bs, Ts, _ = x_sample.shape
    past = cache_k.shape[2]
    Ms = Bs * Ts
    tq = DSA_Q_TILE
    assert S % tq == 0 and Ms == tq and Ts % SUBLANES == 0

    wu_bf, wd_bf, bwin_bf, bwo_bf, wo_a = _cast_bf16(ffn_w_up, ffn_w_down, b_w_in, b_w_o, a_w_o)
    tm = _pick_tile(S, (1024, 512, 256, 128))
    qT, qiT, wT, vT, k_p, v_p, ki_p, kg, kib = _dsa_project(x_prompt, norm_mix[0], a_w_in[0], tm=tm, tq=tq)
    pos = jnp.arange(S, dtype=I32)
    lim_p = ((pos // STREAM_CHUNK + 1) * STREAM_CHUNK).reshape(S // tq, 1, tq)
    L_p = _round_up(S, DSA_KEY_CHUNK)
    if L_p != S:
        pad = L_p - S
        kg = jnp.pad(kg, ((0, 0), (0, 0), (0, pad), (0, 0)))
        vT = jnp.pad(vT, ((0, 0), (0, 0), (0, 0), (0, pad)))
        kib = jnp.pad(kib, ((0, 0), (0, pad), (0, 0)))
    hp = _dsa_attend(x_prompt, qT, qiT, wT, lim_p, kg, vT, kib, wo_a,
                     topk=min(TOPK_MAX, S // 4), causal=True, shared_queries=False, rows_out=tq)
    xs = x_sample.reshape(1, Ms, D)
    qT_s, qiT_s, wT_s, _, k_s, v_s, ki_s, _, _ = _dsa_project(xs, norm_mix[0], a_w_in[0], tm=Ms, tq=tq)
    k_s = k_s.reshape(Bs, Ts, _NKV)
    v_s = v_s.reshape(Bs, Ts, _NKV)
    ki_s = ki_s.reshape(Bs, Ts, IDX_DIM)
    L_real = past + Ts
    L_s = _round_up(L_real, DSA_KEY_CHUNK)
    kg_s, vT_s, kib_s = _stage_sample_keys(
        cache_k[0].reshape(Bs, past, _NKV), cache_v[0].reshape(Bs, past, _NKV), cache_kidx[0],
        k_s, v_s, ki_s, L_s)
    pos_s = past + jnp.arange(Ts, dtype=I32)
    lim_s = jnp.minimum((pos_s // STREAM_CHUNK + 1) * STREAM_CHUNK, L_real)
    lim_s = jnp.tile(lim_s, Bs).reshape(1, 1, Ms)
    hs = _dsa_attend(xs, qT_s, qiT_s, wT_s, lim_s, kg_s, vT_s, kib_s, wo_a,
                     topk=min(TOPK_MAX, L_real // 4), causal=False, shared_queries=True, rows_out=Ts)

    tmm = _pick_tile(B * S, (1024, 512, 256, 128))
    hp = _mlp(hp.reshape(B * S, D), norm_ffn[0], wu_bf, wd_bf, 0, norm_final, tm=tmm, final_norm=False)
    hs = _mlp(hs.reshape(Ms, D), norm_ffn[0], wu_bf, wd_bf, 0, norm_final, tm=Ms, final_norm=False)

    q, k, v, gate, la = _gla_project(hp, norm_mix[1], bwin_bf, b_w_a2[0], b_b_a[0], tm=tmm)
    chunk = _pick_tile(S, (256, 128, 64))
    r3 = lambda a: a.reshape(B, S, a.shape[-1])
    s0 = jnp.zeros((B, GLA_HEADS, GLA_DK, GLA_DV), F32)
    hp, s_p = _gla(hp.reshape(B, S, D), r3(q), r3(k), r3(v), r3(gate), r3(la), s0, b_norm[0], bwo_bf, chunk=chunk)
    q, k, v, gate, la = _gla_project(hs, norm_mix[1], bwin_bf, b_w_a2[0], b_b_a[0], tm=Ms)
    r3 = lambda a: a.reshape(Bs, Ts, a.shape[-1])
    hs, s_s = _gla(hs.reshape(Bs, Ts, D), r3(q), r3(k), r3(v), r3(gate), r3(la), state_gla[0], b_norm[0], bwo_bf,
                   chunk=Ts)

    y_p = _mlp(hp.reshape(B * S, D), norm_ffn[1], wu_bf, wd_bf, 1, norm_final, tm=tmm, final_norm=True)
    y_s = _mlp(hs.reshape(Ms, D), norm_ffn[1], wu_bf, wd_bf, 1, norm_final, tm=Ms, final_norm=True)

    return (y_p.reshape(B, S, D), y_s.reshape(Bs, Ts, D),
            k_p.reshape(1, B, S, A_KV_HEADS, A_HEAD_DIM), v_p.reshape(1, B, S, A_KV_HEADS, A_HEAD_DIM),
            ki_p.reshape(1, B, S, IDX_DIM), s_p[None],
            k_s.reshape(1, Bs, Ts, A_KV_HEADS, A_HEAD_DIM), v_s.reshape(1, Bs, Ts, A_KV_HEADS, A_HEAD_DIM),
            ki_s.reshape(1, Bs, Ts, IDX_DIM), s_s[None])
```

```python
import functools

import jax
import jax.numpy as jnp
from jax import lax
from jax.experimental import pallas as pl
from jax.experimental.pallas import tpu as pltpu

F32 = jnp.float32
BF16 = jnp.bfloat16
I32 = jnp.int32

EPS = 1e-6
STREAM_CHUNK = 64
TOPK_MAX = 256
A_HEADS = 16
A_KV_HEADS = 4
A_GROUP = A_HEADS // A_KV_HEADS
A_HEAD_DIM = 64
IDX_HEADS = 8
IDX_DIM = 64
GLA_HEADS = 4
GLA_DK = 128
GLA_DV = 256
GLA_RANK = 16
GLA_TAU = 16.0

LANES = 128
SUBLANES = 8
INT_MIN = -(2 ** 31)

DSA_Q_TILE = LANES
DSA_KEY_CHUNK = 256
SEARCH_CHECKPOINTS = (22, 25, 28)
GLA_DIAG = 16
VMEM_LIMIT = 56 * 1024 * 1024


def _params(*sem):
    return pltpu.CompilerParams(dimension_semantics=sem, vmem_limit_bytes=VMEM_LIMIT)


def _rms(x, g):
    ms = jnp.mean(x * x, axis=-1, keepdims=True)
    return x * lax.rsqrt(ms + EPS) * g


def _dot(a, b):
    return jnp.dot(a, b, preferred_element_type=F32)


def _dot_nt(a, b):
    return lax.dot_general(a, b, (((1,), (1,)), ((), ())), preferred_element_type=F32)


_NQ = A_HEADS * A_HEAD_DIM
_NKV = A_KV_HEADS * A_HEAD_DIM
_NQI = IDX_HEADS * IDX_DIM
BF16_SUBLANES = 16
_T_ROWS = _NQ + _NQI + _NKV + BF16_SUBLANES
V_AUG_ROWS = A_HEAD_DIM + BF16_SUBLANES
LOG2E = 1.4426950408889634
MASKED = -1e30
M_INIT = -1e29


def _dsa_proj_kernel(x_ref, g_ref, wt_ref, wn_ref,
                     qT_ref, qiT_ref, wT_ref, vT_ref, k_ref, v_ref, ki_ref, kg_ref, kib_ref, *, tq):
    xn = _rms(x_ref[0], g_ref[...]).astype(BF16)
    tm = xn.shape[0]
    yT = _dot_nt(wt_ref[...], xn)
    y = _dot(xn, wn_ref[...])
    for j in range(tm // tq):
        sl = slice(j * tq, (j + 1) * tq)
        qT_ref[0, j] = yT[0:_NQ, sl].astype(BF16)
        qiT_ref[0, j] = yT[_NQ:_NQ + _NQI, sl].astype(BF16)
        wT_ref[0, j] = yT[_NQ + _NQI + _NKV:_NQ + _NQI + _NKV + IDX_HEADS, sl]
    for h in range(A_KV_HEADS):
        r0 = _NQ + _NQI + h * A_HEAD_DIM
        vT_ref[0, h, 0:A_HEAD_DIM, :] = yT[r0:r0 + A_HEAD_DIM, :].astype(BF16)
        vT_ref[0, h, A_HEAD_DIM:V_AUG_ROWS, :] = jnp.ones((V_AUG_ROWS - A_HEAD_DIM, tm), BF16)
    k = y[:, 0:_NKV]
    ki = y[:, 2 * _NKV:2 * _NKV + IDX_DIM]
    k_ref[0] = k
    v_ref[0] = y[:, _NKV:2 * _NKV]
    ki_ref[0] = ki
    for h in range(A_KV_HEADS):
        kg_ref[0, h] = k[:, h * A_HEAD_DIM:(h + 1) * A_HEAD_DIM].astype(BF16)
    kib_ref[0] = ki.astype(BF16)


def _dsa_project(x, gain, w_in, *, tm, tq):
    B, S, D = x.shape
    o = 0
    parts = []
    for n in (_NQ, _NKV, _NKV, _NQI, IDX_DIM, IDX_HEADS):
        parts.append(w_in[:, o:o + n])
        o += n
    w_q, w_k, w_v, w_qi, w_ki, w_wt = parts
    wt_all = jnp.concatenate(
        [w_q * (A_HEAD_DIM ** -0.5 * LOG2E), w_qi, w_v, w_wt, jnp.zeros((D, BF16_SUBLANES - IDX_HEADS), F32)],
        axis=1).T.astype(BF16)
    wn_all = jnp.concatenate([w_k, w_v, w_ki], axis=1).astype(BF16)
    nq = S // tq
    nt = S // tm
    jq = tm // tq
    out_shape = (
        jax.ShapeDtypeStruct((B, nq, _NQ, tq), BF16),
        jax.ShapeDtypeStruct((B, nq, _NQI, tq), BF16),
        jax.ShapeDtypeStruct((B, nq, IDX_HEADS, tq), F32),
        jax.ShapeDtypeStruct((B, A_KV_HEADS, V_AUG_ROWS, S), BF16),
        jax.ShapeDtypeStruct((B, S, _NKV), F32),
        jax.ShapeDtypeStruct((B, S, _NKV), F32),
        jax.ShapeDtypeStruct((B, S, IDX_DIM), F32),
        jax.ShapeDtypeStruct((B, A_KV_HEADS, S, A_HEAD_DIM), BF16),
        jax.ShapeDtypeStruct((B, S, IDX_DIM), BF16),
    )
    out_specs = (
        pl.BlockSpec((1, jq, _NQ, tq), lambda b, i: (b, i, 0, 0)),
        pl.BlockSpec((1, jq, _NQI, tq), lambda b, i: (b, i, 0, 0)),
        pl.BlockSpec((1, jq, IDX_HEADS, tq), lambda b, i: (b, i, 0, 0)),
        pl.BlockSpec((1, A_KV_HEADS, V_AUG_ROWS, tm), lambda b, i: (b, 0, 0, i)),
        pl.BlockSpec((1, tm, _NKV), lambda b, i: (b, i, 0)),
        pl.BlockSpec((1, tm, _NKV), lambda b, i: (b, i, 0)),
        pl.BlockSpec((1, tm, IDX_DIM), lambda b, i: (b, i, 0)),
        pl.BlockSpec((1, A_KV_HEADS, tm, A_HEAD_DIM), lambda b, i: (b, 0, i, 0)),
        pl.BlockSpec((1, tm, IDX_DIM), lambda b, i: (b, i, 0)),
    )
    in_specs = [
        pl.BlockSpec((1, tm, D), lambda b, i: (b, i, 0)),
        pl.BlockSpec((1, D), lambda b, i: (0, 0)),
        pl.BlockSpec((_T_ROWS, D), lambda b, i: (0, 0)),
        pl.BlockSpec((D, 2 * _NKV + IDX_DIM), lambda b, i: (0, 0)),
    ]
    return pl.pallas_call(
        functools.partial(_dsa_proj_kernel, tq=tq), name="dsa_project",
        grid=(B, nt), in_specs=in_specs, out_specs=out_specs, out_shape=out_shape,
        compiler_params=_params("arbitrary", "arbitrary"),
    )(x, gain.reshape(1, D), wt_all, wn_all)


def _stage_keys_kernel(ck_ref, cv_ref, cki_ref, nk_ref, nv_ref, nki_ref, kg_ref, vT_ref, kib_ref, kv_scr):
    past = ck_ref.shape[1]
    new = nk_ref.shape[1]
    L = kv_scr.shape[0]

    def gather_rows(cache_ref, new_ref, dst_ref, cast):
        dst_ref[0:past, :] = cast(cache_ref[0])
        dst_ref[past:past + new, :] = cast(new_ref[0])
        if past + new < L:
            dst_ref[past + new:L, :] = jnp.zeros((L - past - new, dst_ref.shape[-1]), dst_ref.dtype)

    gather_rows(ck_ref, nk_ref, kv_scr, lambda a: a)
    k_all = kv_scr[...]
    for h in range(A_KV_HEADS):
        kg_ref[0, h] = k_all[:, h * A_HEAD_DIM:(h + 1) * A_HEAD_DIM].astype(BF16)
    gather_rows(cv_ref, nv_ref, kv_scr, lambda a: a)
    v_t = jnp.transpose(kv_scr[...])
    for h in range(A_KV_HEADS):
        vT_ref[0, h, 0:A_HEAD_DIM, :] = v_t[h * A_HEAD_DIM:(h + 1) * A_HEAD_DIM, :].astype(BF16)
        vT_ref[0, h, A_HEAD_DIM:V_AUG_ROWS, :] = jnp.ones((V_AUG_ROWS - A_HEAD_DIM, L), BF16)
    gather_rows(cki_ref, nki_ref, kib_ref.at[0], lambda a: a.astype(BF16))


def _stage_sample_keys(cache_k, cache_v, cache_ki, new_k, new_v, new_ki, L):
    Bs, past, _ = cache_k.shape
    new = new_k.shape[1]
    row = lambda b: (b, 0, 0)
    return pl.pallas_call(
        _stage_keys_kernel, grid=(Bs,), name="stage_sample_keys",
        in_specs=[
            pl.BlockSpec((1, past, _NKV), row), pl.BlockSpec((1, past, _NKV), row),
            pl.BlockSpec((1, past, IDX_DIM), row),
            pl.BlockSpec((1, new, _NKV), row), pl.BlockSpec((1, new, _NKV), row),
            pl.BlockSpec((1, new, IDX_DIM), row),
        ],
        out_specs=(
            pl.BlockSpec((1, A_KV_HEADS, L, A_HEAD_DIM), lambda b: (b, 0, 0, 0)),
            pl.BlockSpec((1, A_KV_HEADS, V_AUG_ROWS, L), lambda b: (b, 0, 0, 0)),
            pl.BlockSpec((1, L, IDX_DIM), row),
        ),
        out_shape=(
            jax.ShapeDtypeStruct((Bs, A_KV_HEADS, L, A_HEAD_DIM), BF16),
            jax.ShapeDtypeStruct((Bs, A_KV_HEADS, V_AUG_ROWS, L), BF16),
            jax.ShapeDtypeStruct((Bs, L, IDX_DIM), BF16),
        ),
        scratch_shapes=[pltpu.VMEM((L, _NKV), F32)],
        compiler_params=_params("arbitrary"),
    )(cache_k, cache_v, cache_ki, new_k, new_v, new_ki)


def _dsa_attn_kernel(lim_ref, x_ref, qT_ref, qiT_ref, wT_ref, kg_ref, vT_ref, kib_ref, wo_ref,
                     out_ref, key_scr, oT_scr, res_scr, acc_scr, bias_scr, sa_scr, sb_scr,
                     ia_scr, ib_scr, tie_scr,
                     *, topk, causal, n_chunks_total, rows_out):
    tq = qT_ref.shape[-1]
    lc = DSA_KEY_CHUNK
    if causal:
        n_chunks = jnp.minimum(((pl.program_id(1) + 1) * tq + lc - 1) // lc, n_chunks_total)
    else:
        n_chunks = jnp.int32(n_chunks_total)
    n_pairs = n_chunks // 2
    odd = (n_chunks & 1) == 1
    last = n_chunks - 1
    lim = lim_ref[0]

    def chunk_start(c):
        return pl.multiple_of(c * lc, lc)

    def key_index(c):
        return c * lc + lax.broadcasted_iota(I32, (lc, tq), 0)

    qi_all = jnp.concatenate(
        [qiT_ref[0, 0, h * IDX_DIM:(h + 1) * IDX_DIM, :] for h in range(IDX_HEADS)], axis=1)
    w_rows = wT_ref[0, 0] * ((IDX_DIM ** -0.5) * (IDX_HEADS ** -0.5))

    def idx_logits(c):
        return _dot(kib_ref[0, pl.ds(chunk_start(c), lc), :], qi_all)

    def store_keys(s_ref, c):
        off = chunk_start(c)
        score = jnp.maximum(s_ref[:, 0:tq], 0.0) * w_rows[0:1, :]
        for h in range(1, IDX_HEADS):
            score = score + jnp.maximum(s_ref[:, h * tq:(h + 1) * tq], 0.0) * w_rows[h:h + 1, :]
        key_scr[pl.ds(off, lc), :] = jnp.where(key_index(c) < lim, score, -jnp.inf)

    ia_scr[...] = idx_logits(0)

    def score_body(i, carry):
        c0 = 2 * i
        ib_scr[...] = idx_logits(c0 + 1)
        store_keys(ia_scr, c0)
        ia_scr[...] = idx_logits(jnp.minimum(c0 + 2, last))
        store_keys(ib_scr, c0 + 1)
        return carry

    lax.fori_loop(0, n_pairs, score_body, 0)

    @pl.when(odd)
    def _():
        store_keys(ia_scr, last)

    def count(pred):
        def chunk_flags(c):
            kk = key_scr[pl.ds(chunk_start(c), lc), :]
            m = jnp.where(pred(kk, c), 1, 0).astype(I32)
            return [m[r * SUBLANES:(r + 1) * SUBLANES, :] for r in range(lc // SUBLANES)]

        def tree_sum(flags):
            while len(flags) > 1:
                flags = [a + b for a, b in zip(flags[0::2], flags[1::2])]
            return flags[0]

        def span(first, n):
            flags = []
            for j in range(n):
                flags += chunk_flags(first + j)
            return tree_sum(flags)

        zero = jnp.zeros((SUBLANES, tq), I32)
        acc = lax.fori_loop(0, n_pairs, lambda i, a: a + span(2 * i, 2), zero)
        acc = acc + lax.cond(odd, lambda: span(last, 1), lambda: zero)
        return jnp.sum(acc, axis=0, keepdims=True)

    def code_to_float(code):
        bits = jnp.where(code < 0, (-code) | INT_MIN, code)
        return pltpu.bitcast(bits, F32)

    def bit_body(i, carry):
        code, cge = carry
        cand = code + jnp.left_shift(jnp.int32(1), 31 - i)
        cand_f = code_to_float(cand)
        c = count(lambda kk, _: kk >= cand_f)
        c = c - jnp.where(cand_f == -jnp.inf, n_inadmissible, 0)
        ok = c >= topk
        return jnp.where(ok, cand, code), jnp.where(ok, c, cge)

    code0 = jnp.full((1, tq), INT_MIN, I32)
    cge0 = jnp.zeros((1, tq), I32) + n_chunks * lc
    n_inadmissible = n_chunks * lc - lim

    def settled(carry):
        ok = jnp.logical_or(carry[1] == topk, lim < topk)
        return jnp.min(jnp.where(ok, 1, 0)) > 0

    stops = SEARCH_CHECKPOINTS + (32,)
    carry = lax.fori_loop(0, stops[0], bit_body, (code0, cge0))
    for lo, hi in zip(stops[:-1], stops[1:]):
        carry = lax.cond(settled(carry), lambda c: c,
                         lambda c, lo=lo, hi=hi: lax.fori_loop(lo, hi, bit_body, c), carry)
    code, cge = carry
    thr = jnp.where(code > INT_MIN, code_to_float(code), -jnp.inf)

    n_idx_bits = max(1, (n_chunks_total * lc - 1).bit_length())
    tie_scr[...] = jnp.broadcast_to(jnp.where(thr == -jnp.inf, lim - 1, 2 ** 30), (SUBLANES, tq))
    surplus = jnp.logical_and(cge > topk, code > INT_MIN)

    @pl.when(jnp.max(jnp.where(surplus, 1, 0)) > 0)
    def _():
        cgt = count(lambda kk, _: kk > thr)
        want = topk - cgt
        bound = jnp.zeros((1, tq), I32)
        for bit in range(n_idx_bits - 1, -1, -1):
            cand = bound + (1 << bit)
            c = count(lambda kk, cc: jnp.logical_and(kk == thr, key_index(cc) < cand))
            bound = jnp.where(c < want, cand, bound)
        tie_scr[...] = jnp.broadcast_to(bound, (SUBLANES, tq))

    tie_bound = tie_scr[0:1, :]

    def bias_body(c, carry):
        off = chunk_start(c)
        kk = key_scr[pl.ds(off, lc), :]
        tie = jnp.where(key_index(c) <= tie_bound, 0.0, MASKED)
        b = jnp.where(kk > thr, 0.0, jnp.where(kk == thr, tie, MASKED))
        bias_scr[pl.ds(off, lc), :] = b.astype(BF16)
        return carry

    lax.fori_loop(0, n_chunks, bias_body, 0)

    gw = A_GROUP * tq
    eye = jnp.where(lax.broadcasted_iota(I32, (tq, gw), 0) == (lax.broadcasted_iota(I32, (tq, gw), 1) & (tq - 1)),
                    1.0, 0.0).astype(BF16)
    q_aug = [jnp.concatenate(
        [eye, jnp.concatenate(
            [qT_ref[0, 0, (g * A_GROUP + h) * A_HEAD_DIM:(g * A_GROUP + h + 1) * A_HEAD_DIM, :]
             for h in range(A_GROUP)], axis=1)], axis=0) for g in range(A_KV_HEADS)]
    acc_scr[...] = jnp.zeros(acc_scr.shape, F32)

    def logits(c, g):
        off = chunk_start(c)
        k_aug = jnp.concatenate([bias_scr[pl.ds(off, lc), :], kg_ref[0, g, pl.ds(off, lc), :]], axis=1)
        return _dot(k_aug, q_aug[g])

    def softmax_pv(s, c, g, m):
        mn = jnp.maximum(m, jnp.max(s, axis=0, keepdims=True))
        alpha = jnp.exp2(m - mn)
        p = jnp.exp2(s - mn).astype(BF16)
        vt = vT_ref[0, g, :, pl.ds(chunk_start(c), lc)]
        acc_scr[g] = alpha * acc_scr[g] + _dot(vt, p)
        return mn

    for g in range(A_KV_HEADS):
        sa_scr[g] = logits(0, g)

    def att_body(i, ms):
        c0 = 2 * i
        ms = list(ms)
        for g in range(A_KV_HEADS):
            sb_scr[g] = logits(c0 + 1, g)
            ms[g] = softmax_pv(sa_scr[g], c0, g, ms[g])
        c2 = jnp.minimum(c0 + 2, last)
        for g in range(A_KV_HEADS):
            sa_scr[g] = logits(c2, g)
            ms[g] = softmax_pv(sb_scr[g], c0 + 1, g, ms[g])
        return tuple(ms)

    m0 = tuple(jnp.full((1, gw), M_INIT, F32) for _ in range(A_KV_HEADS))
    ms = lax.fori_loop(0, n_pairs, att_body, m0)

    @pl.when(odd)
    def _():
        for g in range(A_KV_HEADS):
            softmax_pv(sa_scr[g], last, g, ms[g])

    for g in range(A_KV_HEADS):
        acc = acc_scr[g]
        o = acc[0:A_HEAD_DIM, :] / acc[A_HEAD_DIM:A_HEAD_DIM + 1, :]
        for h in range(A_GROUP):
            r0 = (g * A_GROUP + h) * A_HEAD_DIM
            oT_scr[r0:r0 + A_HEAD_DIM, :] = o[:, h * tq:(h + 1) * tq]

    o = jnp.transpose(oT_scr[...]).astype(BF16)
    res = x_ref[0] + _dot(o, wo_ref[...])
    if rows_out == tq:
        out_ref[0] = res
    else:
        res_scr[...] = res
        r0 = pl.multiple_of(pl.program_id(0) * rows_out, rows_out)
        out_ref[0] = res_scr[pl.ds(r0, rows_out), :]


def _dsa_attend(x, qT, qiT, wT, lim, kg, vT, kib, wo, *, topk, causal, shared_queries, rows_out):
    Bk, _, L, _ = kg.shape
    D = x.shape[-1]
    tq = qT.shape[-1]
    nq = qT.shape[1]
    assert L % DSA_KEY_CHUNK == 0
    if shared_queries:
        grid = (Bk, 1)
        qmap = lambda b, j: (0, 0, 0, 0)
        xmap = lambda b, j: (0, 0, 0)
        omap = lambda b, j: (0, b, 0)
        lmap = lambda b, j: (0, 0, 0)
    else:
        grid = (Bk, nq)
        qmap = lambda b, j: (b, j, 0, 0)
        xmap = lambda b, j: (b, j, 0)
        omap = xmap
        lmap = lambda b, j: (j, 0, 0)
    in_specs = [
        pl.BlockSpec((1, 1, tq), lmap),
        pl.BlockSpec((1, tq, D), xmap),
        pl.BlockSpec((1, 1, _NQ, tq), qmap),
        pl.BlockSpec((1, 1, _NQI, tq), qmap),
        pl.BlockSpec((1, 1, IDX_HEADS, tq), qmap),
        pl.BlockSpec((1, A_KV_HEADS, L, A_HEAD_DIM), lambda b, j: (b, 0, 0, 0)),
        pl.BlockSpec((1, A_KV_HEADS, V_AUG_ROWS, L), lambda b, j: (b, 0, 0, 0)),
        pl.BlockSpec((1, L, IDX_DIM), lambda b, j: (b, 0, 0)),
        pl.BlockSpec((None, _NQ, D), lambda b, j: (0, 0, 0)),
    ]
    kern = functools.partial(_dsa_attn_kernel, topk=topk, causal=causal,
                             n_chunks_total=L // DSA_KEY_CHUNK, rows_out=rows_out)
    return pl.pallas_call(
        kern, grid=grid, in_specs=in_specs, name="dsa_attend",
        out_specs=pl.BlockSpec((1, rows_out, D), omap),
        out_shape=jax.ShapeDtypeStruct(x.shape, F32),
        scratch_shapes=[pltpu.VMEM((L, tq), F32),
                        pltpu.VMEM((_NQ, tq), F32), pltpu.VMEM((tq, D), F32),
                        pltpu.VMEM((A_KV_HEADS, V_AUG_ROWS, A_GROUP * tq), F32),
                        pltpu.VMEM((L, tq), BF16),
                        pltpu.VMEM((A_KV_HEADS, DSA_KEY_CHUNK, A_GROUP * tq), F32),
                        pltpu.VMEM((A_KV_HEADS, DSA_KEY_CHUNK, A_GROUP * tq), F32),
                        pltpu.VMEM((DSA_KEY_CHUNK, IDX_HEADS * tq), F32),
                        pltpu.VMEM((DSA_KEY_CHUNK, IDX_HEADS * tq), F32),
                        pltpu.VMEM((SUBLANES, tq), I32)],
        compiler_params=_params("arbitrary", "arbitrary"),
    )(lim, x, qT, qiT, wT, kg, vT, kib, wo)


CAST_STEPS = 8


def _cast_kernel(*refs):
    n = len(refs) // 2
    for src, dst in zip(refs[:n], refs[n:]):
        dst[...] = src[...].astype(BF16)


def _cast_bf16(*mats):
    specs = [pl.BlockSpec((m.shape[0], m.shape[1] // CAST_STEPS, m.shape[2]), lambda i: (0, i, 0)) for m in mats]
    return pl.pallas_call(
        _cast_kernel, grid=(CAST_STEPS,), name="cast_bf16",
        in_specs=specs, out_specs=tuple(specs),
        out_shape=tuple(jax.ShapeDtypeStruct(m.shape, BF16) for m in mats),
        compiler_params=_params("arbitrary"),
    )(*mats)


MLP_FF_CHUNK = 1024


def _mlp_kernel(x_ref, g_ref, wu_ref, wd_ref, gf_ref, out_ref, *, final_norm):
    x = x_ref[...]
    xn = _rms(x, g_ref[...]).astype(BF16)
    acc = x
    for f in range(0, wu_ref.shape[1], MLP_FF_CHUNK):
        h = jnp.maximum(_dot(xn, wu_ref[:, f:f + MLP_FF_CHUNK]), 0.0)
        acc = acc + _dot((h * h).astype(BF16), wd_ref[f:f + MLP_FF_CHUNK, :])
    if final_norm:
        acc = _rms(acc, gf_ref[...])
    out_ref[...] = acc


def _mlp(x, gain, w_up, w_down, layer, gain_final, *, tm, final_norm):
    M, D = x.shape
    FF = w_up.shape[2]
    const = lambda i: (0, 0)
    pick = lambda i: (layer, 0, 0)
    return pl.pallas_call(
        functools.partial(_mlp_kernel, final_norm=final_norm), name="mlp",
        grid=(M // tm,),
        in_specs=[
            pl.BlockSpec((tm, D), lambda i: (i, 0)),
            pl.BlockSpec((1, D), const),
            pl.BlockSpec((None, D, FF), pick, pipeline_mode=pl.Buffered(1)),
            pl.BlockSpec((None, FF, D), pick, pipeline_mode=pl.Buffered(1)),
            pl.BlockSpec((1, D), const),
        ],
        out_specs=pl.BlockSpec((tm, D), lambda i: (i, 0)),
        out_shape=jax.ShapeDtypeStruct((M, D), F32),
        compiler_params=_params("arbitrary"),
    )(x, gain.reshape(1, D), w_up, w_down, gain_final.reshape(1, D))


_GQK = GLA_HEADS * GLA_DK
_GV = GLA_HEADS * GLA_DV


def _gla_proj_kernel(x_ref, g_ref, w_ref, wa_ref, ba_ref, q_ref, k_ref, v_ref, gate_ref, la_ref):
    xn = _rms(x_ref[...], g_ref[...]).astype(BF16)
    y = _dot(xn, w_ref[...])
    D = gate_ref.shape[-1]
    q_ref[...] = y[:, 0:_GQK]
    k_ref[...] = y[:, _GQK:2 * _GQK]
    v_ref[...] = y[:, 2 * _GQK:2 * _GQK + _GV].astype(BF16)
    o = 2 * _GQK + _GV
    gate_ref[...] = y[:, o:o + D]
    a = y[:, o + D:o + D + GLA_RANK].astype(BF16)
    z = _dot(a, wa_ref[...]) + ba_ref[...]
    la_ref[...] = (jnp.minimum(z, 0.0) - jnp.log1p(jnp.exp(-jnp.abs(z)))) * (1.0 / GLA_TAU)


def _gla_project(x, gain, w_in, w_a2, b_a, *, tm):
    M, D = x.shape
    N = w_in.shape[2]
    const = lambda i: (0, 0)
    row = lambda i: (i, 0)
    return pl.pallas_call(
        _gla_proj_kernel, grid=(M // tm,), name="gla_project",
        in_specs=[
            pl.BlockSpec((tm, D), row),
            pl.BlockSpec((1, D), const),
            pl.BlockSpec((None, D, N), lambda i: (0, 0, 0)),
            pl.BlockSpec((GLA_RANK, _GQK), const),
            pl.BlockSpec((1, _GQK), const),
        ],
        out_specs=(
            pl.BlockSpec((tm, _GQK), row), pl.BlockSpec((tm, _GQK), row), pl.BlockSpec((tm, _GV), row),
            pl.BlockSpec((tm, D), row), pl.BlockSpec((tm, _GQK), row)),
        out_shape=(
            jax.ShapeDtypeStruct((M, _GQK), F32), jax.ShapeDtypeStruct((M, _GQK), F32),
            jax.ShapeDtypeStruct((M, _GV), BF16), jax.ShapeDtypeStruct((M, D), F32),
            jax.ShapeDtypeStruct((M, _GQK), F32)),
        compiler_params=_params("arbitrary"),
    )(x, gain.reshape(1, D), w_in, w_a2.astype(BF16), b_a.reshape(1, _GQK))


def _split3(a):
    a0 = a.astype(BF16)
    r = a - a0.astype(F32)
    a1 = r.astype(BF16)
    a2 = (r - a1.astype(F32)).astype(BF16)
    return a0, a1, a2


def _gla_kernel(x_ref, q_ref, k_ref, v_ref, gate_ref, la_ref, s0_ref, ng_ref, wo_ref,
                out_ref, sout_ref, s_scr, a_scr, o_scr):
    c = pl.program_id(1)
    C = q_ref.shape[1]

    @pl.when(c == 0)
    def _():
        s_scr[...] = s0_ref[0]

    la = la_ref[0]
    row = lax.broadcasted_iota(I32, (C, C), 0)
    col = lax.broadcasted_iota(I32, (C, C), 1)
    tri = jnp.where(row >= col, 1.0, 0.0).astype(BF16)
    l0, l1, l2 = _split3(la)
    b = _dot(tri, l0) + _dot(tri, l1) + _dot(tri, l2)
    q = q_ref[0] * (GLA_DK ** -0.5)
    k = k_ref[0]
    trow = lax.broadcasted_iota(I32, (C, 1), 0)

    def block_ref(values, size, pick):
        v3 = values.reshape(C // size, size, values.shape[-1])
        return jnp.broadcast_to(v3[:, pick:pick + 1, :], v3.shape).reshape(values.shape)

    def accumulate(qh, kh, mask, first):
        for h in range(GLA_HEADS):
            sl = slice(h * GLA_DK, (h + 1) * GLA_DK)
            blk = _dot_nt(qh[:, sl], kh[:, sl])
            if mask is not None:
                blk = jnp.where(mask, blk, 0.0)
            if first:
                a_scr[h] = blk
            else:
                a_scr[h] = a_scr[h] + blk

    d = min(GLA_DIAG, C)
    before = block_ref(b - la, d, 0)
    qh = (q * jnp.exp(b - before)).astype(BF16)
    kh = (k * jnp.exp(before - b)).astype(BF16)
    sh = d.bit_length() - 1
    accumulate(qh, kh, jnp.logical_and((row >> sh) == (col >> sh), row >= col), True)
    half = d
    while half < C:
        upper = (trow & (2 * half - 1)) >= half
        split = block_ref(b, 2 * half, half - 1)
        qh = (q * jnp.exp(jnp.where(upper, b - split, -jnp.inf))).astype(BF16)
        kh = (k * jnp.exp(jnp.where(upper, -jnp.inf, split - b))).astype(BF16)
        sh = (2 * half).bit_length() - 1
        accumulate(qh, kh, (row >> sh) == (col >> sh) if 2 * half < C else None, False)
        half *= 2

    b_end = b[C - 1:C, :]
    q_in = (q * jnp.exp(b)).astype(BF16)
    k_out = k * jnp.exp(b_end - b)
    ng = ng_ref[...]
    for h in range(GLA_HEADS):
        sl = slice(h * GLA_DK, (h + 1) * GLA_DK)
        vh = v_ref[0, :, h * GLA_DV:(h + 1) * GLA_DV]
        s_old = s_scr[h]
        o = _dot(q_in[:, sl], s_old.astype(BF16)) + _dot(a_scr[h].astype(BF16), vh)
        dec = jnp.transpose(jnp.broadcast_to(jnp.exp(b_end[:, sl]), (GLA_DK, GLA_DK)))
        dec = jnp.concatenate([dec] * (GLA_DV // GLA_DK), axis=1)
        s_scr[h] = dec * s_old + _dot(jnp.transpose(k_out[:, sl]).astype(BF16), vh)
        o = _rms(o, ng)
        gt = gate_ref[0, :, h * GLA_DV:(h + 1) * GLA_DV]
        o_scr[:, h * GLA_DV:(h + 1) * GLA_DV] = (o * (gt / (1.0 + jnp.exp(-gt)))).astype(BF16)

    out_ref[0] = x_ref[0] + _dot(o_scr[...], wo_ref[...])

    @pl.when(c == pl.num_programs(1) - 1)
    def _():
        sout_ref[0] = s_scr[...]


def _gla(x, q, k, v, gate, la, s0, norm_g, w_o, *, chunk):
    B, T, D = x.shape
    n = T // chunk
    tok = lambda b, c: (b, c, 0)
    const = lambda b, c: (0, 0)
    st = lambda b, c: (b, 0, 0, 0)
    return pl.pallas_call(
        _gla_kernel, grid=(B, n), name="gla",
        in_specs=[
            pl.BlockSpec((1, chunk, D), tok),
            pl.BlockSpec((1, chunk, _GQK), tok),
            pl.BlockSpec((1, chunk, _GQK), tok),
            pl.BlockSpec((1, chunk, _GV), tok),
            pl.BlockSpec((1, chunk, D), tok),
            pl.BlockSpec((1, chunk, _GQK), tok),
            pl.BlockSpec((1, GLA_HEADS, GLA_DK, GLA_DV), st),
            pl.BlockSpec((1, GLA_DV), const),
            pl.BlockSpec((None, _GV, D), lambda b, c: (0, 0, 0)),
        ],
        out_specs=(pl.BlockSpec((1, chunk, D), tok), pl.BlockSpec((1, GLA_HEADS, GLA_DK, GLA_DV), st)),
        out_shape=(jax.ShapeDtypeStruct((B, T, D), F32),
                   jax.ShapeDtypeStruct((B, GLA_HEADS, GLA_DK, GLA_DV), F32)),
        scratch_shapes=[pltpu.VMEM((GLA_HEADS, GLA_DK, GLA_DV), F32),
                        pltpu.VMEM((GLA_HEADS, chunk, chunk), F32),
                        pltpu.VMEM((chunk, _GV), BF16)],
        compiler_params=_params("arbitrary", "arbitrary"),
    )(x, q, k, v, gate, la, s0, norm_g.reshape(1, GLA_DV), w_o)


def _pick_tile(n, candidates):
    for t in candidates:
        if n % t == 0:
            return t
    return n


def _round_up(n, m):
    return (n + m - 1) // m * m


def kernel(x_prompt, x_sample, cache_k, cache_v, cache_kidx, state_gla, norm_mix, norm_ffn, norm_final,
           a_w_in, a_w_o, b_w_in, b_w_a2, b_b_a, b_norm, b_w_o, ffn_w_up, ffn_w_down):
    B, S, D = x_prompt.shape
    Bs, Ts, _ = x_sample.shape
    past = cache_k.shape[2]
    Ms = Bs * Ts
    tq = DSA_Q_TILE
    assert S % tq == 0 and Ms == tq and Ts % SUBLANES == 0

    wu_bf, wd_bf, bwin_bf, bwo_bf, wo_a = _cast_bf16(ffn_w_up, ffn_w_down, b_w_in, b_w_o, a_w_o)
    tm = _pick_tile(S, (1024, 512, 256, 128))
    qT, qiT, wT, vT, k_p, v_p, ki_p, kg, kib = _dsa_project(x_prompt, norm_mix[0], a_w_in[0], tm=tm, tq=tq)
    pos = jnp.arange(S, dtype=I32)
    lim_p = ((pos // STREAM_CHUNK + 1) * STREAM_CHUNK).reshape(S // tq, 1, tq)
    L_p = _round_up(S, DSA_KEY_CHUNK)
    if L_p != S:
        pad = L_p - S
        kg = jnp.pad(kg, ((0, 0), (0, 0), (0, pad), (0, 0)))
        vT = jnp.pad(vT, ((0, 0), (0, 0), (0, 0), (0, pad)))
        kib = jnp.pad(kib, ((0, 0), (0, pad), (0, 0)))
    hp = _dsa_attend(x_prompt, qT, qiT, wT, lim_p, kg, vT, kib, wo_a,
                     topk=min(TOPK_MAX, S // 4), causal=True, shared_queries=False, rows_out=tq)
    xs = x_sample.reshape(1, Ms, D)
    qT_s, qiT_s, wT_s, _, k_s, v_s, ki_s, _, _ = _dsa_project(xs, norm_mix[0], a_w_in[0], tm=Ms, tq=tq)
    k_s = k_s.reshape(Bs, Ts, _NKV)
    v_s = v_s.reshape(Bs, Ts, _NKV)
    ki_s = ki_s.reshape(Bs, Ts, IDX_DIM)
    L_real = past + Ts
    L_s = _round_up(L_real, DSA_KEY_CHUNK)
    kg_s, vT_s, kib_s = _stage_sample_keys(
        cache_k[0].reshape(Bs, past, _NKV), cache_v[0].reshape(Bs, past, _NKV), cache_kidx[0],
        k_s, v_s, ki_s, L_s)
    pos_s = past + jnp.arange(Ts, dtype=I32)
    lim_s = jnp.minimum((pos_s // STREAM_CHUNK + 1) * STREAM_CHUNK, L_real)
    lim_s = jnp.tile(lim_s, Bs).reshape(1, 1, Ms)
    hs = _dsa_attend(xs, qT_s, qiT_s, wT_s, lim_s, kg_s, vT_s, kib_s, wo_a,
                     topk=min(TOPK_MAX, L_real // 4), causal=False, shared_queries=True, rows_out=Ts)

    tmm = _pick_tile(B * S, (1024, 512, 256, 128))
    hp = _mlp(hp.reshape(B * S, D), norm_ffn[0], wu_bf, wd_bf, 0, norm_final, tm=tmm, final_norm=False)
    hs = _mlp(hs.reshape(Ms, D), norm_ffn[0], wu_bf, wd_bf, 0, norm_final, tm=Ms, final_norm=False)

    q, k, v, gate, la = _gla_project(hp, norm_mix[1], bwin_bf, b_w_a2[0], b_b_a[0], tm=tmm)
    chunk = _pick_tile(S, (256, 128, 64))
    r3 = lambda a: a.reshape(B, S, a.shape[-1])
    s0 = jnp.zeros((B, GLA_HEADS, GLA_DK, GLA_DV), F32)
    hp, s_p = _gla(hp.reshape(B, S, D), r3(q), r3(k), r3(v), r3(gate), r3(la), s0, b_norm[0], bwo_bf, chunk=chunk)
    q, k, v, gate, la = _gla_project(hs, norm_mix[1], bwin_bf, b_w_a2[0], b_b_a[0], tm=Ms)
    r3 = lambda a: a.reshape(Bs, Ts, a.shape[-1])
    hs, s_s = _gla(hs.reshape(Bs, Ts, D), r3(q), r3(k), r3(v), r3(gate), r3(la), state_gla[0], b_norm[0], bwo_bf,
                   chunk=Ts)

    y_p = _mlp(hp.reshape(B * S, D), norm_ffn[1], wu_bf, wd_bf, 1, norm_final, tm=tmm, final_norm=True)
    y_s = _mlp(hs.reshape(Ms, D), norm_ffn[1], wu_bf, wd_bf, 1, norm_final, tm=Ms, final_norm=True)

    return (y_p.reshape(B, S, D), y_s.reshape(Bs, Ts, D),
            k_p.reshape(1, B, S, A_KV_HEADS, A_HEAD_DIM), v_p.reshape(1, B, S, A_KV_HEADS, A_HEAD_DIM),
            ki_p.reshape(1, B, S, IDX_DIM), s_p[None],
            k_s.reshape(1, Bs, Ts, A_KV_HEADS, A_HEAD_DIM), v_s.reshape(1, Bs, Ts, A_KV_HEADS, A_HEAD_DIM),
            ki_s.reshape(1, Bs, Ts, IDX_DIM), s_s[None])
```
